```python
import jax, jax.numpy as jnp
from jax import lax
import numpy as np

D_MODEL = 1024
BATCH = 8
SEQ = 4096
DEPTH = 2

HEAD_DIM = 64
A_HEADS = D_MODEL // HEAD_DIM
A_CONFIGS = ((128, 1), (512, 4), (2048, 16))
N_A_GROUPS = len(A_CONFIGS)
B_Q_HEADS = D_MODEL // HEAD_DIM
B_KV_HEADS = 2
B_GROUP = B_Q_HEADS // B_KV_HEADS
B_WINDOW = 128
BLOCK = 128
D_FF = 2816
N_EXPERTS = 8
TOP_K = 2
D_EXPERT = 3584
PLE_DIM = 256
ROPE_THETA = 10000.0
EPS = 1e-6
NEG_INF = -1e30
N_A_LAYERS = (DEPTH + 1) // 2
N_B_LAYERS = DEPTH // 2
N_DENSE = (DEPTH + 1) // 2
N_MOE = DEPTH // 2

kernel_name = "yoco_dilated_swa_sink_moe_ple"


def _rmsnorm(x, g):
    xf = x.astype(jnp.float32)
    y = xf * lax.rsqrt(jnp.mean(xf * xf, axis=-1, keepdims=True) + EPS)
    return (y * g.astype(jnp.float32)).astype(x.dtype)


def _rope(t, pos):
    half = t.shape[-1] // 2
    inv = 1.0 / (ROPE_THETA ** (jnp.arange(half, dtype=jnp.float32) / half))
    ang = pos[:, None] * inv[None, :]
    cos = jnp.cos(ang)[:, None, :]
    sin = jnp.sin(ang)[:, None, :]
    tf = t.astype(jnp.float32)
    t1, t2 = tf[..., :half], tf[..., half:]
    return jnp.concatenate([t1 * cos - t2 * sin, t2 * cos + t1 * sin], axis=-1).astype(t.dtype)


def _band_keys(t, blk_axis):
    prev = lax.slice_in_dim(t, 0, t.shape[blk_axis] - 1, axis=blk_axis)
    pad = [(0, 0)] * t.ndim
    pad[blk_axis] = (1, 0)
    prev = jnp.pad(prev, pad)
    return jnp.concatenate([prev, t], axis=blk_axis + 1)


def _band_mask(nb, max_dist):
    qi = jnp.arange(BLOCK)[:, None] + BLOCK
    kj = jnp.arange(2 * BLOCK)[None, :]
    rel = qi - kj
    key_idx = jnp.arange(nb)[:, None, None] * BLOCK + kj[None] - BLOCK
    return (rel >= 0)[None] & (rel <= max_dist)[None] & (key_idx >= 0)


def _dilated_group(q, k, v, dil, steps):
    B, S, H, Dh = q.shape
    n = S // dil
    nb = -(-n // BLOCK)
    n_pad = nb * BLOCK

    def to_sub(t):
        t = jnp.moveaxis(t.reshape(B, n, dil, H, Dh), 2, 1)
        t = jnp.pad(t, ((0, 0), (0, 0), (0, n_pad - n), (0, 0), (0, 0)))
        return t.reshape(B, dil, nb, BLOCK, H, Dh)

    def from_sub(t):
        tail = t.shape[4:]
        t = t.reshape((B, dil, n_pad) + tail)[:, :, :n]
        return jnp.moveaxis(t, 1, 2).reshape((B, S) + tail)

    qs = to_sub(q)
    ks = _band_keys(to_sub(k), 2)
    vs = _band_keys(to_sub(v), 2)
    s = jnp.einsum('brnqhd,brnkhd->brnhqk', qs, ks,
                   preferred_element_type=jnp.float32) * (Dh ** -0.5)
    mask = _band_mask(nb, steps)[None, None, :, None]
    s = jnp.where(mask, s, NEG_INF)
    m = jnp.max(s, axis=-1, keepdims=True)
    pexp = jnp.exp(s - m)
    den = jnp.sum(pexp, axis=-1)
    o = jnp.einsum('brnhqk,brnkhd->brnqhd', pexp, vs.astype(jnp.float32))
    den_t = jnp.swapaxes(den, -1, -2)
    o = o / den_t[..., None]
    lse = jnp.swapaxes(m[..., 0] + jnp.log(den), -1, -2)
    return from_sub(o), from_sub(lse)


def _dilated_mixer(h, w_qkv, w_o, pos):
    B, S, _ = h.shape
    qkv = (h @ w_qkv).reshape(B, S, N_A_GROUPS, 3, A_HEADS, HEAD_DIM)
    outs, lses = [], []
    for g, (win, dil) in enumerate(A_CONFIGS):
        q = _rope(qkv[:, :, g, 0], pos)
        k = _rope(qkv[:, :, g, 1], pos)
        o, lse = _dilated_group(q, k, qkv[:, :, g, 2], dil, win // dil)
        outs.append(o)
        lses.append(lse)
    wts = jax.nn.softmax(jnp.stack(lses), axis=0)
    o = jnp.sum(wts[..., None] * jnp.stack(outs), axis=0)
    return o.reshape(B, S, A_HEADS * HEAD_DIM).astype(h.dtype) @ w_o


def _shared_kv(x, g, w_kv, pos):
    B, S, _ = x.shape
    kv = (_rmsnorm(x, g) @ w_kv).reshape(B, S, 2, B_KV_HEADS, HEAD_DIM)
    return _rope(kv[:, :, 0], pos), kv[:, :, 1]


def _swa_sink_mixer(h, k, v, w_q, sinks, w_o, pos):
    B, S, _ = h.shape
    nb = S // BLOCK
    q = _rope((h @ w_q).reshape(B, S, B_Q_HEADS, HEAD_DIM), pos)
    qb = q.reshape(B, nb, BLOCK, B_KV_HEADS, B_GROUP, HEAD_DIM)
    kb = _band_keys(k.reshape(B, nb, BLOCK, B_KV_HEADS, HEAD_DIM), 1)
    vb = _band_keys(v.reshape(B, nb, BLOCK, B_KV_HEADS, HEAD_DIM), 1)
    s = jnp.einsum('bnqhgd,bnkhd->bnhgqk', qb, kb,
                   preferred_element_type=jnp.float32) * (HEAD_DIM ** -0.5)
    mask = _band_mask(nb, B_WINDOW - 1)[None, :, None, None]
    s = jnp.where(mask, s, NEG_INF)
    sink = sinks.astype(jnp.float32).reshape(1, 1, B_KV_HEADS, B_GROUP, 1, 1)
    m = jnp.maximum(jnp.max(s, axis=-1, keepdims=True), sink)
    pexp = jnp.exp(s - m)
    den = jnp.sum(pexp, axis=-1, keepdims=True) + jnp.exp(sink - m)
    o = jnp.einsum('bnhgqk,bnkhd->bnqhgd', pexp / den, vb.astype(jnp.float32))
    return o.reshape(B, S, B_Q_HEADS * HEAD_DIM).astype(h.dtype) @ w_o


def _swiglu(h, w1, w3, w2):
    return (jax.nn.silu(h @ w1) * (h @ w3)) @ w2


def _moe(h, w_router, w1, w3, w2):
    logits = (h @ w_router).astype(jnp.float32)
    top_v, top_i = lax.top_k(logits, TOP_K)
    wts = jax.nn.softmax(top_v, axis=-1)
    gates = jnp.einsum('bske,bsk->bse',
                       jax.nn.one_hot(top_i, N_EXPERTS, dtype=jnp.float32), wts).astype(h.dtype)
    y = jnp.zeros_like(h)
    for e in range(N_EXPERTS):
        y = y + gates[..., e:e + 1] * _swiglu(h, w1[e], w3[e], w2[e])
    return y


def setup_inputs(seed: int = 0) -> dict:
    key = jax.random.key(seed)
    ks = iter(jax.random.split(key, 32))
    f32 = jnp.float32

    def w(shape, fan_in):
        return jax.random.normal(next(ks), shape, f32) * (fan_in ** -0.5)

    def gain(shape):
        return 1.0 + 0.05 * jax.random.normal(next(ks), shape, f32)

    qkv_cols = N_A_GROUPS * 3 * A_HEADS * HEAD_DIM
    return {
        "x": jax.random.normal(next(ks), (BATCH, SEQ, D_MODEL), f32),
        "p": jax.random.normal(next(ks), (DEPTH, BATCH, SEQ, PLE_DIM), f32),
        "attn_norm": gain((DEPTH, D_MODEL)),
        "ffn_norm": gain((DEPTH, D_MODEL)),
        "a_w_qkv": w((N_A_LAYERS, D_MODEL, qkv_cols), D_MODEL),
        "a_w_o": w((N_A_LAYERS, A_HEADS * HEAD_DIM, D_MODEL), A_HEADS * HEAD_DIM),
        "kv_norm": gain((D_MODEL,)),
        "kv_w": w((D_MODEL, 2 * B_KV_HEADS * HEAD_DIM), D_MODEL),
        "b_w_q": w((N_B_LAYERS, D_MODEL, B_Q_HEADS * HEAD_DIM), D_MODEL),
        "b_sinks": jax.random.normal(next(ks), (N_B_LAYERS, B_Q_HEADS), f32),
        "b_w_o": w((N_B_LAYERS, B_Q_HEADS * HEAD_DIM, D_MODEL), B_Q_HEADS * HEAD_DIM),
        "dense_w1": w((N_DENSE, D_MODEL, D_FF), D_MODEL),
        "dense_w3": w((N_DENSE, D_MODEL, D_FF), D_MODEL),
        "dense_w2": w((N_DENSE, D_FF, D_MODEL), D_FF),
        "moe_router": w((N_MOE, D_MODEL, N_EXPERTS), D_MODEL),
        "moe_w1": w((N_MOE, N_EXPERTS, D_MODEL, D_EXPERT), D_MODEL),
        "moe_w3": w((N_MOE, N_EXPERTS, D_MODEL, D_EXPERT), D_MODEL),
        "moe_w2": w((N_MOE, N_EXPERTS, D_EXPERT, D_MODEL), D_EXPERT),
        "ple_norm": gain((DEPTH, D_MODEL)),
        "ple_w_gate": w((DEPTH, D_MODEL, D_MODEL), D_MODEL),
        "ple_w_proj": w((DEPTH, PLE_DIM, D_MODEL), PLE_DIM),
        "final_norm": gain((D_MODEL,)),
    }


def reference(x, p, attn_norm, ffn_norm, a_w_qkv, a_w_o, kv_norm, kv_w, b_w_q, b_sinks,
              b_w_o, dense_w1, dense_w3, dense_w2, moe_router, moe_w1, moe_w3, moe_w2,
              ple_norm, ple_w_gate, ple_w_proj, final_norm):
    S = x.shape[1]
    pos = jnp.arange(S, dtype=jnp.float32)
    k_shared = v_shared = None
    for i in range(DEPTH):
        h = _rmsnorm(x, attn_norm[i])
        if i < N_A_LAYERS:
            x = x + _dilated_mixer(h, a_w_qkv[i], a_w_o[i], pos)
        else:
            j = i - N_A_LAYERS
            if j == 0:
                k_shared, v_shared = _shared_kv(x, kv_norm, kv_w, pos)
                h = _rmsnorm(x, attn_norm[i])
            x = x + _swa_sink_mixer(h, k_shared, v_shared, b_w_q[j], b_sinks[j], b_w_o[j], pos)
        h = _rmsnorm(x, ffn_norm[i])
        if i % 2 == 0:
            x = x + _swiglu(h, dense_w1[i // 2], dense_w3[i // 2], dense_w2[i // 2])
        else:
            x = x + _moe(h, moe_router[i // 2], moe_w1[i // 2], moe_w3[i // 2], moe_w2[i // 2])
        gate = jax.nn.sigmoid(_rmsnorm(x, ple_norm[i]) @ ple_w_gate[i])
        x = x + (p[i].astype(x.dtype) @ ple_w_proj[i]) * gate
    return _rmsnorm(x, final_norm)
```

```python
import functools

import numpy as np
import jax
import jax.numpy as jnp
from jax import lax
from jax.experimental import pallas as pl
from jax.experimental.pallas import tpu as pltpu

F32 = jnp.float32
BF16 = jnp.bfloat16

LANES = 128
HEAD_DIM = 64
HALF_DIM = HEAD_DIM // 2
BLOCK = 128
A_CONFIGS = ((128, 1), (512, 4), (2048, 16))
B_KV_HEADS = 2
B_WINDOW = 128
N_EXPERTS = 8
ROPE_THETA = 10000.0
EPS = 1e-6
NEG_INF = -1e30
VMEM_LIMIT = 56 * 1024 * 1024

PLAIN, ROPE, ROPE_SCALED = 0, 1, 2
Q_SCALE = HEAD_DIM ** -0.5


def _params(*sem):
    return pltpu.CompilerParams(dimension_semantics=sem, vmem_limit_bytes=VMEM_LIMIT)


def _rms(x, g):
    ms = jnp.mean(x * x, axis=-1, keepdims=True)
    return x * lax.rsqrt(ms + EPS) * g


def _pair_perm(pairs):
    idx = []
    for ha, hb in pairs:
        for base in (0, HALF_DIM):
            idx += [ha * HEAD_DIM + base + d for d in range(HALF_DIM)]
            idx += [hb * HEAD_DIM + base + d for d in range(HALF_DIM)]
    return np.asarray(idx, np.int32)


def _rope_tables(seq):
    pos = jnp.arange(seq, dtype=F32)
    inv = 1.0 / (ROPE_THETA ** (jnp.arange(HALF_DIM, dtype=F32) / HALF_DIM))
    ang = pos[:, None] * inv[None, :]
    cos, sin = jnp.cos(ang), jnp.sin(ang)
    return jnp.tile(cos, (1, 4)), jnp.concatenate([-sin, -sin, sin, sin], axis=1)


def _store_chunks(acc, cos, sin, o_ref, kinds):
    for c, kind in enumerate(kinds):
        seg = acc[:, c * LANES:(c + 1) * LANES]
        if kind != PLAIN:
            seg = seg * cos + pltpu.roll(seg, LANES // 2, 1) * sin
            if kind == ROPE_SCALED:
                seg = seg * Q_SCALE
        o_ref[:, c * LANES:(c + 1) * LANES] = seg.astype(o_ref.dtype)


def _norm_proj_kernel(x_ref, g_ref, w_ref, cos_ref, sin_ref, o_ref, h_ref, *, n_q, n_k):
    j = pl.program_id(1)

    @pl.when(j == 0)
    def _():
        h_ref[...] = _rms(x_ref[...], g_ref[...]).astype(BF16)

    acc = jnp.dot(h_ref[...], w_ref[...], preferred_element_type=F32)
    n_chunks = acc.shape[1] // LANES

    @pl.when(j < n_q)
    def _():
        _store_chunks(acc, cos_ref[...], sin_ref[...], o_ref, (ROPE_SCALED,) * n_chunks)

    @pl.when((j >= n_q) & (j < n_q + n_k))
    def _():
        _store_chunks(acc, cos_ref[...], sin_ref[...], o_ref, (ROPE,) * n_chunks)

    @pl.when(j >= n_q + n_k)
    def _():
        _store_chunks(acc, None, None, o_ref, (PLAIN,) * n_chunks)


def _norm_proj(x, g, w, cos, sin, *, n_q, n_k, tm, tn):
    t, d = x.shape
    n = w.shape[1]
    s_tiles = cos.shape[0] // tm
    return pl.pallas_call(
        functools.partial(_norm_proj_kernel, n_q=n_q, n_k=n_k),
        grid=(t // tm, n // tn),
        in_specs=[
            pl.BlockSpec((tm, d), lambda i, j: (i, 0)),
            pl.BlockSpec((1, d), lambda i, j: (0, 0)),
            pl.BlockSpec((d, tn), lambda i, j: (0, j)),
            pl.BlockSpec((tm, LANES), lambda i, j: (i % s_tiles, 0)),
            pl.BlockSpec((tm, LANES), lambda i, j: (i % s_tiles, 0)),
        ],
        out_specs=pl.BlockSpec((tm, tn), lambda i, j: (i, j)),
        out_shape=jax.ShapeDtypeStruct((t, n), BF16),
        scratch_shapes=[pltpu.VMEM((tm, d), BF16)],
        compiler_params=_params("parallel", "arbitrary"),
        name="norm_proj",
    )(x, g, w, cos, sin)


def _qkv1_kernel(x_ref, gq_ref, gkv_ref, wq_ref, wkv_ref, cos_ref, sin_ref, q_ref, kv_ref):
    x = x_ref[...]
    xn = x * lax.rsqrt(jnp.mean(x * x, axis=-1, keepdims=True) + EPS)
    hq = (xn * gq_ref[...]).astype(BF16)
    hkv = (xn * gkv_ref[...]).astype(BF16)
    cos, sin = cos_ref[...], sin_ref[...]
    accq = jnp.dot(hq, wq_ref[...], preferred_element_type=F32)
    _store_chunks(accq, cos, sin, q_ref, (ROPE_SCALED,) * (accq.shape[1] // LANES))
    acckv = jnp.dot(hkv, wkv_ref[...], preferred_element_type=F32)
    _store_chunks(acckv, cos, sin, kv_ref, (ROPE, PLAIN))


def _qkv1(x, gq, gkv, wq, wkv, cos, sin, *, tm):
    t, d = x.shape
    s_tiles = cos.shape[0] // tm
    row = lambda i: (i, 0)
    fixed = lambda i: (0, 0)
    return pl.pallas_call(
        _qkv1_kernel,
        grid=(t // tm,),
        in_specs=[
            pl.BlockSpec((tm, d), row),
            pl.BlockSpec((1, d), fixed),
            pl.BlockSpec((1, d), fixed),
            pl.BlockSpec(wq.shape, fixed),
            pl.BlockSpec(wkv.shape, fixed),
            pl.BlockSpec((tm, LANES), lambda i: (i % s_tiles, 0)),
            pl.BlockSpec((tm, LANES), lambda i: (i % s_tiles, 0)),
        ],
        out_specs=[pl.BlockSpec((tm, wq.shape[1]), row), pl.BlockSpec((tm, wkv.shape[1]), row)],
        out_shape=[jax.ShapeDtypeStruct((t, wq.shape[1]), BF16),
                   jax.ShapeDtypeStruct((t, wkv.shape[1]), BF16)],
        compiler_params=_params("parallel"),
        name="qkv_layer1",
    )(x, gq, gkv, wq, wkv, cos, sin)


def _band_mask(blk, max_dist):
    qi = lax.broadcasted_iota(jnp.int32, (BLOCK, 2 * BLOCK), 0) + BLOCK
    kj = lax.broadcasted_iota(jnp.int32, (BLOCK, 2 * BLOCK), 1)
    rel = qi - kj
    mask = (rel >= 0) & (rel <= max_dist) & ((kj >= BLOCK) | (blk > 0))
    return jnp.concatenate([mask, mask], axis=0)


def _lane_masks():
    lane = lax.broadcasted_iota(jnp.int32, (BLOCK, LANES), 1)
    first_qk = (lane % HEAD_DIM) < HALF_DIM
    sel_a = jnp.where(first_qk, 1.0, 0.0).astype(BF16)
    sel_b = jnp.where(first_qk, 0.0, 1.0).astype(BF16)
    return lane, sel_a, sel_b, lane < HEAD_DIM


def _pair_scores(q2, kcat, sel_a, sel_b, mask2):
    qs = jnp.concatenate([q2 * sel_a, q2 * sel_b], axis=0)
    s = lax.dot_general(qs, kcat, (((1,), (1,)), ((), ())), preferred_element_type=F32)
    return jnp.where(mask2, s, NEG_INF)


def _dil_attn_kernel(q_ref, kp_ref, kc_ref, vp_ref, vc_ref, o_ref, st_ref, *, steps):
    mask2 = _band_mask(pl.program_id(2), steps)
    lane, sel_a, sel_b, first_v = _lane_masks()
    stats = jnp.zeros((BLOCK, LANES), F32)
    for p in range(q_ref.shape[1] // LANES):
        sl = slice(p * LANES, (p + 1) * LANES)
        kcat = jnp.concatenate([kp_ref[:, sl], kc_ref[:, sl]], axis=0)
        vcat = jnp.concatenate([vp_ref[:, sl], vc_ref[:, sl]], axis=0)
        s = _pair_scores(q_ref[:, sl], kcat, sel_a, sel_b, mask2)
        m = jnp.max(s, axis=-1, keepdims=True)
        pe = jnp.exp(s - m)
        den = jnp.sum(pe, axis=-1, keepdims=True)
        pv = jnp.dot(pe.astype(BF16), vcat, preferred_element_type=F32)
        pv = pv * (1.0 / den)
        o_ref[:, sl] = jnp.where(first_v, pv[:BLOCK], pv[BLOCK:]).astype(o_ref.dtype)
        lse = m + jnp.log(den)
        stats = jnp.where(lane == 2 * p, lse[:BLOCK], stats)
        stats = jnp.where(lane == 2 * p + 1, lse[BLOCK:], stats)
    st_ref[...] = stats


def _dilated_group(qkv, batch, seq, g, n_groups):
    win, dil = A_CONFIGS[g]
    d = qkv.shape[1] // (3 * n_groups)
    n = seq // dil
    nb = n // BLOCK
    cols = 3 * n_groups
    view = qkv.reshape(batch, n, dil * cols * d)

    def spec(kind, prev):
        def index(b, r, blk):
            return (b, jnp.maximum(blk - 1, 0) if prev else blk, r * cols + kind * n_groups + g)
        return pl.BlockSpec((None, BLOCK, d), index)

    o, st = pl.pallas_call(
        functools.partial(_dil_attn_kernel, steps=win // dil),
        grid=(batch, dil, nb),
        in_specs=[spec(0, False), spec(1, True), spec(1, False), spec(2, True), spec(2, False)],
        out_specs=[pl.BlockSpec((None, BLOCK, d), lambda b, r, blk: (b, blk, r)),
                   pl.BlockSpec((None, BLOCK, LANES), lambda b, r, blk: (b, blk, r))],
        out_shape=[jax.ShapeDtypeStruct((batch, n, dil * d), BF16),
                   jax.ShapeDtypeStruct((batch, n, dil * LANES), F32)],
        compiler_params=_params("parallel", "parallel", "arbitrary"),
        name=f"dilated_attn_g{g}",
    )(view, view, view, view, view)
    return o.reshape(batch * seq, d), st.reshape(batch * seq, LANES)


def _swa_kernel(sink_ref, q_ref, kp_ref, kc_ref, vp_ref, vc_ref, o_ref):
    mask2 = _band_mask(pl.program_id(1), B_WINDOW - 1)
    _, sel_a, sel_b, first_v = _lane_masks()
    kcat = jnp.concatenate([kp_ref[...], kc_ref[...]], axis=0)
    vcat = jnp.concatenate([vp_ref[...], vc_ref[...]], axis=0)
    n_tiles = q_ref.shape[1] // LANES
    for j in range(n_tiles):
        sl = slice(j * LANES, (j + 1) * LANES)
        s = _pair_scores(q_ref[:, sl], kcat, sel_a, sel_b, mask2)
        sink = jnp.concatenate([jnp.full((BLOCK, 1), sink_ref[j], F32),
                                jnp.full((BLOCK, 1), sink_ref[n_tiles + j], F32)], axis=0)
        m = jnp.maximum(jnp.max(s, axis=-1, keepdims=True), sink)
        pe = jnp.exp(s - m)
        den = jnp.sum(pe, axis=-1, keepdims=True) + jnp.exp(sink - m)
        pv = jnp.dot(pe.astype(BF16), vcat, preferred_element_type=F32)
        pv = pv * (1.0 / den)
        o_ref[:, sl] = jnp.where(first_v, pv[:BLOCK], pv[BLOCK:]).astype(o_ref.dtype)


def _swa_attention(q, kv, sinks, batch, seq):
    d = q.shape[1]
    nb = seq // BLOCK
    qv = q.reshape(batch, seq, d)
    kvv = kv.reshape(batch, seq, 2 * LANES)

    def kv_spec(col, prev):
        def index(b, blk):
            return (b, jnp.maximum(blk - 1, 0) if prev else blk, col)
        return pl.BlockSpec((None, BLOCK, LANES), index)

    o = pl.pallas_call(
        _swa_kernel,
        grid=(batch, nb),
        in_specs=[pl.BlockSpec(memory_space=pltpu.SMEM),
                  pl.BlockSpec((None, BLOCK, d), lambda b, blk: (b, blk, 0)),
                  kv_spec(0, True), kv_spec(0, False), kv_spec(1, True), kv_spec(1, False)],
        out_specs=pl.BlockSpec((None, BLOCK, d), lambda b, blk: (b, blk, 0)),
        out_shape=jax.ShapeDtypeStruct((batch, seq, d), BF16),
        compiler_params=_params("parallel", "arbitrary"),
        name="swa_sink_attn",
    )(sinks, qv, kvv, kvv, kvv, kvv)
    return o.reshape(batch * seq, d)


def _merge_wo_kernel(x_ref, o0_ref, o1_ref, o2_ref, s0_ref, s1_ref, s2_ref, e_ref, w_ref, out_ref):
    s0, s1, s2 = s0_ref[...], s1_ref[...], s2_ref[...]
    top = jnp.maximum(jnp.maximum(s0, s1), s2)
    e0, e1, e2 = jnp.exp(s0 - top), jnp.exp(s1 - top), jnp.exp(s2 - top)
    inv = 1.0 / (e0 + e1 + e2)
    expand = e_ref[...]
    merged = None
    for e, o_ref in ((e0, o0_ref), (e1, o1_ref), (e2, o2_ref)):
        wt = jnp.dot((e * inv).astype(BF16), expand, preferred_element_type=F32)
        term = wt * o_ref[...].astype(F32)
        merged = term if merged is None else merged + term
    out_ref[...] = x_ref[...] + jnp.dot(merged.astype(BF16), w_ref[...], preferred_element_type=F32)


def _merge_wo(x, outs, stats, expand, w, *, tm):
    t, d = x.shape
    row = lambda i: (i, 0)
    fixed = lambda i: (0, 0)
    return pl.pallas_call(
        _merge_wo_kernel,
        grid=(t // tm,),
        in_specs=[pl.BlockSpec((tm, d), row)]
        + [pl.BlockSpec((tm, d), row)] * 3
        + [pl.BlockSpec((tm, LANES), row)] * 3
        + [pl.BlockSpec(expand.shape, fixed), pl.BlockSpec(w.shape, fixed)],
        out_specs=pl.BlockSpec((tm, d), row),
        out_shape=jax.ShapeDtypeStruct((t, d), F32),
        compiler_params=_params("parallel"),
        name="merge_wo",
    )(x, *outs, *stats, expand, w)


def _wo_kernel(x_ref, o_ref, w_ref, out_ref):
    out_ref[...] = x_ref[...] + jnp.dot(o_ref[...], w_ref[...], preferred_element_type=F32)


def _wo(x, o, w, *, tm):
    t, d = x.shape
    row = lambda i: (i, 0)
    return pl.pallas_call(
        _wo_kernel,
        grid=(t // tm,),
        in_specs=[pl.BlockSpec((tm, d), row), pl.BlockSpec((tm, o.shape[1]), row),
                  pl.BlockSpec(w.shape, lambda i: (0, 0))],
        out_specs=pl.BlockSpec((tm, d), row),
        out_shape=jax.ShapeDtypeStruct((t, d), F32),
        compiler_params=_params("parallel"),
        name="wo_residual",
    )(x, o, w)


def _swiglu_step(h, w1_ref, w3_ref, w2_ref, acc_ref):
    a = jnp.dot(h, w1_ref[...], preferred_element_type=F32)
    b = jnp.dot(h, w3_ref[...], preferred_element_type=F32)
    act = (a * jax.nn.sigmoid(a) * b).astype(BF16)
    acc_ref[...] += jnp.dot(act, w2_ref[...], preferred_element_type=F32)


def _ffn_kernel(x_ref, g_ref, w1_ref, w3_ref, w2_ref, o_ref, h_ref, acc_ref):
    j = pl.program_id(1)

    @pl.when(j == 0)
    def _():
        h_ref[...] = _rms(x_ref[...], g_ref[...]).astype(BF16)
        acc_ref[...] = jnp.zeros_like(acc_ref)

    _swiglu_step(h_ref[...], w1_ref, w3_ref, w2_ref, acc_ref)

    @pl.when(j == pl.num_programs(1) - 1)
    def _():
        o_ref[...] = x_ref[...] + acc_ref[...]


def _ffn(x, g, w1, w3, w2, *, tm, tf):
    t, d = x.shape
    f = w1.shape[1]
    return pl.pallas_call(
        _ffn_kernel,
        grid=(t // tm, f // tf),
        in_specs=[pl.BlockSpec((tm, d), lambda i, j: (i, 0)),
                  pl.BlockSpec((1, d), lambda i, j: (0, 0)),
                  pl.BlockSpec((d, tf), lambda i, j: (0, j)),
                  pl.BlockSpec((d, tf), lambda i, j: (0, j)),
                  pl.BlockSpec((tf, d), lambda i, j: (j, 0))],
        out_specs=pl.BlockSpec((tm, d), lambda i, j: (i, 0)),
        out_shape=jax.ShapeDtypeStruct((t, d), F32),
        scratch_shapes=[pltpu.VMEM((tm, d), BF16), pltpu.VMEM((tm, d), F32)],
        compiler_params=_params("parallel", "arbitrary"),
        name="dense_swiglu",
    )(x, g, w1, w3, w2)


def _moe_kernel(te_ref, nu_ref, xs_ref, g_ref, w1_ref, w3_ref, w2_ref, o_ref, h_ref, acc_ref):
    i, j = pl.program_id(0), pl.program_id(1)
    used = i < nu_ref[0]
    last = j == pl.num_programs(1) - 1

    @pl.when(used & (j == 0))
    def _():
        h_ref[...] = _rms(xs_ref[...], g_ref[...]).astype(BF16)
        acc_ref[...] = jnp.zeros_like(acc_ref)

    @pl.when(used)
    def _():
        _swiglu_step(h_ref[...], w1_ref, w3_ref, w2_ref, acc_ref)

    @pl.when(used & last)
    def _():
        o_ref[...] = acc_ref[...]

    @pl.when(jnp.logical_not(used) & last)
    def _():
        o_ref[...] = jnp.zeros_like(o_ref)


def _moe_experts(xs, g, w1, w3, w2, tile_expert, n_used, *, tm, tf):
    p, d = xs.shape
    f = w1.shape[2]
    n_j = f // tf

    def jj(i, j, nu):
        return jnp.where(i < nu[0], j, n_j - 1)

    grid_spec = pltpu.PrefetchScalarGridSpec(
        num_scalar_prefetch=2,
        grid=(p // tm, n_j),
        in_specs=[pl.BlockSpec((tm, d), lambda i, j, te, nu: (i, 0)),
                  pl.BlockSpec((1, d), lambda i, j, te, nu: (0, 0)),
                  pl.BlockSpec((None, d, tf), lambda i, j, te, nu: (te[i], 0, jj(i, j, nu))),
                  pl.BlockSpec((None, d, tf), lambda i, j, te, nu: (te[i], 0, jj(i, j, nu))),
                  pl.BlockSpec((None, tf, d), lambda i, j, te, nu: (te[i], jj(i, j, nu), 0))],
        out_specs=pl.BlockSpec((tm, d), lambda i, j, te, nu: (i, 0)),
        scratch_shapes=[pltpu.VMEM((tm, d), BF16), pltpu.VMEM((tm, d), F32)],
    )
    return pl.pallas_call(
        _moe_kernel,
        grid_spec=grid_spec,
        out_shape=jax.ShapeDtypeStruct((p, d), F32),
        compiler_params=_params("arbitrary", "arbitrary"),
        name="moe_experts",
    )(tile_expert, n_used, xs, g, w1, w3, w2)


def _router_kernel(x_ref, g_ref, wr_ref, o_ref):
    h = _rms(x_ref[...], g_ref[...]).astype(BF16)
    logits = jnp.dot(h, wr_ref[...], preferred_element_type=F32)
    lane = lax.broadcasted_iota(jnp.int32, logits.shape, 1)
    logits = jnp.where(lane < N_EXPERTS, logits, -jnp.inf)
    v1 = jnp.max(logits, axis=-1, keepdims=True)
    i1 = jnp.min(jnp.where(logits == v1, lane, LANES), axis=-1, keepdims=True)
    rest = jnp.where(lane == i1, -jnp.inf, logits)
    v2 = jnp.max(rest, axis=-1, keepdims=True)
    i2 = jnp.min(jnp.where(rest == v2, lane, LANES), axis=-1, keepdims=True)
    e2 = jnp.exp(v2 - v1)
    inv = 1.0 / (1.0 + e2)
    out = jnp.where(lane == 0, i1.astype(F32), 0.0)
    out = jnp.where(lane == 1, i2.astype(F32), out)
    out = jnp.where(lane == 2, inv, out)
    out = jnp.where(lane == 3, e2 * inv, out)
    o_ref[...] = out


def _router(x, g, wr, *, tm):
    t, d = x.shape
    return pl.pallas_call(
        _router_kernel,
        grid=(t // tm,),
        in_specs=[pl.BlockSpec((tm, d), lambda i: (i, 0)),
                  pl.BlockSpec((1, d), lambda i: (0, 0)),
                  pl.BlockSpec(wr.shape, lambda i: (0, 0))],
        out_specs=pl.BlockSpec((tm, LANES), lambda i: (i, 0)),
        out_shape=jax.ShapeDtypeStruct((t, LANES), F32),
        compiler_params=_params("parallel"),
        name="router_top2",
    )(x, g, wr)


def _row_copy(src_hbm, dst_ref, sem, src_row, dst_row):
    return pltpu.make_async_copy(src_hbm.at[pl.ds(src_row, 1)], dst_ref.at[pl.ds(dst_row, 1)], sem)


def _gather_kernel(idx_ref, src_hbm, o_ref, sem):
    rows = o_ref.shape[0]

    def issue(r, carry):
        _row_copy(src_hbm, o_ref, sem, idx_ref[0, 0, r], r).start()
        return carry

    lax.fori_loop(0, rows, issue, 0)
    pltpu.make_async_copy(src_hbm.at[pl.ds(0, rows)], o_ref, sem).wait()


def _gather_rows(src, idx, *, tg):
    p = idx.shape[0]
    d = src.shape[1]
    return pl.pallas_call(
        _gather_kernel,
        grid=(p // tg,),
        in_specs=[pl.BlockSpec((1, 1, tg), lambda i: (i, 0, 0), memory_space=pltpu.SMEM),
                  pl.BlockSpec(memory_space=pl.ANY)],
        out_specs=pl.BlockSpec((tg, d), lambda i: (i, 0)),
        out_shape=jax.ShapeDtypeStruct((p, d), src.dtype),
        scratch_shapes=[pltpu.SemaphoreType.DMA(())],
        compiler_params=_params("arbitrary"),
        name="gather_rows",
    )(idx.reshape(p // tg, 1, tg), src)


def _combine_kernel(slot_ref, x_ref, route_ref, ys_hbm, o_ref, a_ref, b_ref, sem):
    rows = o_ref.shape[0]

    def issue(r, carry):
        _row_copy(ys_hbm, a_ref, sem, slot_ref[0, 0, 2 * r], r).start()
        _row_copy(ys_hbm, b_ref, sem, slot_ref[0, 0, 2 * r + 1], r).start()
        return carry

    lax.fori_loop(0, rows, issue, 0)
    pltpu.make_async_copy(ys_hbm.at[pl.ds(0, rows)], a_ref, sem).wait()
    pltpu.make_async_copy(ys_hbm.at[pl.ds(0, rows)], b_ref, sem).wait()
    route = route_ref[...]
    y = route[:, 2:3] * a_ref[...] + route[:, 3:4] * b_ref[...]
    o_ref[...] = x_ref[...] + y


def _combine(x, route, ys, slots, *, tm):
    t, d = x.shape
    row = lambda i: (i, 0)
    return pl.pallas_call(
        _combine_kernel,
        grid=(t // tm,),
        in_specs=[pl.BlockSpec((1, 1, 2 * tm), lambda i: (i, 0, 0), memory_space=pltpu.SMEM),
                  pl.BlockSpec((tm, d), row),
                  pl.BlockSpec((tm, LANES), row),
                  pl.BlockSpec(memory_space=pl.ANY)],
        out_specs=pl.BlockSpec((tm, d), row),
        out_shape=jax.ShapeDtypeStruct((t, d), F32),
        scratch_shapes=[pltpu.VMEM((tm, d), F32), pltpu.VMEM((tm, d), F32),
                        pltpu.SemaphoreType.DMA(())],
        compiler_params=_params("arbitrary"),
        name="moe_combine",
    )(slots.reshape(t // tm, 1, 2 * tm), x, route, ys)


def _ple_kernel(x_ref, p_ref, g_ref, wg_ref, wp_ref, fg_ref, o_ref, *, final):
    x = x_ref[...]
    h = _rms(x, g_ref[...]).astype(BF16)
    gate = jax.nn.sigmoid(jnp.dot(h, wg_ref[...], preferred_element_type=F32))
    proj = jnp.dot(p_ref[...].astype(BF16), wp_ref[...], preferred_element_type=F32)
    y = x + proj * gate
    if final:
        y = _rms(y, fg_ref[...])
    o_ref[...] = y


def _ple(x, p, layer, g, wg, wp, fg, *, final, tm):
    t, d = x.shape
    fixed = lambda i: (0, 0)
    return pl.pallas_call(
        functools.partial(_ple_kernel, final=final),
        grid=(t // tm,),
        in_specs=[pl.BlockSpec((tm, d), lambda i: (i, 0)),
                  pl.BlockSpec((None, tm, p.shape[2]), lambda i: (layer, i, 0)),
                  pl.BlockSpec((1, d), fixed),
                  pl.BlockSpec(wg.shape, fixed),
                  pl.BlockSpec(wp.shape, fixed),
                  pl.BlockSpec((1, d), fixed)],
        out_specs=pl.BlockSpec((tm, d), lambda i: (i, 0)),
        out_shape=jax.ShapeDtypeStruct((t, d), F32),
        compiler_params=_params("parallel"),
        name="ple_final" if final else "ple",
    )(x, p, g, wg, wp, fg)


def _routing_tables(route, tm):
    t = route.shape[0]
    ids = route[:, :2].astype(jnp.int32)
    flat = ids.reshape(-1)
    onehot = (flat[:, None] == jnp.arange(N_EXPERTS)[None, :]).astype(jnp.int32)
    csum = jnp.cumsum(onehot, axis=0)
    rank = jnp.take_along_axis(csum, flat[:, None], axis=1)[:, 0] - 1
    counts = csum[-1]
    tiles_per = (counts + tm - 1) // tm
    tile_end = jnp.cumsum(tiles_per)
    tile_start = tile_end - tiles_per
    dest = tile_start[flat] * tm + rank
    n_tiles = (2 * t) // tm + N_EXPERTS
    n_used = tile_end[-1]
    tile_ids = jnp.minimum(jnp.arange(n_tiles), n_used - 1)
    tile_expert = jnp.minimum(jnp.searchsorted(tile_end, tile_ids, side="right"),
                              N_EXPERTS - 1).astype(jnp.int32)
    row_token = jnp.zeros((n_tiles * tm,), jnp.int32).at[dest].set(
        jnp.arange(2 * t, dtype=jnp.int32) // 2, unique_indices=True)
    return row_token, dest.astype(jnp.int32), tile_expert, n_used.reshape(1).astype(jnp.int32)


def kernel(x, p, attn_norm, ffn_norm, a_w_qkv, a_w_o, kv_norm, kv_w, b_w_q, b_sinks, b_w_o,
           dense_w1, dense_w3, dense_w2, moe_router, moe_w1, moe_w3, moe_w2,
           ple_norm, ple_w_gate, ple_w_proj, final_norm):
    batch, seq, d = x.shape
    t = batch * seq
    n_heads = d // HEAD_DIM
    n_groups = len(A_CONFIGS)
    tm = 512
    xr = x.reshape(t, d)
    pr = p.reshape(p.shape[0], t, p.shape[3])
    cos, sin = _rope_tables(seq)
    row = lambda v: v.reshape(1, -1)

    perm_a = _pair_perm([(2 * j, 2 * j + 1) for j in range(n_heads // 2)])
    w = a_w_qkv[0].reshape(d, n_groups, 3, d)
    w = jnp.concatenate([w[:, :, 0][:, :, perm_a], w[:, :, 1][:, :, perm_a], w[:, :, 2]], axis=1)
    w_qkv = w.reshape(d, 3 * n_groups * d).astype(BF16)
    qkv = _norm_proj(xr, row(attn_norm[0]), w_qkv, cos, sin,
                     n_q=n_groups, n_k=n_groups, tm=tm, tn=d)
    outs, stats = zip(*[_dilated_group(qkv, batch, seq, g, n_groups) for g in range(n_groups)])
    head_of_col = jnp.arange(d) // HEAD_DIM
    expand = (jnp.arange(LANES)[:, None] == head_of_col[None, :]).astype(BF16)
    xr = _merge_wo(xr, outs, stats, expand, a_w_o[0].astype(BF16), tm=tm)
    xr = _ffn(xr, row(ffn_norm[0]), dense_w1[0].astype(BF16), dense_w3[0].astype(BF16),
              dense_w2[0].astype(BF16), tm=tm, tf=dense_w1.shape[2] // 2)
    xr = _ple(xr, pr, 0, row(ple_norm[0]), ple_w_gate[0].astype(BF16), ple_w_proj[0].astype(BF16),
              row(final_norm), final=False, tm=tm)

    half = n_heads // 2
    perm_q = _pair_perm([(j, half + j) for j in range(half)])
    perm_k = _pair_perm([(0, 1)])
    kv_cols = B_KV_HEADS * HEAD_DIM
    w_kv = jnp.concatenate([kv_w[:, :kv_cols][:, perm_k], kv_w[:, kv_cols:]], axis=1).astype(BF16)
    q1, kv1 = _qkv1(xr, row(attn_norm[1]), row(kv_norm), b_w_q[0][:, perm_q].astype(BF16), w_kv,
                    cos, sin, tm=tm)
    o1 = _swa_attention(q1, kv1, b_sinks[0], batch, seq)
    o_rows = np.concatenate([np.concatenate([np.arange(j * HEAD_DIM, (j + 1) * HEAD_DIM),
                                             np.arange((half + j) * HEAD_DIM, (half + j + 1) * HEAD_DIM)])
                             for j in range(half)])
    xr = _wo(xr, o1, b_w_o[0][o_rows].astype(BF16), tm=tm)

    wr = jnp.zeros((d, LANES), F32).at[:, :N_EXPERTS].set(moe_router[0]).astype(BF16)
    route = _router(xr, row(ffn_norm[1]), wr, tm=tm)
    row_token, dest, tile_expert, n_used = _routing_tables(route, tm)
    xs = _gather_rows(xr, row_token, tg=256)
    ys = _moe_experts(xs, row(ffn_norm[1]), moe_w1[0].astype(BF16), moe_w3[0].astype(BF16),
                      moe_w2[0].astype(BF16), tile_expert, n_used, tm=tm, tf=moe_w1.shape[3] // 2)
    xr = _combine(xr, route, ys, dest, tm=256)
    xr = _ple(xr, pr, 1, row(ple_norm[1]), ple_w_gate[1].astype(BF16), ple_w_proj[1].astype(BF16),
              row(final_norm), final=True, tm=tm)
    return xr.reshape(batch, seq, d)
```

```python
import functools

import numpy as np
import jax
import jax.numpy as jnp
from jax import lax
from jax.experimental import pallas as pl
from jax.experimental.pallas import tpu as pltpu

F32 = jnp.float32
BF16 = jnp.bfloat16

LANES = 128
HEAD_DIM = 64
HALF_DIM = HEAD_DIM // 2
BLOCK = 128
A_CONFIGS = ((128, 1), (512, 4), (2048, 16))
B_KV_HEADS = 2
B_WINDOW = 128
N_EXPERTS = 8
ROPE_THETA = 10000.0
EPS = 1e-6
NEG_INF = -1e30
VMEM_LIMIT = 56 * 1024 * 1024

PLAIN, ROPE, ROPE_SCALED = 0, 1, 2
Q_SCALE = HEAD_DIM ** -0.5


def _params(*sem):
    return pltpu.CompilerParams(dimension_semantics=sem, vmem_limit_bytes=VMEM_LIMIT)


def _rms(x, g):
    ms = jnp.mean(x * x, axis=-1, keepdims=True)
    return x * lax.rsqrt(ms + EPS) * g


def _pair_perm(pairs):
    idx = []
    for ha, hb in pairs:
        for base in (0, HALF_DIM):
            idx += [ha * HEAD_DIM + base + d for d in range(HALF_DIM)]
            idx += [hb * HEAD_DIM + base + d for d in range(HALF_DIM)]
    return np.asarray(idx, np.int32)


def _rope_tables(seq):
    pos = jnp.arange(seq, dtype=F32)
    inv = 1.0 / (ROPE_THETA ** (jnp.arange(HALF_DIM, dtype=F32) / HALF_DIM))
    ang = pos[:, None] * inv[None, :]
    cos, sin = jnp.cos(ang), jnp.sin(ang)
    return jnp.tile(cos, (1, 4)), jnp.concatenate([-sin, -sin, sin, sin], axis=1)


def _rope_coeffs(cos, sin):
    a = jnp.stack([cos * Q_SCALE, cos, jnp.ones_like(cos)])
    b = jnp.stack([sin * Q_SCALE, sin, jnp.zeros_like(sin)])
    return a, b


def _store_chunks(acc, cos, sin, o_ref, kinds):
    for c, kind in enumerate(kinds):
        seg = acc[:, c * LANES:(c + 1) * LANES]
        if kind != PLAIN:
            seg = seg * cos + pltpu.roll(seg, LANES // 2, 1) * sin
            if kind == ROPE_SCALED:
                seg = seg * Q_SCALE
        o_ref[:, c * LANES:(c + 1) * LANES] = seg.astype(o_ref.dtype)


def _group_proj_kernel(x_ref, g_ref, w_ref, a_ref, b_ref, o_ref, h_ref, *scratch, dil):
    @pl.when(pl.program_id(1) == 0)
    def _():
        h_ref[...] = _rms(x_ref[...], g_ref[...]).astype(BF16)

    acc = jnp.dot(h_ref[...], w_ref[...], preferred_element_type=F32)
    a, b = a_ref[...], b_ref[...]
    rows = acc.shape[0]
    for c in range(acc.shape[1] // LANES):
        sl = slice(c * LANES, (c + 1) * LANES)
        seg = acc[:, sl]
        seg = seg * a + pltpu.roll(seg, LANES // 2, 1) * b
        if dil == 1:
            o_ref[:, sl] = seg.astype(o_ref.dtype)
        else:
            scratch[0][c] = seg
    if dil > 1:
        for r in range(dil):
            for c in range(acc.shape[1] // LANES):
                part = scratch[0][c, pl.ds(r, rows // dil, stride=dil), :]
                o_ref[r, :, c * LANES:(c + 1) * LANES] = part.astype(o_ref.dtype)


def _group_proj(x, g, w, coef_a, coef_b, batch, seq, dil, *, tm):
    t, d = x.shape
    tn = w.shape[1] // 3
    tiles_per_seq = seq // tm
    n = seq // dil
    if dil == 1:
        out_spec = pl.BlockSpec((tm, tn), lambda i, j: (i, j))
        out_shape = jax.ShapeDtypeStruct((t, 3 * tn), BF16)
        scratch = []
    else:
        out_spec = pl.BlockSpec((None, dil, tm // dil, tn),
                                lambda i, j: (i // tiles_per_seq, 0, i % tiles_per_seq, j))
        out_shape = jax.ShapeDtypeStruct((batch, dil, n, 3 * tn), BF16)
        scratch = [pltpu.VMEM((tn // LANES, tm, LANES), F32)]
    coef_spec = pl.BlockSpec((None, tm, LANES), lambda i, j: (j, i % tiles_per_seq, 0))
    return pl.pallas_call(
        functools.partial(_group_proj_kernel, dil=dil),
        grid=(t // tm, 3),
        in_specs=[
            pl.BlockSpec((tm, d), lambda i, j: (i, 0)),
            pl.BlockSpec((1, d), lambda i, j: (0, 0)),
            pl.BlockSpec((d, tn), lambda i, j: (0, j)),
            coef_spec, coef_spec,
        ],
        out_specs=out_spec,
        out_shape=out_shape,
        scratch_shapes=[pltpu.VMEM((tm, d), BF16)] + scratch,
        compiler_params=_params("parallel", "arbitrary"),
        name=f"group_proj_d{dil}",
    )(x, g, w, coef_a, coef_b)


def _qkv1_kernel(x_ref, gq_ref, gkv_ref, wq_ref, wkv_ref, cos_ref, sin_ref, q_ref, kv_ref):
    x = x_ref[...]
    xn = x * lax.rsqrt(jnp.mean(x * x, axis=-1, keepdims=True) + EPS)
    hq = (xn * gq_ref[...]).astype(BF16)
    hkv = (xn * gkv_ref[...]).astype(BF16)
    cos, sin = cos_ref[...], sin_ref[...]
    accq = jnp.dot(hq, wq_ref[...], preferred_element_type=F32)
    _store_chunks(accq, cos, sin, q_ref, (ROPE_SCALED,) * (accq.shape[1] // LANES))
    acckv = jnp.dot(hkv, wkv_ref[...], preferred_element_type=F32)
    _store_chunks(acckv, cos, sin, kv_ref, (ROPE, PLAIN))


def _qkv1(x, gq, gkv, wq, wkv, cos, sin, *, tm):
    t, d = x.shape
    s_tiles = cos.shape[0] // tm
    row = lambda i: (i, 0)
    fixed = lambda i: (0, 0)
    return pl.pallas_call(
        _qkv1_kernel,
        grid=(t // tm,),
        in_specs=[
            pl.BlockSpec((tm, d), row),
            pl.BlockSpec((1, d), fixed),
            pl.BlockSpec((1, d), fixed),
            pl.BlockSpec(wq.shape, fixed),
            pl.BlockSpec(wkv.shape, fixed),
            pl.BlockSpec((tm, LANES), lambda i: (i % s_tiles, 0)),
            pl.BlockSpec((tm, LANES), lambda i: (i % s_tiles, 0)),
        ],
        out_specs=[pl.BlockSpec((tm, wq.shape[1]), row), pl.BlockSpec((tm, wkv.shape[1]), row)],
        out_shape=[jax.ShapeDtypeStruct((t, wq.shape[1]), BF16),
                   jax.ShapeDtypeStruct((t, wkv.shape[1]), BF16)],
        compiler_params=_params("parallel"),
        name="qkv_layer1",
    )(x, gq, gkv, wq, wkv, cos, sin)


def _band_mask(blk, max_dist):
    qi = lax.broadcasted_iota(jnp.int32, (BLOCK, 2 * BLOCK), 0) + BLOCK
    kj = lax.broadcasted_iota(jnp.int32, (BLOCK, 2 * BLOCK), 1)
    rel = qi - kj
    mask = (rel >= 0) & (rel <= max_dist) & ((kj >= BLOCK) | (blk > 0))
    return jnp.concatenate([mask, mask], axis=0)


def _lane_masks():
    lane = lax.broadcasted_iota(jnp.int32, (BLOCK, LANES), 1)
    first_qk = (lane % HEAD_DIM) < HALF_DIM
    sel_a = jnp.where(first_qk, 1.0, 0.0).astype(BF16)
    sel_b = jnp.where(first_qk, 0.0, 1.0).astype(BF16)
    return lane, sel_a, sel_b, lane < HEAD_DIM


def _pair_scores(q2, kcat, sel_a, sel_b, mask2):
    qs = jnp.concatenate([q2 * sel_a, q2 * sel_b], axis=0)
    s = lax.dot_general(qs, kcat, (((1,), (1,)), ((), ())), preferred_element_type=F32)
    return jnp.where(mask2, s, NEG_INF)


def _dil_attn_kernel(q_ref, kp_ref, kc_ref, vp_ref, vc_ref, o_ref, st_ref, *, steps):
    mask2 = _band_mask(pl.program_id(2), steps)
    lane, sel_a, sel_b, first_v = _lane_masks()
    stats = jnp.zeros((BLOCK, LANES), F32)
    for p in range(q_ref.shape[1] // LANES):
        sl = slice(p * LANES, (p + 1) * LANES)
        kcat = jnp.concatenate([kp_ref[:, sl], kc_ref[:, sl]], axis=0)
        vcat = jnp.concatenate([vp_ref[:, sl], vc_ref[:, sl]], axis=0)
        s = _pair_scores(q_ref[:, sl], kcat, sel_a, sel_b, mask2)
        m = jnp.max(s, axis=-1, keepdims=True)
        pe = jnp.exp(s - m)
        den = jnp.sum(pe, axis=-1, keepdims=True)
        pv = jnp.dot(pe.astype(BF16), vcat, preferred_element_type=F32)
        pv = pv * (1.0 / den)
        o_ref[:, sl] = jnp.where(first_v, pv[:BLOCK], pv[BLOCK:]).astype(o_ref.dtype)
        lse = m + jnp.log(den)
        stats = jnp.where(lane == 2 * p, lse[:BLOCK], stats)
        stats = jnp.where(lane == 2 * p + 1, lse[BLOCK:], stats)
    st_ref[...] = stats


def _dilated_group(qkv, batch, seq, g):
    win, dil = A_CONFIGS[g]
    n = seq // dil
    d = qkv.shape[-1] // 3
    view = qkv.reshape(batch, dil, n, 3 * d)

    def spec(kind, prev):
        def index(b, r, blk):
            return (b, r, jnp.maximum(blk - 1, 0) if prev else blk, kind)
        return pl.BlockSpec((None, None, BLOCK, d), index)

    return pl.pallas_call(
        functools.partial(_dil_attn_kernel, steps=win // dil),
        grid=(batch, dil, n // BLOCK),
        in_specs=[spec(0, False), spec(1, True), spec(1, False), spec(2, True), spec(2, False)],
        out_specs=[pl.BlockSpec((None, None, BLOCK, d), lambda b, r, blk: (b, r, blk, 0)),
                   pl.BlockSpec((None, None, BLOCK, LANES), lambda b, r, blk: (b, r, blk, 0))],
        out_shape=[jax.ShapeDtypeStruct((batch, dil, n, d), BF16),
                   jax.ShapeDtypeStruct((batch, dil, n, LANES), F32)],
        compiler_params=_params("parallel", "parallel", "arbitrary"),
        name=f"dilated_attn_g{g}",
    )(view, view, view, view, view)


def _swa_kernel(fill_ref, q_ref, kp_ref, kc_ref, vp_ref, vc_ref, o_ref):
    mask2 = _band_mask(pl.program_id(1), B_WINDOW - 1)
    _, sel_a, sel_b, first_v = _lane_masks()
    kcat = jnp.concatenate([kp_ref[...], kc_ref[...]], axis=0)
    vcat = jnp.concatenate([vp_ref[...], vc_ref[...]], axis=0)
    key_row = lax.broadcasted_iota(jnp.int32, vcat.shape, 0)
    vcat = jnp.where(key_row == 0, jnp.zeros_like(vcat), vcat)
    for j in range(q_ref.shape[1] // LANES):
        sl = slice(j * LANES, (j + 1) * LANES)
        qs = jnp.concatenate([q_ref[:, sl] * sel_a, q_ref[:, sl] * sel_b], axis=0)
        s = lax.dot_general(qs, kcat, (((1,), (1,)), ((), ())), preferred_element_type=F32)
        s = jnp.where(mask2, s, fill_ref[j])
        m = jnp.max(s, axis=-1, keepdims=True)
        pe = jnp.exp(s - m)
        den = jnp.sum(pe, axis=-1, keepdims=True)
        pv = jnp.dot(pe.astype(BF16), vcat, preferred_element_type=F32)
        pv = pv * (1.0 / den)
        o_ref[:, sl] = jnp.where(first_v, pv[:BLOCK], pv[BLOCK:]).astype(o_ref.dtype)


def _swa_attention(q, kv, sinks, batch, seq):
    d = q.shape[1]
    nb = seq // BLOCK
    qv = q.reshape(batch, seq, d)
    kvv = kv.reshape(batch, seq, 2 * LANES)

    def kv_spec(col, prev):
        def index(b, blk):
            return (b, jnp.maximum(blk - 1, 0) if prev else blk, col)
        return pl.BlockSpec((None, BLOCK, LANES), index)

    half = sinks.shape[0] // 2
    sink_rows = jnp.repeat(jnp.stack([sinks[:half], sinks[half:]], axis=1), BLOCK, axis=1)
    first_key = jnp.arange(2 * BLOCK)[None, None, :] == 0
    fill = jnp.where(first_key, sink_rows[:, :, None].astype(F32), NEG_INF)

    o = pl.pallas_call(
        _swa_kernel,
        grid=(batch, nb),
        in_specs=[pl.BlockSpec(fill.shape, lambda b, blk: (0, 0, 0)),
                  pl.BlockSpec((None, BLOCK, d), lambda b, blk: (b, blk, 0)),
                  kv_spec(0, True), kv_spec(0, False), kv_spec(1, True), kv_spec(1, False)],
        out_specs=pl.BlockSpec((None, BLOCK, d), lambda b, blk: (b, blk, 0)),
        out_shape=jax.ShapeDtypeStruct((batch, seq, d), BF16),
        compiler_params=_params("parallel", "arbitrary"),
        name="swa_sink_attn",
    )(fill, qv, kvv, kvv, kvv, kvv)
    return o.reshape(batch * seq, d)


def _interleave(src_ref, dst_ref):
    dil, sub = src_ref.shape[0], src_ref.shape[1]
    chunks = dst_ref.shape[0]
    for r in range(dil):
        for c in range(chunks):
            part = src_ref[r, :, c * LANES:(c + 1) * LANES].astype(F32)
            dst_ref[c, pl.ds(r, sub, stride=dil), :] = part
    return jnp.concatenate([dst_ref[c] for c in range(chunks)], axis=1)


def _merge_wo_kernel(x_ref, o0_ref, o1_ref, o2_ref, s0_ref, s1_ref, s2_ref, e_ref, w_ref, out_ref,
                     ob_ref, sb1_ref, sb2_ref):
    s0, s1, s2 = s0_ref[0], _interleave(s1_ref, sb1_ref), _interleave(s2_ref, sb2_ref)
    top = jnp.maximum(jnp.maximum(s0, s1), s2)
    e0, e1, e2 = jnp.exp(s0 - top), jnp.exp(s1 - top), jnp.exp(s2 - top)
    inv = 1.0 / (e0 + e1 + e2)
    expand = e_ref[...]

    def weight(e):
        return jnp.dot((e * inv).astype(BF16), expand, preferred_element_type=F32)

    merged = weight(e0) * o0_ref[0].astype(F32)
    merged = merged + weight(e1) * _interleave(o1_ref, ob_ref)
    merged = merged + weight(e2) * _interleave(o2_ref, ob_ref)
    out_ref[...] = x_ref[...] + jnp.dot(merged.astype(BF16), w_ref[...], preferred_element_type=F32)


def _merge_wo(x, outs, stats, expand, w, batch, seq, *, tm):
    t, d = x.shape
    tiles_per_seq = seq // tm
    row = lambda i: (i, 0)
    fixed = lambda i: (0, 0)

    def sub(arr):
        dil = arr.shape[1]
        return pl.BlockSpec((None, dil, tm // dil, arr.shape[3]),
                            lambda i: (i // tiles_per_seq, 0, i % tiles_per_seq, 0))

    return pl.pallas_call(
        _merge_wo_kernel,
        grid=(t // tm,),
        in_specs=[pl.BlockSpec((tm, d), row)]
        + [sub(o) for o in outs] + [sub(s) for s in stats]
        + [pl.BlockSpec(expand.shape, fixed), pl.BlockSpec(w.shape, fixed)],
        out_specs=pl.BlockSpec((tm, d), row),
        out_shape=jax.ShapeDtypeStruct((t, d), F32),
        scratch_shapes=[pltpu.VMEM((d // LANES, tm, LANES), F32), pltpu.VMEM((1, tm, LANES), F32),
                        pltpu.VMEM((1, tm, LANES), F32)],
        compiler_params=_params("parallel"),
        name="merge_wo",
    )(x, *outs, *stats, expand, w)


def _wo_kernel(x_ref, o_ref, w_ref, out_ref):
    out_ref[...] = x_ref[...] + jnp.dot(o_ref[...], w_ref[...], preferred_element_type=F32)


def _wo(x, o, w, *, tm):
    t, d = x.shape
    row = lambda i: (i, 0)
    return pl.pallas_call(
        _wo_kernel,
        grid=(t // tm,),
        in_specs=[pl.BlockSpec((tm, d), row), pl.BlockSpec((tm, o.shape[1]), row),
                  pl.BlockSpec(w.shape, lambda i: (0, 0))],
        out_specs=pl.BlockSpec((tm, d), row),
        out_shape=jax.ShapeDtypeStruct((t, d), F32),
        compiler_params=_params("parallel"),
        name="wo_residual",
    )(x, o, w)


def _swiglu_step(h, w1_ref, w3_ref, w2_ref, acc_ref):
    a = jnp.dot(h, w1_ref[...], preferred_element_type=F32)
    b = jnp.dot(h, w3_ref[...], preferred_element_type=F32)
    act = (a * jax.nn.sigmoid(a) * b).astype(BF16)
    acc_ref[...] += jnp.dot(act, w2_ref[...], preferred_element_type=F32)


def _ffn_kernel(x_ref, g_ref, w1_ref, w3_ref, w2_ref, o_ref, h_ref, acc_ref):
    j = pl.program_id(1)

    @pl.when(j == 0)
    def _():
        h_ref[...] = _rms(x_ref[...], g_ref[...]).astype(BF16)
        acc_ref[...] = jnp.zeros_like(acc_ref)

    _swiglu_step(h_ref[...], w1_ref, w3_ref, w2_ref, acc_ref)

    @pl.when(j == pl.num_programs(1) - 1)
    def _():
        o_ref[...] = x_ref[...] + acc_ref[...]


def _ffn(x, g, w1, w3, w2, *, tm, tf):
    t, d = x.shape
    f = w1.shape[1]
    return pl.pallas_call(
        _ffn_kernel,
        grid=(t // tm, f // tf),
        in_specs=[pl.BlockSpec((tm, d), lambda i, j: (i, 0)),
                  pl.BlockSpec((1, d), lambda i, j: (0, 0)),
                  pl.BlockSpec((d, tf), lambda i, j: (0, j)),
                  pl.BlockSpec((d, tf), lambda i, j: (0, j)),
                  pl.BlockSpec((tf, d), lambda i, j: (j, 0))],
        out_specs=pl.BlockSpec((tm, d), lambda i, j: (i, 0)),
        out_shape=jax.ShapeDtypeStruct((t, d), F32),
        scratch_shapes=[pltpu.VMEM((tm, d), BF16), pltpu.VMEM((tm, d), F32)],
        compiler_params=_params("parallel", "arbitrary"),
        name="dense_swiglu",
    )(x, g, w1, w3, w2)


def _moe_kernel(te_ref, nu_ref, xs_ref, g_ref, w1_ref, w3_ref, w2_ref, o_ref, h_ref, acc_ref):
    i, j = pl.program_id(0), pl.program_id(1)
    used = i < nu_ref[0]
    last = j == pl.num_programs(1) - 1

    @pl.when(used & (j == 0))
    def _():
        h_ref[...] = _rms(xs_ref[...], g_ref[...]).astype(BF16)
        acc_ref[...] = jnp.zeros_like(acc_ref)

    @pl.when(used)
    def _():
        _swiglu_step(h_ref[...], w1_ref, w3_ref, w2_ref, acc_ref)

    @pl.when(used & last)
    def _():
        o_ref[...] = acc_ref[...]

    @pl.when(jnp.logical_not(used) & last)
    def _():
        o_ref[...] = jnp.zeros_like(o_ref)


def _moe_experts(xs, g, w1, w3, w2, tile_expert, n_used, *, tm, tf):
    p, d = xs.shape
    f = w1.shape[2]
    n_j = f // tf

    def jj(i, j, nu):
        return jnp.where(i < nu[0], j, n_j - 1)

    grid_spec = pltpu.PrefetchScalarGridSpec(
        num_scalar_prefetch=2,
        grid=(p // tm, n_j),
        in_specs=[pl.BlockSpec((tm, d), lambda i, j, te, nu: (i, 0)),
                  pl.BlockSpec((1, d), lambda i, j, te, nu: (0, 0)),
                  pl.BlockSpec((None, d, tf), lambda i, j, te, nu: (te[i], 0, jj(i, j, nu))),
                  pl.BlockSpec((None, d, tf), lambda i, j, te, nu: (te[i], 0, jj(i, j, nu))),
                  pl.BlockSpec((None, tf, d), lambda i, j, te, nu: (te[i], jj(i, j, nu), 0))],
        out_specs=pl.BlockSpec((tm, d), lambda i, j, te, nu: (i, 0)),
        scratch_shapes=[pltpu.VMEM((tm, d), BF16), pltpu.VMEM((tm, d), F32)],
    )
    return pl.pallas_call(
        _moe_kernel,
        grid_spec=grid_spec,
        out_shape=jax.ShapeDtypeStruct((p, d), F32),
        compiler_params=_params("arbitrary", "arbitrary"),
        name="moe_experts",
    )(tile_expert, n_used, xs, g, w1, w3, w2)


def _router_kernel(x_ref, g_ref, wr_ref, o_ref):
    h = _rms(x_ref[...], g_ref[...]).astype(BF16)
    logits = jnp.dot(h, wr_ref[...], preferred_element_type=F32)
    lane = lax.broadcasted_iota(jnp.int32, logits.shape, 1)
    logits = jnp.where(lane < N_EXPERTS, logits, -jnp.inf)
    v1 = jnp.max(logits, axis=-1, keepdims=True)
    i1 = jnp.min(jnp.where(logits == v1, lane, LANES), axis=-1, keepdims=True)
    rest = jnp.where(lane == i1, -jnp.inf, logits)
    v2 = jnp.max(rest, axis=-1, keepdims=True)
    i2 = jnp.min(jnp.where(rest == v2, lane, LANES), axis=-1, keepdims=True)
    e2 = jnp.exp(v2 - v1)
    inv = 1.0 / (1.0 + e2)
    out = jnp.where(lane == 0, i1.astype(F32), 0.0)
    out = jnp.where(lane == 1, i2.astype(F32), out)
    out = jnp.where(lane == 2, inv, out)
    out = jnp.where(lane == 3, e2 * inv, out)
    o_ref[...] = out


def _router(x, g, wr, *, tm):
    t, d = x.shape
    return pl.pallas_call(
        _router_kernel,
        grid=(t // tm,),
        in_specs=[pl.BlockSpec((tm, d), lambda i: (i, 0)),
                  pl.BlockSpec((1, d), lambda i: (0, 0)),
                  pl.BlockSpec(wr.shape, lambda i: (0, 0))],
        out_specs=pl.BlockSpec((tm, LANES), lambda i: (i, 0)),
        out_shape=jax.ShapeDtypeStruct((t, LANES), F32),
        compiler_params=_params("parallel"),
        name="router_top2",
    )(x, g, wr)


def _row_copy(src_hbm, dst_ref, sem, src_row, dst_row):
    return pltpu.make_async_copy(src_hbm.at[pl.ds(src_row, 1)], dst_ref.at[pl.ds(dst_row, 1)], sem)


def _gather_kernel(idx_ref, src_hbm, o_ref, sem):
    rows = o_ref.shape[0]

    def issue(r, carry):
        _row_copy(src_hbm, o_ref, sem, idx_ref[0, 0, r], r).start()
        return carry

    lax.fori_loop(0, rows, issue, 0)
    pltpu.make_async_copy(src_hbm.at[pl.ds(0, rows)], o_ref, sem).wait()


def _gather_rows(src, idx, *, tg):
    p = idx.shape[0]
    d = src.shape[1]
    return pl.pallas_call(
        _gather_kernel,
        grid=(p // tg,),
        in_specs=[pl.BlockSpec((1, 1, tg), lambda i: (i, 0, 0), memory_space=pltpu.SMEM),
                  pl.BlockSpec(memory_space=pl.ANY)],
        out_specs=pl.BlockSpec((tg, d), lambda i: (i, 0)),
        out_shape=jax.ShapeDtypeStruct((p, d), src.dtype),
        scratch_shapes=[pltpu.SemaphoreType.DMA(())],
        compiler_params=_params("arbitrary"),
        name="gather_rows",
    )(idx.reshape(p // tg, 1, tg), src)


def _combine_kernel(slot_ref, x_ref, route_ref, ys_hbm, o_ref, a_ref, b_ref, sem):
    rows = o_ref.shape[0]

    def issue(r, carry):
        _row_copy(ys_hbm, a_ref, sem, slot_ref[0, 0, 2 * r], r).start()
        _row_copy(ys_hbm, b_ref, sem, slot_ref[0, 0, 2 * r + 1], r).start()
        return carry

    lax.fori_loop(0, rows, issue, 0)
    pltpu.make_async_copy(ys_hbm.at[pl.ds(0, rows)], a_ref, sem).wait()
    pltpu.make_async_copy(ys_hbm.at[pl.ds(0, rows)], b_ref, sem).wait()
    route = route_ref[...]
    y = route[:, 2:3] * a_ref[...] + route[:, 3:4] * b_ref[...]
    o_ref[...] = x_ref[...] + y


def _combine(x, route, ys, slots, *, tm):
    t, d = x.shape
    row = lambda i: (i, 0)
    return pl.pallas_call(
        _combine_kernel,
        grid=(t // tm,),
        in_specs=[pl.BlockSpec((1, 1, 2 * tm), lambda i: (i, 0, 0), memory_space=pltpu.SMEM),
                  pl.BlockSpec((tm, d), row),
                  pl.BlockSpec((tm, LANES), row),
                  pl.BlockSpec(memory_space=pl.ANY)],
        out_specs=pl.BlockSpec((tm, d), row),
        out_shape=jax.ShapeDtypeStruct((t, d), F32),
        scratch_shapes=[pltpu.VMEM((tm, d), F32), pltpu.VMEM((tm, d), F32),
                        pltpu.SemaphoreType.DMA(())],
        compiler_params=_params("arbitrary"),
        name="moe_combine",
    )(slots.reshape(t // tm, 1, 2 * tm), x, route, ys)


def _ple_kernel(x_ref, p_ref, g_ref, wg_ref, wp_ref, fg_ref, o_ref, *, final):
    x = x_ref[...]
    h = _rms(x, g_ref[...]).astype(BF16)
    gate = jax.nn.sigmoid(jnp.dot(h, wg_ref[...], preferred_element_type=F32))
    proj = jnp.dot(p_ref[...].astype(BF16), wp_ref[...], preferred_element_type=F32)
    y = x + proj * gate
    if final:
        y = _rms(y, fg_ref[...])
    o_ref[...] = y


def _ple(x, p, layer, g, wg, wp, fg, *, final, tm):
    t, d = x.shape
    fixed = lambda i: (0, 0)
    return pl.pallas_call(
        functools.partial(_ple_kernel, final=final),
        grid=(t // tm,),
        in_specs=[pl.BlockSpec((tm, d), lambda i: (i, 0)),
                  pl.BlockSpec((None, tm, p.shape[2]), lambda i: (layer, i, 0)),
                  pl.BlockSpec((1, d), fixed),
                  pl.BlockSpec(wg.shape, fixed),
                  pl.BlockSpec(wp.shape, fixed),
                  pl.BlockSpec((1, d), fixed)],
        out_specs=pl.BlockSpec((tm, d), lambda i: (i, 0)),
        out_shape=jax.ShapeDtypeStruct((t, d), F32),
        compiler_params=_params("parallel"),
        name="ple_final" if final else "ple",
    )(x, p, g, wg, wp, fg)


def _routing_tables(route, tm):
    t = route.shape[0]
    ids = route[:, :2].astype(jnp.int32)
    flat = ids.reshape(-1)
    onehot = (flat[:, None] == jnp.arange(N_EXPERTS)[None, :]).astype(jnp.int32)
    csum = jnp.cumsum(onehot, axis=0)
    rank = jnp.take_along_axis(csum, flat[:, None], axis=1)[:, 0] - 1
    counts = csum[-1]
    tiles_per = (counts + tm - 1) // tm
    tile_end = jnp.cumsum(tiles_per)
    tile_start = tile_end - tiles_per
    dest = tile_start[flat] * tm + rank
    n_tiles = (2 * t) // tm + N_EXPERTS
    n_used = tile_end[-1]
    tile_ids = jnp.minimum(jnp.arange(n_tiles), n_used - 1)
    tile_expert = jnp.minimum(jnp.sum((tile_ids[:, None] >= tile_end[None, :]).astype(jnp.int32), axis=1),
                              N_EXPERTS - 1)
    row_token = jnp.zeros((n_tiles * tm,), jnp.int32).at[dest].set(
        jnp.arange(2 * t, dtype=jnp.int32) // 2, unique_indices=True)
    return row_token, dest.astype(jnp.int32), tile_expert, n_used.reshape(1).astype(jnp.int32)


def kernel(x, p, attn_norm, ffn_norm, a_w_qkv, a_w_o, kv_norm, kv_w, b_w_q, b_sinks, b_w_o,
           dense_w1, dense_w3, dense_w2, moe_router, moe_w1, moe_w3, moe_w2,
           ple_norm, ple_w_gate, ple_w_proj, final_norm):
    batch, seq, d = x.shape
    t = batch * seq
    n_heads = d // HEAD_DIM
    n_groups = len(A_CONFIGS)
    tm = 512
    xr = x.reshape(t, d)
    pr = p.reshape(p.shape[0], t, p.shape[3])
    cos, sin = _rope_tables(seq)
    row = lambda v: v.reshape(1, -1)

    perm_a = _pair_perm([(2 * j, 2 * j + 1) for j in range(n_heads // 2)])
    coef_a, coef_b = _rope_coeffs(cos, sin)
    w = a_w_qkv[0].reshape(d, n_groups, 3, d)
    outs, stats = [], []
    for g, (_, dil) in enumerate(A_CONFIGS):
        w_g = jnp.concatenate([w[:, g, 0][:, perm_a], w[:, g, 1][:, perm_a], w[:, g, 2]],
                              axis=1).astype(BF16)
        qkv = _group_proj(xr, row(attn_norm[0]), w_g, coef_a, coef_b, batch, seq, dil, tm=tm)
        o_g, st_g = _dilated_group(qkv, batch, seq, g)
        outs.append(o_g)
        stats.append(st_g)
    head_of_col = jnp.arange(d) // HEAD_DIM
    expand = (jnp.arange(LANES)[:, None] == head_of_col[None, :]).astype(BF16)
    xr = _merge_wo(xr, outs, stats, expand, a_w_o[0].astype(BF16), batch, seq, tm=tm)
    xr = _ffn(xr, row(ffn_norm[0]), dense_w1[0].astype(BF16), dense_w3[0].astype(BF16),
              dense_w2[0].astype(BF16), tm=tm, tf=dense_w1.shape[2] // 2)
    xr = _ple(xr, pr, 0, row(ple_norm[0]), ple_w_gate[0].astype(BF16), ple_w_proj[0].astype(BF16),
              row(final_norm), final=False, tm=tm)

    half = n_heads // 2
    perm_q = _pair_perm([(j, half + j) for j in range(half)])
    perm_k = _pair_perm([(0, 1)])
    kv_cols = B_KV_HEADS * HEAD_DIM
    w_kv = jnp.concatenate([kv_w[:, :kv_cols][:, perm_k], kv_w[:, kv_cols:]], axis=1).astype(BF16)
    q1, kv1 = _qkv1(xr, row(attn_norm[1]), row(kv_norm), b_w_q[0][:, perm_q].astype(BF16), w_kv,
                    cos, sin, tm=tm)
    o1 = _swa_attention(q1, kv1, b_sinks[0], batch, seq)
    o_rows = np.concatenate([np.concatenate([np.arange(j * HEAD_DIM, (j + 1) * HEAD_DIM),
                                             np.arange((half + j) * HEAD_DIM, (half + j + 1) * HEAD_DIM)])
                             for j in range(half)])
    xr = _wo(xr, o1, b_w_o[0][o_rows].astype(BF16), tm=tm)

    wr = jnp.zeros((d, LANES), F32).at[:, :N_EXPERTS].set(moe_router[0]).astype(BF16)
    route = _router(xr, row(ffn_norm[1]), wr, tm=tm)
    row_token, dest, tile_expert, n_used = _routing_tables(route, tm)
    xs = _gather_rows(xr, row_token, tg=256)
    ys = _moe_experts(xs, row(ffn_norm[1]), moe_w1[0].astype(BF16), moe_w3[0].astype(BF16),
                      moe_w2[0].astype(BF16), tile_expert, n_used, tm=tm, tf=moe_w1.shape[3] // 2)
    xr = _combine(xr, route, ys, dest, tm=256)
    xr = _ple(xr, pr, 1, row(ple_norm[1]), ple_w_gate[1].astype(BF16), ple_w_proj[1].astype(BF16),
              row(final_norm), final=True, tm=tm)
    return xr.reshape(batch, seq, d)
```

```python
import functools

import numpy as np
import jax
import jax.numpy as jnp
from jax import lax
from jax.experimental import pallas as pl
from jax.experimental.pallas import tpu as pltpu

F32 = jnp.float32
BF16 = jnp.bfloat16

LANES = 128
SUBLANES = 8
HEAD_DIM = 64
HALF_DIM = HEAD_DIM // 2
BLOCK = 128
A_CONFIGS = ((128, 1), (512, 4), (2048, 16))
B_KV_HEADS = 2
B_WINDOW = 128
N_EXPERTS = 8
ROPE_THETA = 10000.0
EPS = 1e-6
NEG_INF = -1e30
VMEM_LIMIT = 56 * 1024 * 1024

TM = 512
TM_COMBINE = 256
DISPATCH_ROWS = 2048
DMA_UNROLL = 8

PLAIN, ROPE, ROPE_SCALED = 0, 1, 2
Q_SCALE = HEAD_DIM ** -0.5


def _params(*sem):
    return pltpu.CompilerParams(dimension_semantics=sem, vmem_limit_bytes=VMEM_LIMIT)


def _rms(x, g):
    ms = jnp.mean(x * x, axis=-1, keepdims=True)
    return x * lax.rsqrt(ms + EPS) * g


def _pair_perm(pairs):
    idx = []
    for ha, hb in pairs:
        for base in (0, HALF_DIM):
            idx += [ha * HEAD_DIM + base + d for d in range(HALF_DIM)]
            idx += [hb * HEAD_DIM + base + d for d in range(HALF_DIM)]
    return np.asarray(idx, np.int32)


def _rope_tables(seq):
    pos = jnp.arange(seq, dtype=F32)
    inv = 1.0 / (ROPE_THETA ** (jnp.arange(HALF_DIM, dtype=F32) / HALF_DIM))
    ang = pos[:, None] * inv[None, :]
    cos, sin = jnp.cos(ang), jnp.sin(ang)
    return jnp.tile(cos, (1, 4)), jnp.concatenate([-sin, -sin, sin, sin], axis=1)


def _rope_coeffs(cos, sin):
    a = jnp.stack([cos * Q_SCALE, cos, jnp.ones_like(cos)])
    b = jnp.stack([sin * Q_SCALE, sin, jnp.zeros_like(sin)])
    return a, b


def _store_chunks(acc, cos, sin, o_ref, kinds):
    for c, kind in enumerate(kinds):
        seg = acc[:, c * LANES:(c + 1) * LANES]
        if kind != PLAIN:
            seg = seg * cos + pltpu.roll(seg, LANES // 2, 1) * sin
            if kind == ROPE_SCALED:
                seg = seg * Q_SCALE
        o_ref[:, c * LANES:(c + 1) * LANES] = seg.astype(o_ref.dtype)


def _to_token_tiles(val, ref):
    rows = val.shape[0]
    for c in range(val.shape[1] // LANES):
        ref[pl.ds(c, rows, stride=SUBLANES), :] = val[:, c * LANES:(c + 1) * LANES]


def _from_token_tiles(ref, rows):
    return jnp.concatenate([ref[pl.ds(c, rows, stride=SUBLANES), :] for c in range(SUBLANES)], axis=1)


def _tile_at(ref, line):
    return ref.at[pl.ds(pl.multiple_of(line, SUBLANES), SUBLANES)]


def _group_proj_kernel(x_ref, g_ref, w_ref, a_ref, b_ref, perm_ref, o_ref, *, dil):
    h = _rms(x_ref[...], g_ref[...]).astype(BF16)
    if dil > 1:
        h = jnp.dot(perm_ref[...], h, preferred_element_type=F32).astype(BF16)
    sub = h.shape[0] // dil
    tn = w_ref.shape[1] // 3
    for kind in range(3):
        acc = jnp.dot(h, w_ref[:, kind * tn:(kind + 1) * tn], preferred_element_type=F32)
        a, b = a_ref[kind], b_ref[kind]
        for c in range(tn // LANES):
            seg = acc[:, c * LANES:(c + 1) * LANES]
            seg = (seg * a + pltpu.roll(seg, LANES // 2, 1) * b).astype(o_ref.dtype)
            cols = slice(kind * tn + c * LANES, kind * tn + (c + 1) * LANES)
            if dil == 1:
                o_ref[:, cols] = seg
            else:
                for r in range(dil):
                    o_ref[r, :, cols] = seg[r * sub:(r + 1) * sub]


def _residue_major_rows(dil):
    q = np.arange(TM)
    sub = TM // dil
    return (q % sub) * dil + q // sub


def _group_proj(x, g, w, coef_a, coef_b, batch, seq, dil):
    t, d = x.shape
    n_out = w.shape[1]
    tiles_per_seq = seq // TM
    if dil == 1:
        out_spec = pl.BlockSpec((TM, n_out), lambda i: (i, 0))
        out_shape = jax.ShapeDtypeStruct((t, n_out), BF16)
    else:
        out_spec = pl.BlockSpec((None, dil, TM // dil, n_out),
                                lambda i: (i // tiles_per_seq, 0, i % tiles_per_seq, 0))
        out_shape = jax.ShapeDtypeStruct((batch, dil, seq // dil, n_out), BF16)
    nat = _residue_major_rows(dil)
    perm = jnp.asarray(nat[:, None] == np.arange(TM)[None, :], BF16)
    rows = (np.arange(seq) // TM) * TM + nat[np.arange(seq) % TM]
    coef_spec = pl.BlockSpec((3, TM, LANES), lambda i: (0, i % tiles_per_seq, 0))
    return pl.pallas_call(
        functools.partial(_group_proj_kernel, dil=dil),
        grid=(t // TM,),
        in_specs=[
            pl.BlockSpec((TM, d), lambda i: (i, 0)),
            pl.BlockSpec((1, d), lambda i: (0, 0)),
            pl.BlockSpec(w.shape, lambda i: (0, 0)),
            coef_spec, coef_spec,
            pl.BlockSpec((TM, TM), lambda i: (0, 0)),
        ],
        out_specs=out_spec,
        out_shape=out_shape,
        compiler_params=_params("parallel"),
        name=f"group_proj_d{dil}",
    )(x, g, w, coef_a[:, rows], coef_b[:, rows], perm)


def _qkv1_kernel(x_ref, gq_ref, gkv_ref, wq_ref, wkv_ref, cos_ref, sin_ref, q_ref, kv_ref):
    x = x_ref[...]
    xn = x * lax.rsqrt(jnp.mean(x * x, axis=-1, keepdims=True) + EPS)
    hq = (xn * gq_ref[...]).astype(BF16)
    hkv = (xn * gkv_ref[...]).astype(BF16)
    cos, sin = cos_ref[...], sin_ref[...]
    accq = jnp.dot(hq, wq_ref[...], preferred_element_type=F32)
    _store_chunks(accq, cos, sin, q_ref, (ROPE_SCALED,) * (accq.shape[1] // LANES))
    acckv = jnp.dot(hkv, wkv_ref[...], preferred_element_type=F32)
    _store_chunks(acckv, cos, sin, kv_ref, (ROPE, PLAIN))


def _qkv1(x, gq, gkv, wq, wkv, cos, sin):
    t, d = x.shape
    s_tiles = cos.shape[0] // TM
    row = lambda i: (i, 0)
    fixed = lambda i: (0, 0)
    return pl.pallas_call(
        _qkv1_kernel,
        grid=(t // TM,),
        in_specs=[
            pl.BlockSpec((TM, d), row),
            pl.BlockSpec((1, d), fixed),
            pl.BlockSpec((1, d), fixed),
            pl.BlockSpec(wq.shape, fixed),
            pl.BlockSpec(wkv.shape, fixed),
            pl.BlockSpec((TM, LANES), lambda i: (i % s_tiles, 0)),
            pl.BlockSpec((TM, LANES), lambda i: (i % s_tiles, 0)),
        ],
        out_specs=[pl.BlockSpec((TM, wq.shape[1]), row), pl.BlockSpec((TM, wkv.shape[1]), row)],
        out_shape=[jax.ShapeDtypeStruct((t, wq.shape[1]), BF16),
                   jax.ShapeDtypeStruct((t, wkv.shape[1]), BF16)],
        compiler_params=_params("parallel"),
        name="qkv_layer1",
    )(x, gq, gkv, wq, wkv, cos, sin)


def _band_mask(blk, max_dist):
    qi = lax.broadcasted_iota(jnp.int32, (BLOCK, 2 * BLOCK), 0) + BLOCK
    kj = lax.broadcasted_iota(jnp.int32, (BLOCK, 2 * BLOCK), 1)
    rel = qi - kj
    mask = (rel >= 0) & (rel <= max_dist) & ((kj >= BLOCK) | (blk > 0))
    return jnp.concatenate([mask, mask], axis=0)


def _lane_masks():
    lane = lax.broadcasted_iota(jnp.int32, (BLOCK, LANES), 1)
    first_qk = (lane % HEAD_DIM) < HALF_DIM
    sel_a = jnp.where(first_qk, 1.0, 0.0).astype(BF16)
    sel_b = jnp.where(first_qk, 0.0, 1.0).astype(BF16)
    return lane, sel_a, sel_b, lane < HEAD_DIM


def _pair_scores(q2, kcat, sel_a, sel_b):
    qs = jnp.concatenate([q2 * sel_a, q2 * sel_b], axis=0)
    return lax.dot_general(qs, kcat, (((1,), (1,)), ((), ())), preferred_element_type=F32)


def _dil_attn_kernel(q_ref, kp_ref, kc_ref, vp_ref, vc_ref, o_ref, st_ref, *, steps):
    mask2 = _band_mask(pl.program_id(2), steps)
    lane, sel_a, sel_b, first_v = _lane_masks()
    stats = jnp.zeros((BLOCK, LANES), F32)
    for p in range(q_ref.shape[1] // LANES):
        sl = slice(p * LANES, (p + 1) * LANES)
        kcat = jnp.concatenate([kp_ref[:, sl], kc_ref[:, sl]], axis=0)
        vcat = jnp.concatenate([vp_ref[:, sl], vc_ref[:, sl]], axis=0)
        s = jnp.where(mask2, _pair_scores(q_ref[:, sl], kcat, sel_a, sel_b), NEG_INF)
        m = jnp.max(s, axis=-1, keepdims=True)
        pe = jnp.exp(s - m)
        den = jnp.sum(pe, axis=-1, keepdims=True)
        pv = jnp.dot(pe.astype(BF16), vcat, preferred_element_type=F32)
        pv = pv * (1.0 / den)
        o_ref[:, sl] = jnp.where(first_v, pv[:BLOCK], pv[BLOCK:]).astype(o_ref.dtype)
        lse = m + jnp.log(den)
        stats = jnp.where(lane == 2 * p, lse[:BLOCK], stats)
        stats = jnp.where(lane == 2 * p + 1, lse[BLOCK:], stats)
    st_ref[...] = stats


def _dilated_group(qkv, batch, seq, g):
    win, dil = A_CONFIGS[g]
    n = seq // dil
    d = qkv.shape[-1] // 3
    view = qkv.reshape(batch, dil, n, 3 * d)

    def spec(kind, prev):
        def index(b, r, blk):
            return (b, r, jnp.maximum(blk - 1, 0) if prev else blk, kind)
        return pl.BlockSpec((None, None, BLOCK, d), index)

    return pl.pallas_call(
        functools.partial(_dil_attn_kernel, steps=win // dil),
        grid=(batch, dil, n // BLOCK),
        in_specs=[spec(0, False), spec(1, True), spec(1, False), spec(2, True), spec(2, False)],
        out_specs=[pl.BlockSpec((None, None, BLOCK, d), lambda b, r, blk: (b, r, blk, 0)),
                   pl.BlockSpec((None, None, BLOCK, LANES), lambda b, r, blk: (b, r, blk, 0))],
        out_shape=[jax.ShapeDtypeStruct((batch, dil, n, d), BF16),
                   jax.ShapeDtypeStruct((batch, dil, n, LANES), F32)],
        compiler_params=_params("parallel", "parallel", "arbitrary"),
        name=f"dilated_attn_g{g}",
    )(view, view, view, view, view)


def _swa_kernel(fill_ref, q_ref, kp_ref, kc_ref, vp_ref, vc_ref, o_ref):
    mask2 = _band_mask(pl.program_id(1), B_WINDOW - 1)
    _, sel_a, sel_b, first_v = _lane_masks()
    kcat = jnp.concatenate([kp_ref[...], kc_ref[...]], axis=0)
    vcat = jnp.concatenate([vp_ref[...], vc_ref[...]], axis=0)
    key_row = lax.broadcasted_iota(jnp.int32, vcat.shape, 0)
    vcat = jnp.where(key_row == 0, jnp.zeros_like(vcat), vcat)
    for j in range(q_ref.shape[1] // LANES):
        sl = slice(j * LANES, (j + 1) * LANES)
        s = jnp.where(mask2, _pair_scores(q_ref[:, sl], kcat, sel_a, sel_b), fill_ref[j])
        m = jnp.max(s, axis=-1, keepdims=True)
        pe = jnp.exp(s - m)
        den = jnp.sum(pe, axis=-1, keepdims=True)
        pv = jnp.dot(pe.astype(BF16), vcat, preferred_element_type=F32)
        pv = pv * (1.0 / den)
        o_ref[:, sl] = jnp.where(first_v, pv[:BLOCK], pv[BLOCK:]).astype(o_ref.dtype)


def _swa_attention(q, kv, sinks, batch, seq):
    d = q.shape[1]
    nb = seq // BLOCK
    qv = q.reshape(batch, seq, d)
    kvv = kv.reshape(batch, seq, 2 * LANES)

    def kv_spec(col, prev):
        def index(b, blk):
            return (b, jnp.maximum(blk - 1, 0) if prev else blk, col)
        return pl.BlockSpec((None, BLOCK, LANES), index)

    half = sinks.shape[0] // 2
    sink_rows = jnp.repeat(jnp.stack([sinks[:half], sinks[half:]], axis=1), BLOCK, axis=1)
    first_key = jnp.arange(2 * BLOCK)[None, None, :] == 0
    fill = jnp.where(first_key, sink_rows[:, :, None].astype(F32), NEG_INF)

    o = pl.pallas_call(
        _swa_kernel,
        grid=(batch, nb),
        in_specs=[pl.BlockSpec(fill.shape, lambda b, blk: (0, 0, 0)),
                  pl.BlockSpec((None, BLOCK, d), lambda b, blk: (b, blk, 0)),
                  kv_spec(0, True), kv_spec(0, False), kv_spec(1, True), kv_spec(1, False)],
        out_specs=pl.BlockSpec((None, BLOCK, d), lambda b, blk: (b, blk, 0)),
        out_shape=jax.ShapeDtypeStruct((batch, seq, d), BF16),
        compiler_params=_params("parallel", "arbitrary"),
        name="swa_sink_attn",
    )(fill, qv, kvv, kvv, kvv, kvv)
    return o.reshape(batch * seq, d)


def _interleave(src_ref, dst_ref):
    dil, sub = src_ref.shape[0], src_ref.shape[1]
    chunks = dst_ref.shape[0]
    for r in range(dil):
        for c in range(chunks):
            part = src_ref[r, :, c * LANES:(c + 1) * LANES].astype(F32)
            dst_ref[c, pl.ds(r, sub, stride=dil), :] = part
    return jnp.concatenate([dst_ref[c] for c in range(chunks)], axis=1)


def _merge_wo_kernel(x_ref, o0_ref, o1_ref, o2_ref, s0_ref, s1_ref, s2_ref, e_ref, w_ref, out_ref,
                     ob_ref, sb1_ref, sb2_ref):
    s0, s1, s2 = s0_ref[0], _interleave(s1_ref, sb1_ref), _interleave(s2_ref, sb2_ref)
    top = jnp.maximum(jnp.maximum(s0, s1), s2)
    e0, e1, e2 = jnp.exp(s0 - top), jnp.exp(s1 - top), jnp.exp(s2 - top)
    inv = 1.0 / (e0 + e1 + e2)
    expand = e_ref[...]

    def weight(e):
        return jnp.dot((e * inv).astype(BF16), expand, preferred_element_type=F32)

    merged = weight(e0) * o0_ref[0].astype(F32)
    merged = merged + weight(e1) * _interleave(o1_ref, ob_ref)
    merged = merged + weight(e2) * _interleave(o2_ref, ob_ref)
    out_ref[...] = x_ref[...] + jnp.dot(merged.astype(BF16), w_ref[...], preferred_element_type=F32)


def _merge_wo(x, outs, stats, expand, w, batch, seq):
    t, d = x.shape
    tiles_per_seq = seq // TM
    row = lambda i: (i, 0)
    fixed = lambda i: (0, 0)

    def sub(arr):
        dil = arr.shape[1]
        return pl.BlockSpec((None, dil, TM // dil, arr.shape[3]),
                            lambda i: (i // tiles_per_seq, 0, i % tiles_per_seq, 0))

    return pl.pallas_call(
        _merge_wo_kernel,
        grid=(t // TM,),
        in_specs=[pl.BlockSpec((TM, d), row)]
        + [sub(o) for o in outs] + [sub(s) for s in stats]
        + [pl.BlockSpec(expand.shape, fixed), pl.BlockSpec(w.shape, fixed)],
        out_specs=pl.BlockSpec((TM, d), row),
        out_shape=jax.ShapeDtypeStruct((t, d), F32),
        scratch_shapes=[pltpu.VMEM((d // LANES, TM, LANES), F32), pltpu.VMEM((1, TM, LANES), F32),
                        pltpu.VMEM((1, TM, LANES), F32)],
        compiler_params=_params("parallel"),
        name="merge_wo",
    )(x, *outs, *stats, expand, w)


def _wo_router_kernel(x_ref, o_ref, w_ref, g_ref, wr_ref, x1_ref, x1t_ref, ids_ref, gates_ref):
    x1 = x_ref[...] + jnp.dot(o_ref[...], w_ref[...], preferred_element_type=F32)
    x1_ref[...] = x1
    _to_token_tiles(x1, x1t_ref)
    h = _rms(x1, g_ref[...]).astype(BF16)
    logits = jnp.dot(h, wr_ref[...], preferred_element_type=F32)
    lane = lax.broadcasted_iota(jnp.int32, logits.shape, 1)
    logits = jnp.where(lane < N_EXPERTS, logits, -jnp.inf)
    v1 = jnp.max(logits, axis=-1, keepdims=True)
    i1 = jnp.min(jnp.where(logits == v1, lane, LANES), axis=-1, keepdims=True)
    rest = jnp.where(lane == i1, -jnp.inf, logits)
    v2 = jnp.max(rest, axis=-1, keepdims=True)
    i2 = jnp.min(jnp.where(rest == v2, lane, LANES), axis=-1, keepdims=True)
    e2 = jnp.exp(v2 - v1)
    inv = 1.0 / (1.0 + e2)
    ids = jnp.where(lane == 0, i1.astype(F32), 0.0)
    ids_ref[...] = jnp.where(lane == 1, i2.astype(F32), ids)
    gates_ref[:, :LANES] = jnp.broadcast_to(inv, logits.shape)
    gates_ref[:, LANES:] = jnp.broadcast_to(e2 * inv, logits.shape)


def _wo_router(x, o, w, g, wr):
    t, d = x.shape
    row = lambda i: (i, 0)
    fixed = lambda i: (0, 0)
    return pl.pallas_call(
        _wo_router_kernel,
        grid=(t // TM,),
        in_specs=[pl.BlockSpec((TM, d), row), pl.BlockSpec((TM, o.shape[1]), row),
                  pl.BlockSpec(w.shape, fixed), pl.BlockSpec((1, d), fixed),
                  pl.BlockSpec(wr.shape, fixed)],
        out_specs=[pl.BlockSpec((TM, d), row), pl.BlockSpec((TM * SUBLANES, LANES), row),
                   pl.BlockSpec((TM, LANES), row), pl.BlockSpec((TM, 2 * LANES), row)],
        out_shape=[jax.ShapeDtypeStruct((t, d), F32),
                   jax.ShapeDtypeStruct((t * SUBLANES, LANES), F32),
                   jax.ShapeDtypeStruct((t, LANES), F32),
                   jax.ShapeDtypeStruct((t, 2 * LANES), F32)],
        compiler_params=_params("parallel"),
        name="wo_router",
    )(x, o, w, g, wr)


def _swiglu_step(h, w1_ref, w3_ref, w2_ref, acc_ref):
    a = jnp.dot(h, w1_ref[...], preferred_element_type=F32)
    b = jnp.dot(h, w3_ref[...], preferred_element_type=F32)
    act = (a * jax.nn.sigmoid(a) * b).astype(BF16)
    acc_ref[...] += jnp.dot(act, w2_ref[...], preferred_element_type=F32)


def _ffn_kernel(x_ref, g_ref, w1_ref, w3_ref, w2_ref, o_ref, h_ref, acc_ref):
    j = pl.program_id(1)

    @pl.when(j == 0)
    def _():
        h_ref[...] = _rms(x_ref[...], g_ref[...]).astype(BF16)
        acc_ref[...] = jnp.zeros_like(acc_ref)

    _swiglu_step(h_ref[...], w1_ref, w3_ref, w2_ref, acc_ref)

    @pl.when(j == pl.num_programs(1) - 1)
    def _():
        o_ref[...] = x_ref[...] + acc_ref[...]


def _ffn(x, g, w1, w3, w2, *, tf):
    t, d = x.shape
    f = w1.shape[1]
    return pl.pallas_call(
        _ffn_kernel,
        grid=(t // TM, f // tf),
        in_specs=[pl.BlockSpec((TM, d), lambda i, j: (i, 0)),
                  pl.BlockSpec((1, d), lambda i, j: (0, 0)),
                  pl.BlockSpec((d, tf), lambda i, j: (0, j)),
                  pl.BlockSpec((d, tf), lambda i, j: (0, j)),
                  pl.BlockSpec((tf, d), lambda i, j: (j, 0))],
        out_specs=pl.BlockSpec((TM, d), lambda i, j: (i, 0)),
        out_shape=jax.ShapeDtypeStruct((t, d), F32),
        scratch_shapes=[pltpu.VMEM((TM, d), BF16), pltpu.VMEM((TM, d), F32)],
        compiler_params=_params("parallel", "arbitrary"),
        name="dense_swiglu",
    )(x, g, w1, w3, w2)


def _moe_kernel(te_ref, nu_ref, xs_ref, g_ref, w1_ref, w3_ref, w2_ref, o_ref, h_ref, acc_ref):
    i, j = pl.program_id(0), pl.program_id(1)
    used = i < nu_ref[0]
    last = j == pl.num_programs(1) - 1
    rows = h_ref.shape[0]

    @pl.when(used & (j == 0))
    def _():
        h_ref[...] = _rms(_from_token_tiles(xs_ref, rows), g_ref[...]).astype(BF16)
        acc_ref[...] = jnp.zeros_like(acc_ref)

    @pl.when(used)
    def _():
        _swiglu_step(h_ref[...], w1_ref, w3_ref, w2_ref, acc_ref)

    @pl.when(used & last)
    def _():
        _to_token_tiles(acc_ref[...], o_ref)

    @pl.when(jnp.logical_not(used) & last)
    def _():
        o_ref[...] = jnp.zeros_like(o_ref)


def _moe_experts(xs, g, w1, w3, w2, tile_expert, n_used, n_tiles, *, tf):
    d = w1.shape[1]
    f = w1.shape[2]
    n_j = f // tf
    lines = TM * SUBLANES

    def jj(i, j, nu):
        return jnp.where(i < nu[0], j, n_j - 1)

    grid_spec = pltpu.PrefetchScalarGridSpec(
        num_scalar_prefetch=2,
        grid=(n_tiles, n_j),
        in_specs=[pl.BlockSpec((lines, LANES), lambda i, j, te, nu: (i, 0)),
                  pl.BlockSpec((1, d), lambda i, j, te, nu: (0, 0)),
                  pl.BlockSpec((None, d, tf), lambda i, j, te, nu: (te[i], 0, jj(i, j, nu))),
                  pl.BlockSpec((None, d, tf), lambda i, j, te, nu: (te[i], 0, jj(i, j, nu))),
                  pl.BlockSpec((None, tf, d), lambda i, j, te, nu: (te[i], jj(i, j, nu), 0))],
        out_specs=pl.BlockSpec((lines, LANES), lambda i, j, te, nu: (i, 0)),
        scratch_shapes=[pltpu.VMEM((TM, d), BF16), pltpu.VMEM((TM, d), F32)],
    )
    return pl.pallas_call(
        _moe_kernel,
        grid_spec=grid_spec,
        out_shape=jax.ShapeDtypeStruct((n_tiles * lines, LANES), F32),
        compiler_params=_params("arbitrary", "arbitrary"),
        name="moe_experts",
    )(tile_expert, n_used, xs, g, w1, w3, w2)


def _dispatch_kernel(dst_ref, pad_ref, src_hbm, zero_hbm, out_hbm, sem):
    i = pl.program_id(0)
    rows = dst_ref.shape[2] // 2
    n_pad = pad_ref.shape[0]

    def issue(r, carry):
        base = r * DMA_UNROLL
        idx = [dst_ref[0, 0, 2 * base + q] for q in range(2 * DMA_UNROLL)]
        for u in range(DMA_UNROLL):
            src = _tile_at(src_hbm, (i * rows + base + u) * SUBLANES)
            for k in range(2):
                pltpu.make_async_copy(src, _tile_at(out_hbm, idx[2 * u + k]), sem).start()
        return carry

    lax.fori_loop(0, rows // DMA_UNROLL, issue, 0)
    lines = 2 * rows * SUBLANES
    pltpu.make_async_copy(out_hbm.at[pl.ds(0, lines)], out_hbm.at[pl.ds(0, lines)], sem).wait()

    @pl.when(i == 0)
    def _():
        def fill(r, carry):
            base = r * DMA_UNROLL
            idx = [pad_ref[base + q] for q in range(DMA_UNROLL)]
            for q in range(DMA_UNROLL):
                pltpu.make_async_copy(zero_hbm, _tile_at(out_hbm, idx[q]), sem).start()
            return carry

        lax.fori_loop(0, n_pad // DMA_UNROLL, fill, 0)
        pad_lines = n_pad * SUBLANES
        pltpu.make_async_copy(out_hbm.at[pl.ds(0, pad_lines)], out_hbm.at[pl.ds(0, pad_lines)],
                              sem).wait()


def _dispatch(x_tiles, dest_line, pad_line, n_slots):
    t = x_tiles.shape[0] // SUBLANES
    steps = t // DISPATCH_ROWS
    zero = jnp.zeros((SUBLANES, LANES), x_tiles.dtype)
    return pl.pallas_call(
        _dispatch_kernel,
        grid=(steps,),
        in_specs=[pl.BlockSpec((1, 1, 2 * DISPATCH_ROWS), lambda i: (i, 0, 0), memory_space=pltpu.SMEM),
                  pl.BlockSpec(memory_space=pltpu.SMEM),
                  pl.BlockSpec(memory_space=pl.ANY),
                  pl.BlockSpec(memory_space=pl.ANY)],
        out_specs=pl.BlockSpec(memory_space=pl.ANY),
        out_shape=jax.ShapeDtypeStruct((n_slots * SUBLANES, LANES), x_tiles.dtype),
        scratch_shapes=[pltpu.SemaphoreType.DMA(())],
        compiler_params=_params("arbitrary"),
        name="moe_dispatch",
    )(dest_line.reshape(steps, 1, 2 * DISPATCH_ROWS), pad_line, x_tiles, zero)


def _ple_rows(x, p, g_ref, wg_ref, wp_ref):
    h = _rms(x, g_ref[...]).astype(BF16)
    gate = jax.nn.sigmoid(jnp.dot(h, wg_ref[...], preferred_element_type=F32))
    proj = jnp.dot(p.astype(BF16), wp_ref[...], preferred_element_type=F32)
    return x + proj * gate


def _combine_ple_kernel(cur_ref, nxt_ref, x_ref, gates_ref, p_ref, g_ref, wg_ref, wp_ref, fg_ref,
                        ys_hbm, o_ref, a_ref, b_ref, sem):
    i = pl.program_id(0)
    tm = a_ref.shape[1] // SUBLANES

    def issue(idx_ref, offset, slot):
        def body(r, carry):
            base = r * DMA_UNROLL
            idx = [idx_ref[0, 0, offset + 2 * base + q] for q in range(2 * DMA_UNROLL)]
            for u in range(DMA_UNROLL):
                line = (base + u) * SUBLANES
                pltpu.make_async_copy(_tile_at(ys_hbm, idx[2 * u]), _tile_at(a_ref.at[slot], line),
                                      sem.at[slot]).start()
                pltpu.make_async_copy(_tile_at(ys_hbm, idx[2 * u + 1]), _tile_at(b_ref.at[slot], line),
                                      sem.at[slot]).start()
            return carry

        lax.fori_loop(0, tm // DMA_UNROLL, body, 0)

    def wait(slot):
        whole = ys_hbm.at[pl.ds(0, tm * SUBLANES)]
        pltpu.make_async_copy(whole, a_ref.at[slot], sem.at[slot]).wait()
        pltpu.make_async_copy(whole, b_ref.at[slot], sem.at[slot]).wait()

    def compute(slot):
        rows = slice(slot * tm, (slot + 1) * tm)
        g1, g2 = gates_ref[rows, :LANES], gates_ref[rows, LANES:]
        y = jnp.concatenate(
            [g1 * a_ref[slot, pl.ds(c, tm, stride=SUBLANES), :]
             + g2 * b_ref[slot, pl.ds(c, tm, stride=SUBLANES), :] for c in range(SUBLANES)], axis=1)
        x = _ple_rows(x_ref[rows, :] + y, p_ref[rows, :], g_ref, wg_ref, wp_ref)
        o_ref[rows, :] = _rms(x, fg_ref[...])

    @pl.when(i == 0)
    def _():
        issue(cur_ref, 0, 0)

    issue(cur_ref, 2 * tm, 1)
    wait(0)
    compute(0)

    @pl.when(i + 1 < pl.num_programs(0))
    def _():
        issue(nxt_ref, 0, 0)

    wait(1)
    compute(1)


def _combine_ple(x, gates, ys, dest_line, p, layer, g, wg, wp, fg):
    t, d = x.shape
    tm = TM_COMBINE
    steps = t // (2 * tm)
    row = lambda i: (i, 0)
    fixed = lambda i: (0, 0)
    idx = dest_line.reshape(steps, 1, 4 * tm)
    idx_spec = lambda index: pl.BlockSpec((1, 1, 4 * tm), index, memory_space=pltpu.SMEM)
    return pl.pallas_call(
        _combine_ple_kernel,
        grid=(steps,),
        in_specs=[idx_spec(lambda i: (i, 0, 0)),
                  idx_spec(lambda i: (jnp.minimum(i + 1, steps - 1), 0, 0)),
                  pl.BlockSpec((2 * tm, d), row),
                  pl.BlockSpec((2 * tm, 2 * LANES), row),
                  pl.BlockSpec((None, 2 * tm, p.shape[2]), lambda i: (layer, i, 0)),
                  pl.BlockSpec((1, d), fixed),
                  pl.BlockSpec(wg.shape, fixed),
                  pl.BlockSpec(wp.shape, fixed),
                  pl.BlockSpec((1, d), fixed),
                  pl.BlockSpec(memory_space=pl.ANY)],
        out_specs=pl.BlockSpec((2 * tm, d), row),
        out_shape=jax.ShapeDtypeStruct((t, d), F32),
        scratch_shapes=[pltpu.VMEM((2, tm * SUBLANES, LANES), F32),
                        pltpu.VMEM((2, tm * SUBLANES, LANES), F32),
                        pltpu.SemaphoreType.DMA((2,))],
        compiler_params=_params("arbitrary"),
        name="combine_ple_final",
    )(idx, idx, x, gates, p, g, wg, wp, fg, ys)


def _ple_kernel(x_ref, p_ref, g_ref, wg_ref, wp_ref, o_ref):
    o_ref[...] = _ple_rows(x_ref[...], p_ref[...], g_ref, wg_ref, wp_ref)


def _ple(x, p, layer, g, wg, wp):
    t, d = x.shape
    fixed = lambda i: (0, 0)
    return pl.pallas_call(
        _ple_kernel,
        grid=(t // TM,),
        in_specs=[pl.BlockSpec((TM, d), lambda i: (i, 0)),
                  pl.BlockSpec((None, TM, p.shape[2]), lambda i: (layer, i, 0)),
                  pl.BlockSpec((1, d), fixed),
                  pl.BlockSpec(wg.shape, fixed),
                  pl.BlockSpec(wp.shape, fixed)],
        out_specs=pl.BlockSpec((TM, d), lambda i: (i, 0)),
        out_shape=jax.ShapeDtypeStruct((t, d), F32),
        compiler_params=_params("parallel"),
        name="ple",
    )(x, p, g, wg, wp)


def _routing_tables(ids_f):
    t = ids_f.shape[0]
    flat = ids_f[:, :2].astype(jnp.int32).reshape(-1)
    experts = jnp.arange(N_EXPERTS)
    onehot = (flat[:, None] == experts[None, :]).astype(jnp.int32)
    csum = jnp.cumsum(onehot, axis=0)
    rank = jnp.take_along_axis(csum, flat[:, None], axis=1)[:, 0] - 1
    counts = csum[-1]
    tiles_per = (counts + TM - 1) // TM
    tile_end = jnp.cumsum(tiles_per)
    tile_start = tile_end - tiles_per
    dest = tile_start[flat] * TM + rank
    n_tiles = (2 * t) // TM + N_EXPERTS
    n_used = tile_end[-1]
    tile_ids = jnp.minimum(jnp.arange(n_tiles), n_used - 1)
    tile_expert = jnp.minimum(jnp.sum((tile_ids[:, None] >= tile_end[None, :]).astype(jnp.int32), axis=1),
                              N_EXPERTS - 1)
    pad_sizes = jnp.concatenate([tiles_per * TM - counts, ((n_tiles - n_used) * TM).reshape(1)])
    pad_end = jnp.cumsum(pad_sizes)
    seg_first = jnp.concatenate([tile_start * TM + counts, (n_used * TM).reshape(1)])
    k = jnp.arange(N_EXPERTS * TM)
    seg = jnp.sum((k[:, None] >= pad_end[None, :]).astype(jnp.int32), axis=1)
    pad = seg_first[seg] + k - (pad_end - pad_sizes)[seg]
    to_line = lambda v: (v * SUBLANES).astype(jnp.int32)
    return (to_line(dest), to_line(pad), tile_expert.astype(jnp.int32),
            n_used.reshape(1).astype(jnp.int32), n_tiles)


def kernel(x, p, attn_norm, ffn_norm, a_w_qkv, a_w_o, kv_norm, kv_w, b_w_q, b_sinks, b_w_o,
           dense_w1, dense_w3, dense_w2, moe_router, moe_w1, moe_w3, moe_w2,
           ple_norm, ple_w_gate, ple_w_proj, final_norm):
    batch, seq, d = x.shape
    t = batch * seq
    n_heads = d // HEAD_DIM
    n_groups = len(A_CONFIGS)
    assert d == SUBLANES * LANES and seq % TM == 0 and t % DISPATCH_ROWS == 0
    xr = x.reshape(t, d)
    pr = p.reshape(p.shape[0], t, p.shape[3])
    cos, sin = _rope_tables(seq)
    row = lambda v: v.reshape(1, -1)

    perm_a = _pair_perm([(2 * j, 2 * j + 1) for j in range(n_heads // 2)])
    coef_a, coef_b = _rope_coeffs(cos, sin)
    w = a_w_qkv[0].reshape(d, n_groups, 3, d)
    outs, stats = [], []
    for g, (_, dil) in enumerate(A_CONFIGS):
        w_g = jnp.concatenate([w[:, g, 0][:, perm_a], w[:, g, 1][:, perm_a], w[:, g, 2]],
                              axis=1).astype(BF16)
        qkv = _group_proj(xr, row(attn_norm[0]), w_g, coef_a, coef_b, batch, seq, dil)
        o_g, st_g = _dilated_group(qkv, batch, seq, g)
        outs.append(o_g)
        stats.append(st_g)
    head_of_col = jnp.arange(d) // HEAD_DIM
    expand = (jnp.arange(LANES)[:, None] == head_of_col[None, :]).astype(BF16)
    xr = _merge_wo(xr, outs, stats, expand, a_w_o[0].astype(BF16), batch, seq)
    xr = _ffn(xr, row(ffn_norm[0]), dense_w1[0].astype(BF16), dense_w3[0].astype(BF16),
              dense_w2[0].astype(BF16), tf=dense_w1.shape[2] // 2)
    xr = _ple(xr, pr, 0, row(ple_norm[0]), ple_w_gate[0].astype(BF16), ple_w_proj[0].astype(BF16))

    half = n_heads // 2
    perm_q = _pair_perm([(j, half + j) for j in range(half)])
    perm_k = _pair_perm([(0, 1)])
    kv_cols = B_KV_HEADS * HEAD_DIM
    w_kv = jnp.concatenate([kv_w[:, :kv_cols][:, perm_k], kv_w[:, kv_cols:]], axis=1).astype(BF16)
    q1, kv1 = _qkv1(xr, row(attn_norm[1]), row(kv_norm), b_w_q[0][:, perm_q].astype(BF16), w_kv,
                    cos, sin)
    o1 = _swa_attention(q1, kv1, b_sinks[0], batch, seq)
    o_rows = np.concatenate([np.concatenate([np.arange(j * HEAD_DIM, (j + 1) * HEAD_DIM),
                                             np.arange((half + j) * HEAD_DIM, (half + j + 1) * HEAD_DIM)])
                             for j in range(half)])
    wr = jnp.zeros((d, LANES), F32).at[:, :N_EXPERTS].set(moe_router[0]).astype(BF16)
    xr, x_tiles, ids, gates = _wo_router(xr, o1, b_w_o[0][o_rows].astype(BF16), row(ffn_norm[1]), wr)

    dest_line, pad_line, tile_expert, n_used, n_tiles = _routing_tables(ids)
    xs = _dispatch(x_tiles, dest_line, pad_line, n_tiles * TM)
    ys = _moe_experts(xs, row(ffn_norm[1]), moe_w1[0].astype(BF16), moe_w3[0].astype(BF16),
                      moe_w2[0].astype(BF16), tile_expert, n_used, n_tiles, tf=moe_w1.shape[3] // 2)
    xr = _combine_ple(xr, gates, ys, dest_line, pr, 1, row(ple_norm[1]), ple_w_gate[1].astype(BF16),
                      ple_w_proj[1].astype(BF16), row(final_norm))
    return xr.reshape(batch, seq, d)
```

```python
import functools

import numpy as np
import jax
import jax.numpy as jnp
from jax import lax
from jax.experimental import pallas as pl
from jax.experimental.pallas import tpu as pltpu

F32 = jnp.float32
BF16 = jnp.bfloat16

LANES = 128
SUBLANES = 8
HEAD_DIM = 64
HALF_DIM = HEAD_DIM // 2
BLOCK = 128
A_CONFIGS = ((128, 1), (512, 4), (2048, 16))
B_KV_HEADS = 2
B_WINDOW = 128
N_EXPERTS = 8
ROPE_THETA = 10000.0
EPS = 1e-6
NEG_INF = -1e30
VMEM_LIMIT = 56 * 1024 * 1024

TM = 512
TM_COMBINE = 256
DISPATCH_ROWS = 2048
DMA_UNROLL = 8
SWIGLU_SPLITS = 2

PLAIN, ROPE, ROPE_SCALED = 0, 1, 2
Q_SCALE = HEAD_DIM ** -0.5


def _params(*sem):
    return pltpu.CompilerParams(dimension_semantics=sem, vmem_limit_bytes=VMEM_LIMIT)


def _rms(x, g):
    ms = jnp.mean(x * x, axis=-1, keepdims=True)
    return x * lax.rsqrt(ms + EPS) * g


def _pair_perm(pairs):
    idx = []
    for ha, hb in pairs:
        for base in (0, HALF_DIM):
            idx += [ha * HEAD_DIM + base + d for d in range(HALF_DIM)]
            idx += [hb * HEAD_DIM + base + d for d in range(HALF_DIM)]
    return np.asarray(idx, np.int32)


def _rope_tables(seq):
    pos = jnp.arange(seq, dtype=F32)
    inv = 1.0 / (ROPE_THETA ** (jnp.arange(HALF_DIM, dtype=F32) / HALF_DIM))
    ang = pos[:, None] * inv[None, :]
    cos, sin = jnp.cos(ang), jnp.sin(ang)
    return jnp.tile(cos, (1, 4)), jnp.concatenate([-sin, -sin, sin, sin], axis=1)


def _rope_coeffs(cos, sin):
    a = jnp.stack([cos * Q_SCALE, cos, jnp.ones_like(cos)])
    b = jnp.stack([sin * Q_SCALE, sin, jnp.zeros_like(sin)])
    return a, b


def _store_chunks(acc, cos, sin, o_ref, kinds):
    for c, kind in enumerate(kinds):
        seg = acc[:, c * LANES:(c + 1) * LANES]
        if kind != PLAIN:
            seg = seg * cos + pltpu.roll(seg, LANES // 2, 1) * sin
            if kind == ROPE_SCALED:
                seg = seg * Q_SCALE
        o_ref[:, c * LANES:(c + 1) * LANES] = seg.astype(o_ref.dtype)


def _to_token_tiles(val, ref):
    rows = val.shape[0]
    for c in range(val.shape[1] // LANES):
        ref[pl.ds(c, rows, stride=SUBLANES), :] = val[:, c * LANES:(c + 1) * LANES]


def _from_token_tiles(ref, rows):
    return jnp.concatenate([ref[pl.ds(c, rows, stride=SUBLANES), :] for c in range(SUBLANES)], axis=1)


def _tile_at(ref, line):
    return ref.at[pl.ds(pl.multiple_of(line, SUBLANES), SUBLANES)]


def _group_proj_kernel(x_ref, g_ref, w_ref, a_ref, b_ref, perm_ref, o_ref, *, dil):
    h = _rms(x_ref[...], g_ref[...]).astype(BF16)
    if dil > 1:
        h = jnp.dot(perm_ref[...], h, preferred_element_type=F32).astype(BF16)
    sub = h.shape[0] // dil
    tn = w_ref.shape[1] // 3
    for kind in range(3):
        acc = jnp.dot(h, w_ref[:, kind * tn:(kind + 1) * tn], preferred_element_type=F32)
        a, b = a_ref[kind], b_ref[kind]
        for c in range(tn // LANES):
            seg = acc[:, c * LANES:(c + 1) * LANES]
            seg = (seg * a + pltpu.roll(seg, LANES // 2, 1) * b).astype(o_ref.dtype)
            cols = slice(kind * tn + c * LANES, kind * tn + (c + 1) * LANES)
            if dil == 1:
                o_ref[:, cols] = seg
            else:
                for r in range(dil):
                    o_ref[r, :, cols] = seg[r * sub:(r + 1) * sub]


def _residue_major_rows(dil):
    q = np.arange(TM)
    sub = TM // dil
    return (q % sub) * dil + q // sub


def _group_proj(x, g, w, coef_a, coef_b, batch, seq, dil):
    t, d = x.shape
    n_out = w.shape[1]
    tiles_per_seq = seq // TM
    if dil == 1:
        out_spec = pl.BlockSpec((TM, n_out), lambda i: (i, 0))
        out_shape = jax.ShapeDtypeStruct((t, n_out), BF16)
    else:
        out_spec = pl.BlockSpec((None, dil, TM // dil, n_out),
                                lambda i: (i // tiles_per_seq, 0, i % tiles_per_seq, 0))
        out_shape = jax.ShapeDtypeStruct((batch, dil, seq // dil, n_out), BF16)
    nat = _residue_major_rows(dil)
    perm = jnp.asarray(nat[:, None] == np.arange(TM)[None, :], BF16)
    rows = (np.arange(seq) // TM) * TM + nat[np.arange(seq) % TM]
    coef_spec = pl.BlockSpec((3, TM, LANES), lambda i: (0, i % tiles_per_seq, 0))
    return pl.pallas_call(
        functools.partial(_group_proj_kernel, dil=dil),
        grid=(t // TM,),
        in_specs=[
            pl.BlockSpec((TM, d), lambda i: (i, 0)),
            pl.BlockSpec((1, d), lambda i: (0, 0)),
            pl.BlockSpec(w.shape, lambda i: (0, 0)),
            coef_spec, coef_spec,
            pl.BlockSpec((TM, TM), lambda i: (0, 0)),
        ],
        out_specs=out_spec,
        out_shape=out_shape,
        compiler_params=_params("parallel"),
        name=f"group_proj_d{dil}",
    )(x, g, w, coef_a[:, rows], coef_b[:, rows], perm)


def _qkv1_kernel(x_ref, gq_ref, gkv_ref, wq_ref, wkv_ref, cos_ref, sin_ref, q_ref, kv_ref):
    x = x_ref[...]
    xn = x * lax.rsqrt(jnp.mean(x * x, axis=-1, keepdims=True) + EPS)
    hq = (xn * gq_ref[...]).astype(BF16)
    hkv = (xn * gkv_ref[...]).astype(BF16)
    cos, sin = cos_ref[...], sin_ref[...]
    accq = jnp.dot(hq, wq_ref[...], preferred_element_type=F32)
    _store_chunks(accq, cos, sin, q_ref, (ROPE_SCALED,) * (accq.shape[1] // LANES))
    acckv = jnp.dot(hkv, wkv_ref[...], preferred_element_type=F32)
    _store_chunks(acckv, cos, sin, kv_ref, (ROPE, PLAIN))


def _qkv1(x, gq, gkv, wq, wkv, cos, sin):
    t, d = x.shape
    s_tiles = cos.shape[0] // TM
    row = lambda i: (i, 0)
    fixed = lambda i: (0, 0)
    return pl.pallas_call(
        _qkv1_kernel,
        grid=(t // TM,),
        in_specs=[
            pl.BlockSpec((TM, d), row),
            pl.BlockSpec((1, d), fixed),
            pl.BlockSpec((1, d), fixed),
            pl.BlockSpec(wq.shape, fixed),
            pl.BlockSpec(wkv.shape, fixed),
            pl.BlockSpec((TM, LANES), lambda i: (i % s_tiles, 0)),
            pl.BlockSpec((TM, LANES), lambda i: (i % s_tiles, 0)),
        ],
        out_specs=[pl.BlockSpec((TM, wq.shape[1]), row), pl.BlockSpec((TM, wkv.shape[1]), row)],
        out_shape=[jax.ShapeDtypeStruct((t, wq.shape[1]), BF16),
                   jax.ShapeDtypeStruct((t, wkv.shape[1]), BF16)],
        compiler_params=_params("parallel"),
        name="qkv_layer1",
    )(x, gq, gkv, wq, wkv, cos, sin)


def _band_mask(blk, max_dist):
    qi = lax.broadcasted_iota(jnp.int32, (BLOCK, 2 * BLOCK), 0) + BLOCK
    kj = lax.broadcasted_iota(jnp.int32, (BLOCK, 2 * BLOCK), 1)
    rel = qi - kj
    mask = (rel >= 0) & (rel <= max_dist) & ((kj >= BLOCK) | (blk > 0))
    return jnp.concatenate([mask, mask], axis=0)


def _lane_masks():
    lane = lax.broadcasted_iota(jnp.int32, (BLOCK, LANES), 1)
    first_qk = (lane % HEAD_DIM) < HALF_DIM
    sel_a = jnp.where(first_qk, 1.0, 0.0).astype(BF16)
    sel_b = jnp.where(first_qk, 0.0, 1.0).astype(BF16)
    return lane, sel_a, sel_b, lane < HEAD_DIM


def _pair_scores(q2, kcat, sel_a, sel_b):
    qs = jnp.concatenate([q2 * sel_a, q2 * sel_b], axis=0)
    return lax.dot_general(qs, kcat, (((1,), (1,)), ((), ())), preferred_element_type=F32)


def _dil_attn_kernel(q_ref, kp_ref, kc_ref, vp_ref, vc_ref, o_ref, st_ref, *, steps):
    mask2 = _band_mask(pl.program_id(2), steps)
    lane, sel_a, sel_b, first_v = _lane_masks()
    stats = jnp.zeros((BLOCK, LANES), F32)
    for p in range(q_ref.shape[1] // LANES):
        sl = slice(p * LANES, (p + 1) * LANES)
        kcat = jnp.concatenate([kp_ref[:, sl], kc_ref[:, sl]], axis=0)
        vcat = jnp.concatenate([vp_ref[:, sl], vc_ref[:, sl]], axis=0)
        s = jnp.where(mask2, _pair_scores(q_ref[:, sl], kcat, sel_a, sel_b), NEG_INF)
        m = jnp.max(s, axis=-1, keepdims=True)
        pe = jnp.exp(s - m)
        den = jnp.sum(pe, axis=-1, keepdims=True)
        pv = jnp.dot(pe.astype(BF16), vcat, preferred_element_type=F32)
        pv = pv * (1.0 / den)
        o_ref[:, sl] = jnp.where(first_v, pv[:BLOCK], pv[BLOCK:]).astype(o_ref.dtype)
        lse = m + jnp.log(den)
        stats = jnp.where(lane == 2 * p, lse[:BLOCK], stats)
        stats = jnp.where(lane == 2 * p + 1, lse[BLOCK:], stats)
    st_ref[...] = stats


def _dilated_group(qkv, batch, seq, g):
    win, dil = A_CONFIGS[g]
    n = seq // dil
    d = qkv.shape[-1] // 3
    view = qkv.reshape(batch, dil, n, 3 * d)

    def spec(kind, prev):
        def index(b, r, blk):
            return (b, r, jnp.maximum(blk - 1, 0) if prev else blk, kind)
        return pl.BlockSpec((None, None, BLOCK, d), index)

    return pl.pallas_call(
        functools.partial(_dil_attn_kernel, steps=win // dil),
        grid=(batch, dil, n // BLOCK),
        in_specs=[spec(0, False), spec(1, True), spec(1, False), spec(2, True), spec(2, False)],
        out_specs=[pl.BlockSpec((None, None, BLOCK, d), lambda b, r, blk: (b, r, blk, 0)),
                   pl.BlockSpec((None, None, BLOCK, LANES), lambda b, r, blk: (b, r, blk, 0))],
        out_shape=[jax.ShapeDtypeStruct((batch, dil, n, d), BF16),
                   jax.ShapeDtypeStruct((batch, dil, n, LANES), F32)],
        compiler_params=_params("parallel", "parallel", "arbitrary"),
        name=f"dilated_attn_g{g}",
    )(view, view, view, view, view)


def _swa_kernel(fill_ref, q_ref, kp_ref, kc_ref, vp_ref, vc_ref, o_ref):
    mask2 = _band_mask(pl.program_id(1), B_WINDOW - 1)
    _, sel_a, sel_b, first_v = _lane_masks()
    kcat = jnp.concatenate([kp_ref[...], kc_ref[...]], axis=0)
    vcat = jnp.concatenate([vp_ref[...], vc_ref[...]], axis=0)
    key_row = lax.broadcasted_iota(jnp.int32, vcat.shape, 0)
    vcat = jnp.where(key_row == 0, jnp.zeros_like(vcat), vcat)
    for j in range(q_ref.shape[1] // LANES):
        sl = slice(j * LANES, (j + 1) * LANES)
        s = jnp.where(mask2, _pair_scores(q_ref[:, sl], kcat, sel_a, sel_b), fill_ref[j])
        m = jnp.max(s, axis=-1, keepdims=True)
        pe = jnp.exp(s - m)
        den = jnp.sum(pe, axis=-1, keepdims=True)
        pv = jnp.dot(pe.astype(BF16), vcat, preferred_element_type=F32)
        pv = pv * (1.0 / den)
        o_ref[:, sl] = jnp.where(first_v, pv[:BLOCK], pv[BLOCK:]).astype(o_ref.dtype)


def _swa_attention(q, kv, sinks, batch, seq):
    d = q.shape[1]
    nb = seq // BLOCK
    qv = q.reshape(batch, seq, d)
    kvv = kv.reshape(batch, seq, 2 * LANES)

    def kv_spec(col, prev):
        def index(b, blk):
            return (b, jnp.maximum(blk - 1, 0) if prev else blk, col)
        return pl.BlockSpec((None, BLOCK, LANES), index)

    half = sinks.shape[0] // 2
    sink_rows = jnp.repeat(jnp.stack([sinks[:half], sinks[half:]], axis=1), BLOCK, axis=1)
    first_key = jnp.arange(2 * BLOCK)[None, None, :] == 0
    fill = jnp.where(first_key, sink_rows[:, :, None].astype(F32), NEG_INF)

    o = pl.pallas_call(
        _swa_kernel,
        grid=(batch, nb),
        in_specs=[pl.BlockSpec(fill.shape, lambda b, blk: (0, 0, 0)),
                  pl.BlockSpec((None, BLOCK, d), lambda b, blk: (b, blk, 0)),
                  kv_spec(0, True), kv_spec(0, False), kv_spec(1, True), kv_spec(1, False)],
        out_specs=pl.BlockSpec((None, BLOCK, d), lambda b, blk: (b, blk, 0)),
        out_shape=jax.ShapeDtypeStruct((batch, seq, d), BF16),
        compiler_params=_params("parallel", "arbitrary"),
        name="swa_sink_attn",
    )(fill, qv, kvv, kvv, kvv, kvv)
    return o.reshape(batch * seq, d)


def _interleave(src_ref, dst_ref):
    dil, sub = src_ref.shape[0], src_ref.shape[1]
    chunks = dst_ref.shape[0]
    for r in range(dil):
        for c in range(chunks):
            part = src_ref[r, :, c * LANES:(c + 1) * LANES].astype(F32)
            dst_ref[c, pl.ds(r, sub, stride=dil), :] = part
    return jnp.concatenate([dst_ref[c] for c in range(chunks)], axis=1)


def _merge_wo_kernel(x_ref, o0_ref, o1_ref, o2_ref, s0_ref, s1_ref, s2_ref, e_ref, w_ref, out_ref,
                     ob_ref, sb1_ref, sb2_ref):
    s0, s1, s2 = s0_ref[0], _interleave(s1_ref, sb1_ref), _interleave(s2_ref, sb2_ref)
    top = jnp.maximum(jnp.maximum(s0, s1), s2)
    e0, e1, e2 = jnp.exp(s0 - top), jnp.exp(s1 - top), jnp.exp(s2 - top)
    inv = 1.0 / (e0 + e1 + e2)
    expand = e_ref[...]

    def weight(e):
        return jnp.dot((e * inv).astype(BF16), expand, preferred_element_type=F32)

    merged = weight(e0) * o0_ref[0].astype(F32)
    merged = merged + weight(e1) * _interleave(o1_ref, ob_ref)
    merged = merged + weight(e2) * _interleave(o2_ref, ob_ref)
    out_ref[...] = x_ref[...] + jnp.dot(merged.astype(BF16), w_ref[...], preferred_element_type=F32)


def _merge_wo(x, outs, stats, expand, w, batch, seq):
    t, d = x.shape
    tiles_per_seq = seq // TM
    row = lambda i: (i, 0)
    fixed = lambda i: (0, 0)

    def sub(arr):
        dil = arr.shape[1]
        return pl.BlockSpec((None, dil, TM // dil, arr.shape[3]),
                            lambda i: (i // tiles_per_seq, 0, i % tiles_per_seq, 0))

    return pl.pallas_call(
        _merge_wo_kernel,
        grid=(t // TM,),
        in_specs=[pl.BlockSpec((TM, d), row)]
        + [sub(o) for o in outs] + [sub(s) for s in stats]
        + [pl.BlockSpec(expand.shape, fixed), pl.BlockSpec(w.shape, fixed)],
        out_specs=pl.BlockSpec((TM, d), row),
        out_shape=jax.ShapeDtypeStruct((t, d), F32),
        scratch_shapes=[pltpu.VMEM((d // LANES, TM, LANES), F32), pltpu.VMEM((1, TM, LANES), F32),
                        pltpu.VMEM((1, TM, LANES), F32)],
        compiler_params=_params("parallel"),
        name="merge_wo",
    )(x, *outs, *stats, expand, w)


def _wo_router_kernel(x_ref, o_ref, w_ref, g_ref, wr_ref, x1_ref, x1t_ref, ids_ref, gates_ref):
    x1 = x_ref[...] + jnp.dot(o_ref[...], w_ref[...], preferred_element_type=F32)
    x1_ref[...] = x1
    _to_token_tiles(x1, x1t_ref)
    h = _rms(x1, g_ref[...]).astype(BF16)
    logits = jnp.dot(h, wr_ref[...], preferred_element_type=F32)
    lane = lax.broadcasted_iota(jnp.int32, logits.shape, 1)
    logits = jnp.where(lane < N_EXPERTS, logits, -jnp.inf)
    v1 = jnp.max(logits, axis=-1, keepdims=True)
    i1 = jnp.min(jnp.where(logits == v1, lane, LANES), axis=-1, keepdims=True)
    rest = jnp.where(lane == i1, -jnp.inf, logits)
    v2 = jnp.max(rest, axis=-1, keepdims=True)
    i2 = jnp.min(jnp.where(rest == v2, lane, LANES), axis=-1, keepdims=True)
    e2 = jnp.exp(v2 - v1)
    inv = 1.0 / (1.0 + e2)
    ids = jnp.where(lane == 0, i1.astype(F32), 0.0)
    ids_ref[...] = jnp.where(lane == 1, i2.astype(F32), ids)
    gates_ref[:, :LANES] = jnp.broadcast_to(inv, logits.shape)
    gates_ref[:, LANES:] = jnp.broadcast_to(e2 * inv, logits.shape)


def _wo_router(x, o, w, g, wr):
    t, d = x.shape
    row = lambda i: (i, 0)
    fixed = lambda i: (0, 0)
    return pl.pallas_call(
        _wo_router_kernel,
        grid=(t // TM,),
        in_specs=[pl.BlockSpec((TM, d), row), pl.BlockSpec((TM, o.shape[1]), row),
                  pl.BlockSpec(w.shape, fixed), pl.BlockSpec((1, d), fixed),
                  pl.BlockSpec(wr.shape, fixed)],
        out_specs=[pl.BlockSpec((TM, d), row), pl.BlockSpec((TM * SUBLANES, LANES), row),
                   pl.BlockSpec((TM, LANES), row), pl.BlockSpec((TM, 2 * LANES), row)],
        out_shape=[jax.ShapeDtypeStruct((t, d), F32),
                   jax.ShapeDtypeStruct((t * SUBLANES, LANES), F32),
                   jax.ShapeDtypeStruct((t, LANES), F32),
                   jax.ShapeDtypeStruct((t, 2 * LANES), F32)],
        compiler_params=_params("parallel"),
        name="wo_router",
    )(x, o, w, g, wr)


def _swiglu(h, w1_ref, w3_ref, w2_ref):
    width = w1_ref.shape[1] // SWIGLU_SPLITS
    y = None
    for c in range(SWIGLU_SPLITS):
        sl = slice(c * width, (c + 1) * width)
        a = jnp.dot(h, w1_ref[:, sl], preferred_element_type=F32)
        b = jnp.dot(h, w3_ref[:, sl], preferred_element_type=F32)
        act = (a * jax.nn.sigmoid(a) * b).astype(BF16)
        part = jnp.dot(act, w2_ref[sl, :], preferred_element_type=F32)
        y = part if y is None else y + part
    return y


def _ffn_kernel(x_ref, g_ref, w1_ref, w3_ref, w2_ref, o_ref):
    x = x_ref[...]
    o_ref[...] = x + _swiglu(_rms(x, g_ref[...]).astype(BF16), w1_ref, w3_ref, w2_ref)


def _ffn(x, g, w1, w3, w2):
    t, d = x.shape
    fixed = lambda i: (0, 0)
    resident = pl.Buffered(1)
    return pl.pallas_call(
        _ffn_kernel,
        grid=(t // TM,),
        in_specs=[pl.BlockSpec((TM, d), lambda i: (i, 0)),
                  pl.BlockSpec((1, d), fixed),
                  pl.BlockSpec(w1.shape, fixed, pipeline_mode=resident),
                  pl.BlockSpec(w3.shape, fixed, pipeline_mode=resident),
                  pl.BlockSpec(w2.shape, fixed, pipeline_mode=resident)],
        out_specs=pl.BlockSpec((TM, d), lambda i: (i, 0)),
        out_shape=jax.ShapeDtypeStruct((t, d), F32),
        compiler_params=_params("parallel"),
        name="dense_swiglu",
    )(x, g, w1, w3, w2)


def _moe_kernel(te_ref, nu_ref, xs_ref, g_ref, w1_ref, w3_ref, w2_ref, o_ref):
    used = pl.program_id(0) < nu_ref[0]

    @pl.when(used)
    def _():
        x = _from_token_tiles(xs_ref, xs_ref.shape[0] // SUBLANES)
        _to_token_tiles(_swiglu(_rms(x, g_ref[...]).astype(BF16), w1_ref, w3_ref, w2_ref), o_ref)

    @pl.when(jnp.logical_not(used))
    def _():
        o_ref[...] = jnp.zeros_like(o_ref)


def _moe_experts(xs, g, w1, w3, w2, tile_expert, n_used, n_tiles):
    d, f = w1.shape[1], w1.shape[2]
    lines = TM * SUBLANES
    resident = pl.Buffered(1)
    grid_spec = pltpu.PrefetchScalarGridSpec(
        num_scalar_prefetch=2,
        grid=(n_tiles,),
        in_specs=[pl.BlockSpec((lines, LANES), lambda i, te, nu: (i, 0)),
                  pl.BlockSpec((1, d), lambda i, te, nu: (0, 0)),
                  pl.BlockSpec((None, d, f), lambda i, te, nu: (te[i], 0, 0), pipeline_mode=resident),
                  pl.BlockSpec((None, d, f), lambda i, te, nu: (te[i], 0, 0), pipeline_mode=resident),
                  pl.BlockSpec((None, f, d), lambda i, te, nu: (te[i], 0, 0), pipeline_mode=resident)],
        out_specs=pl.BlockSpec((lines, LANES), lambda i, te, nu: (i, 0)),
    )
    return pl.pallas_call(
        _moe_kernel,
        grid_spec=grid_spec,
        out_shape=jax.ShapeDtypeStruct((n_tiles * lines, LANES), F32),
        compiler_params=_params("arbitrary"),
        name="moe_experts",
    )(tile_expert, n_used, xs, g, w1, w3, w2)


def _dispatch_kernel(dst_ref, pad_ref, src_ref, zero_ref, out_hbm, sem):
    rows = dst_ref.shape[2] // 2
    n_pad = pad_ref.shape[0]

    def issue(r, carry):
        base = r * DMA_UNROLL
        idx = [dst_ref[0, 0, 2 * base + q] for q in range(2 * DMA_UNROLL)]
        for u in range(DMA_UNROLL):
            src = _tile_at(src_ref, (base + u) * SUBLANES)
            for k in range(2):
                pltpu.make_async_copy(src, _tile_at(out_hbm, idx[2 * u + k]), sem).start()
        return carry

    lax.fori_loop(0, rows // DMA_UNROLL, issue, 0)
    lines = 2 * rows * SUBLANES
    pltpu.make_async_copy(out_hbm.at[pl.ds(0, lines)], out_hbm.at[pl.ds(0, lines)], sem).wait()

    @pl.when(pl.program_id(0) == 0)
    def _():
        def fill(r, carry):
            base = r * DMA_UNROLL
            idx = [pad_ref[base + q] for q in range(DMA_UNROLL)]
            for q in range(DMA_UNROLL):
                pltpu.make_async_copy(zero_ref, _tile_at(out_hbm, idx[q]), sem).start()
            return carry

        lax.fori_loop(0, n_pad // DMA_UNROLL, fill, 0)
        pad_lines = n_pad * SUBLANES
        pltpu.make_async_copy(out_hbm.at[pl.ds(0, pad_lines)], out_hbm.at[pl.ds(0, pad_lines)],
                              sem).wait()


def _dispatch(x_tiles, dest_line, pad_line, n_slots):
    t = x_tiles.shape[0] // SUBLANES
    steps = t // DISPATCH_ROWS
    zero = jnp.zeros((SUBLANES, LANES), x_tiles.dtype)
    return pl.pallas_call(
        _dispatch_kernel,
        grid=(steps,),
        in_specs=[pl.BlockSpec((1, 1, 2 * DISPATCH_ROWS), lambda i: (i, 0, 0), memory_space=pltpu.SMEM),
                  pl.BlockSpec(memory_space=pltpu.SMEM),
                  pl.BlockSpec((DISPATCH_ROWS * SUBLANES, LANES), lambda i: (i, 0)),
                  pl.BlockSpec((SUBLANES, LANES), lambda i: (0, 0))],
        out_specs=pl.BlockSpec(memory_space=pl.ANY),
        out_shape=jax.ShapeDtypeStruct((n_slots * SUBLANES, LANES), x_tiles.dtype),
        scratch_shapes=[pltpu.SemaphoreType.DMA(())],
        compiler_params=_params("arbitrary"),
        name="moe_dispatch",
    )(dest_line.reshape(steps, 1, 2 * DISPATCH_ROWS), pad_line, x_tiles, zero)


def _ple_rows(x, p, g_ref, wg_ref, wp_ref):
    h = _rms(x, g_ref[...]).astype(BF16)
    gate = jax.nn.sigmoid(jnp.dot(h, wg_ref[...], preferred_element_type=F32))
    proj = jnp.dot(p.astype(BF16), wp_ref[...], preferred_element_type=F32)
    return x + proj * gate


def _combine_ple_kernel(cur_ref, nxt_ref, x_ref, gates_ref, p_ref, g_ref, wg_ref, wp_ref, fg_ref,
                        ys_hbm, o_ref, a_ref, b_ref, sem):
    i = pl.program_id(0)
    tm = a_ref.shape[1] // SUBLANES

    def issue(idx_ref, offset, slot):
        def body(r, carry):
            base = r * DMA_UNROLL
            idx = [idx_ref[0, 0, offset + 2 * base + q] for q in range(2 * DMA_UNROLL)]
            for u in range(DMA_UNROLL):
                line = (base + u) * SUBLANES
                pltpu.make_async_copy(_tile_at(ys_hbm, idx[2 * u]), _tile_at(a_ref.at[slot], line),
                                      sem.at[slot]).start()
                pltpu.make_async_copy(_tile_at(ys_hbm, idx[2 * u + 1]), _tile_at(b_ref.at[slot], line),
                                      sem.at[slot]).start()
            return carry

        lax.fori_loop(0, tm // DMA_UNROLL, body, 0)

    def wait(slot):
        whole = ys_hbm.at[pl.ds(0, tm * SUBLANES)]
        pltpu.make_async_copy(whole, a_ref.at[slot], sem.at[slot]).wait()
        pltpu.make_async_copy(whole, b_ref.at[slot], sem.at[slot]).wait()

    def compute(slot):
        rows = slice(slot * tm, (slot + 1) * tm)
        g1, g2 = gates_ref[rows, :LANES], gates_ref[rows, LANES:]
        y = jnp.concatenate(
            [g1 * a_ref[slot, pl.ds(c, tm, stride=SUBLANES), :]
             + g2 * b_ref[slot, pl.ds(c, tm, stride=SUBLANES), :] for c in range(SUBLANES)], axis=1)
        x = _ple_rows(x_ref[rows, :] + y, p_ref[rows, :], g_ref, wg_ref, wp_ref)
        o_ref[rows, :] = _rms(x, fg_ref[...])

    @pl.when(i == 0)
    def _():
        issue(cur_ref, 0, 0)

    issue(cur_ref, 2 * tm, 1)
    wait(0)
    compute(0)

    @pl.when(i + 1 < pl.num_programs(0))
    def _():
        issue(nxt_ref, 0, 0)

    wait(1)
    compute(1)


def _combine_ple(x, gates, ys, dest_line, p, layer, g, wg, wp, fg):
    t, d = x.shape
    tm = TM_COMBINE
    steps = t // (2 * tm)
    row = lambda i: (i, 0)
    fixed = lambda i: (0, 0)
    idx = dest_line.reshape(steps, 1, 4 * tm)
    idx_spec = lambda index: pl.BlockSpec((1, 1, 4 * tm), index, memory_space=pltpu.SMEM)
    return pl.pallas_call(
        _combine_ple_kernel,
        grid=(steps,),
        in_specs=[idx_spec(lambda i: (i, 0, 0)),
                  idx_spec(lambda i: (jnp.minimum(i + 1, steps - 1), 0, 0)),
                  pl.BlockSpec((2 * tm, d), row),
                  pl.BlockSpec((2 * tm, 2 * LANES), row),
                  pl.BlockSpec((None, 2 * tm, p.shape[2]), lambda i: (layer, i, 0)),
                  pl.BlockSpec((1, d), fixed),
                  pl.BlockSpec(wg.shape, fixed),
                  pl.BlockSpec(wp.shape, fixed),
                  pl.BlockSpec((1, d), fixed),
                  pl.BlockSpec(memory_space=pl.ANY)],
        out_specs=pl.BlockSpec((2 * tm, d), row),
        out_shape=jax.ShapeDtypeStruct((t, d), F32),
        scratch_shapes=[pltpu.VMEM((2, tm * SUBLANES, LANES), F32),
                        pltpu.VMEM((2, tm * SUBLANES, LANES), F32),
                        pltpu.SemaphoreType.DMA((2,))],
        compiler_params=_params("arbitrary"),
        name="combine_ple_final",
    )(idx, idx, x, gates, p, g, wg, wp, fg, ys)


def _ple_kernel(x_ref, p_ref, g_ref, wg_ref, wp_ref, o_ref):
    o_ref[...] = _ple_rows(x_ref[...], p_ref[...], g_ref, wg_ref, wp_ref)


def _ple(x, p, layer, g, wg, wp):
    t, d = x.shape
    fixed = lambda i: (0, 0)
    return pl.pallas_call(
        _ple_kernel,
        grid=(t // TM,),
        in_specs=[pl.BlockSpec((TM, d), lambda i: (i, 0)),
                  pl.BlockSpec((None, TM, p.shape[2]), lambda i: (layer, i, 0)),
                  pl.BlockSpec((1, d), fixed),
                  pl.BlockSpec(wg.shape, fixed),
                  pl.BlockSpec(wp.shape, fixed)],
        out_specs=pl.BlockSpec((TM, d), lambda i: (i, 0)),
        out_shape=jax.ShapeDtypeStruct((t, d), F32),
        compiler_params=_params("parallel"),
        name="ple",
    )(x, p, g, wg, wp)


def _routing_tables(ids_f):
    t = ids_f.shape[0]
    flat = ids_f[:, :2].astype(jnp.int32).reshape(-1)
    experts = jnp.arange(N_EXPERTS)
    onehot = (flat[:, None] == experts[None, :]).astype(jnp.int32)
    csum = jnp.cumsum(onehot, axis=0)
    rank = jnp.take_along_axis(csum, flat[:, None], axis=1)[:, 0] - 1
    counts = csum[-1]
    tiles_per = (counts + TM - 1) // TM
    tile_end = jnp.cumsum(tiles_per)
    tile_start = tile_end - tiles_per
    dest = tile_start[flat] * TM + rank
    n_tiles = (2 * t) // TM + N_EXPERTS
    n_used = tile_end[-1]
    tile_ids = jnp.minimum(jnp.arange(n_tiles), n_used - 1)
    tile_expert = jnp.minimum(jnp.sum((tile_ids[:, None] >= tile_end[None, :]).astype(jnp.int32), axis=1),
                              N_EXPERTS - 1)
    pad_sizes = jnp.concatenate([tiles_per * TM - counts, ((n_tiles - n_used) * TM).reshape(1)])
    pad_end = jnp.cumsum(pad_sizes)
    seg_first = jnp.concatenate([tile_start * TM + counts, (n_used * TM).reshape(1)])
    k = jnp.arange(N_EXPERTS * TM)
    seg = jnp.sum((k[:, None] >= pad_end[None, :]).astype(jnp.int32), axis=1)
    pad = seg_first[seg] + k - (pad_end - pad_sizes)[seg]
    to_line = lambda v: (v * SUBLANES).astype(jnp.int32)
    return (to_line(dest), to_line(pad), tile_expert.astype(jnp.int32),
            n_used.reshape(1).astype(jnp.int32), n_tiles)


def kernel(x, p, attn_norm, ffn_norm, a_w_qkv, a_w_o, kv_norm, kv_w, b_w_q, b_sinks, b_w_o,
           dense_w1, dense_w3, dense_w2, moe_router, moe_w1, moe_w3, moe_w2,
           ple_norm, ple_w_gate, ple_w_proj, final_norm):
    batch, seq, d = x.shape
    t = batch * seq
    n_heads = d // HEAD_DIM
    n_groups = len(A_CONFIGS)
    assert d == SUBLANES * LANES and seq % TM == 0 and t % DISPATCH_ROWS == 0
    xr = x.reshape(t, d)
    pr = p.reshape(p.shape[0], t, p.shape[3])
    cos, sin = _rope_tables(seq)
    row = lambda v: v.reshape(1, -1)

    perm_a = _pair_perm([(2 * j, 2 * j + 1) for j in range(n_heads // 2)])
    coef_a, coef_b = _rope_coeffs(cos, sin)
    w = a_w_qkv[0].reshape(d, n_groups, 3, d)
    outs, stats = [], []
    for g, (_, dil) in enumerate(A_CONFIGS):
        w_g = jnp.concatenate([w[:, g, 0][:, perm_a], w[:, g, 1][:, perm_a], w[:, g, 2]],
                              axis=1).astype(BF16)
        qkv = _group_proj(xr, row(attn_norm[0]), w_g, coef_a, coef_b, batch, seq, dil)
        o_g, st_g = _dilated_group(qkv, batch, seq, g)
        outs.append(o_g)
        stats.append(st_g)
    head_of_col = jnp.arange(d) // HEAD_DIM
    expand = (jnp.arange(LANES)[:, None] == head_of_col[None, :]).astype(BF16)
    xr = _merge_wo(xr, outs, stats, expand, a_w_o[0].astype(BF16), batch, seq)
    xr = _ffn(xr, row(ffn_norm[0]), dense_w1[0].astype(BF16), dense_w3[0].astype(BF16),
              dense_w2[0].astype(BF16))
    xr = _ple(xr, pr, 0, row(ple_norm[0]), ple_w_gate[0].astype(BF16), ple_w_proj[0].astype(BF16))

    half = n_heads // 2
    perm_q = _pair_perm([(j, half + j) for j in range(half)])
    perm_k = _pair_perm([(0, 1)])
    kv_cols = B_KV_HEADS * HEAD_DIM
    w_kv = jnp.concatenate([kv_w[:, :kv_cols][:, perm_k], kv_w[:, kv_cols:]], axis=1).astype(BF16)
    q1, kv1 = _qkv1(xr, row(attn_norm[1]), row(kv_norm), b_w_q[0][:, perm_q].astype(BF16), w_kv,
                    cos, sin)
    o1 = _swa_attention(q1, kv1, b_sinks[0], batch, seq)
    o_rows = np.concatenate([np.concatenate([np.arange(j * HEAD_DIM, (j + 1) * HEAD_DIM),
                                             np.arange((half + j) * HEAD_DIM, (half + j + 1) * HEAD_DIM)])
                             for j in range(half)])
    wr = jnp.zeros((d, LANES), F32).at[:, :N_EXPERTS].set(moe_router[0]).astype(BF16)
    xr, x_tiles, ids, gates = _wo_router(xr, o1, b_w_o[0][o_rows].astype(BF16), row(ffn_norm[1]), wr)

    dest_line, pad_line, tile_expert, n_used, n_tiles = _routing_tables(ids)
    xs = _dispatch(x_tiles, dest_line, pad_line, n_tiles * TM)
    ys = _moe_experts(xs, row(ffn_norm[1]), moe_w1[0].astype(BF16), moe_w3[0].astype(BF16),
                      moe_w2[0].astype(BF16), tile_expert, n_used, n_tiles)
    xr = _combine_ple(xr, gates, ys, dest_line, pr, 1, row(ple_norm[1]), ple_w_gate[1].astype(BF16),
                      ple_w_proj[1].astype(BF16), row(final_norm))
    return xr.reshape(batch, seq, d)
```

```python
import functools

import numpy as np
import jax
import jax.numpy as jnp
from jax import lax
from jax.experimental import pallas as pl
from jax.experimental.pallas import tpu as pltpu

F32 = jnp.float32
BF16 = jnp.bfloat16

LANES = 128
SUBLANES = 8
HEAD_DIM = 64
HALF_DIM = HEAD_DIM // 2
BLOCK = 128
Q_BLOCKS = 2
A_CONFIGS = ((128, 1), (512, 4), (2048, 16))
B_KV_HEADS = 2
B_WINDOW = 128
N_EXPERTS = 8
ROPE_THETA = 10000.0
EPS = 1e-6
NEG_INF = -1e30
VMEM_LIMIT = 56 * 1024 * 1024

TM = 512
TM_COMBINE = 256
DISPATCH_ROWS = 2048
DMA_UNROLL = 8
SWIGLU_SPLITS = 2

PLAIN, ROPE, ROPE_SCALED = 0, 1, 2
LOG2_E = 1.4426950408889634
LN_2 = 0.6931471805599453
Q_SCALE = HEAD_DIM ** -0.5 * LOG2_E


def _params(*sem):
    return pltpu.CompilerParams(dimension_semantics=sem, vmem_limit_bytes=VMEM_LIMIT)


def _rms(x, g):
    ms = jnp.mean(x * x, axis=-1, keepdims=True)
    return x * lax.rsqrt(ms + EPS) * g


def _pair_perm(pairs):
    idx = []
    for ha, hb in pairs:
        for base in (0, HALF_DIM):
            idx += [ha * HEAD_DIM + base + d for d in range(HALF_DIM)]
            idx += [hb * HEAD_DIM + base + d for d in range(HALF_DIM)]
    return np.asarray(idx, np.int32)


def _rope_tables(seq):
    pos = jnp.arange(seq, dtype=F32)
    inv = 1.0 / (ROPE_THETA ** (jnp.arange(HALF_DIM, dtype=F32) / HALF_DIM))
    ang = pos[:, None] * inv[None, :]
    cos, sin = jnp.cos(ang), jnp.sin(ang)
    return jnp.tile(cos, (1, 4)), jnp.concatenate([-sin, -sin, sin, sin], axis=1)


def _rope_coeffs(cos, sin):
    a = jnp.stack([cos * Q_SCALE, cos, jnp.ones_like(cos)])
    b = jnp.stack([sin * Q_SCALE, sin, jnp.zeros_like(sin)])
    return a, b


def _store_chunks(acc, cos, sin, o_ref, kinds):
    for c, kind in enumerate(kinds):
        seg = acc[:, c * LANES:(c + 1) * LANES]
        if kind != PLAIN:
            seg = seg * cos + pltpu.roll(seg, LANES // 2, 1) * sin
            if kind == ROPE_SCALED:
                seg = seg * Q_SCALE
        o_ref[:, c * LANES:(c + 1) * LANES] = seg.astype(o_ref.dtype)


def _to_token_tiles(val, ref):
    rows = val.shape[0]
    for c in range(val.shape[1] // LANES):
        ref[pl.ds(c, rows, stride=SUBLANES), :] = val[:, c * LANES:(c + 1) * LANES]


def _from_token_tiles(ref, rows):
    return jnp.concatenate([ref[pl.ds(c, rows, stride=SUBLANES), :] for c in range(SUBLANES)], axis=1)


def _tile_at(ref, line):
    return ref.at[pl.ds(pl.multiple_of(line, SUBLANES), SUBLANES)]


def _group_proj_kernel(x_ref, g_ref, w_ref, a_ref, b_ref, perm_ref, o_ref, *, dil):
    h = _rms(x_ref[...], g_ref[...]).astype(BF16)
    if dil > 1:
        h = jnp.dot(perm_ref[...], h, preferred_element_type=F32).astype(BF16)
    sub = h.shape[0] // dil
    tn = w_ref.shape[1] // 3
    for kind in range(3):
        acc = jnp.dot(h, w_ref[:, kind * tn:(kind + 1) * tn], preferred_element_type=F32)
        a, b = a_ref[kind], b_ref[kind]
        for c in range(tn // LANES):
            seg = acc[:, c * LANES:(c + 1) * LANES]
            seg = (seg * a + pltpu.roll(seg, LANES // 2, 1) * b).astype(o_ref.dtype)
            cols = slice(kind * tn + c * LANES, kind * tn + (c + 1) * LANES)
            if dil == 1:
                o_ref[:, cols] = seg
            else:
                for r in range(dil):
                    o_ref[r, :, cols] = seg[r * sub:(r + 1) * sub]


def _residue_major_rows(dil):
    q = np.arange(TM)
    sub = TM // dil
    return (q % sub) * dil + q // sub


def _group_proj(x, g, w, coef_a, coef_b, batch, seq, dil):
    t, d = x.shape
    n_out = w.shape[1]
    tiles_per_seq = seq // TM
    if dil == 1:
        out_spec = pl.BlockSpec((TM, n_out), lambda i: (i, 0))
        out_shape = jax.ShapeDtypeStruct((t, n_out), BF16)
    else:
        out_spec = pl.BlockSpec((None, dil, TM // dil, n_out),
                                lambda i: (i // tiles_per_seq, 0, i % tiles_per_seq, 0))
        out_shape = jax.ShapeDtypeStruct((batch, dil, seq // dil, n_out), BF16)
    nat = _residue_major_rows(dil)
    perm = jnp.asarray(nat[:, None] == np.arange(TM)[None, :], BF16)
    rows = (np.arange(seq) // TM) * TM + nat[np.arange(seq) % TM]
    coef_spec = pl.BlockSpec((3, TM, LANES), lambda i: (0, i % tiles_per_seq, 0))
    return pl.pallas_call(
        functools.partial(_group_proj_kernel, dil=dil),
        grid=(t // TM,),
        in_specs=[
            pl.BlockSpec((TM, d), lambda i: (i, 0)),
            pl.BlockSpec((1, d), lambda i: (0, 0)),
            pl.BlockSpec(w.shape, lambda i: (0, 0)),
            coef_spec, coef_spec,
            pl.BlockSpec((TM, TM), lambda i: (0, 0)),
        ],
        out_specs=out_spec,
        out_shape=out_shape,
        compiler_params=_params("parallel"),
        name=f"group_proj_d{dil}",
    )(x, g, w, coef_a[:, rows], coef_b[:, rows], perm)


def _qkv1_kernel(x_ref, gq_ref, gkv_ref, wq_ref, wkv_ref, cos_ref, sin_ref, q_ref, kv_ref):
    x = x_ref[...]
    xn = x * lax.rsqrt(jnp.mean(x * x, axis=-1, keepdims=True) + EPS)
    hq = (xn * gq_ref[...]).astype(BF16)
    hkv = (xn * gkv_ref[...]).astype(BF16)
    cos, sin = cos_ref[...], sin_ref[...]
    accq = jnp.dot(hq, wq_ref[...], preferred_element_type=F32)
    _store_chunks(accq, cos, sin, q_ref, (ROPE_SCALED,) * (accq.shape[1] // LANES))
    acckv = jnp.dot(hkv, wkv_ref[...], preferred_element_type=F32)
    _store_chunks(acckv, cos, sin, kv_ref, (ROPE, PLAIN))


def _qkv1(x, gq, gkv, wq, wkv, cos, sin):
    t, d = x.shape
    s_tiles = cos.shape[0] // TM
    row = lambda i: (i, 0)
    fixed = lambda i: (0, 0)
    return pl.pallas_call(
        _qkv1_kernel,
        grid=(t // TM,),
        in_specs=[
            pl.BlockSpec((TM, d), row),
            pl.BlockSpec((1, d), fixed),
            pl.BlockSpec((1, d), fixed),
            pl.BlockSpec(wq.shape, fixed),
            pl.BlockSpec(wkv.shape, fixed),
            pl.BlockSpec((TM, LANES), lambda i: (i % s_tiles, 0)),
            pl.BlockSpec((TM, LANES), lambda i: (i % s_tiles, 0)),
        ],
        out_specs=[pl.BlockSpec((TM, wq.shape[1]), row), pl.BlockSpec((TM, wkv.shape[1]), row)],
        out_shape=[jax.ShapeDtypeStruct((t, wq.shape[1]), BF16),
                   jax.ShapeDtypeStruct((t, wkv.shape[1]), BF16)],
        compiler_params=_params("parallel"),
        name="qkv_layer1",
    )(x, gq, gkv, wq, wkv, cos, sin)


def _band_mask(has_prev, max_dist):
    qi = lax.broadcasted_iota(jnp.int32, (BLOCK, 2 * BLOCK), 0) + BLOCK
    kj = lax.broadcasted_iota(jnp.int32, (BLOCK, 2 * BLOCK), 1)
    rel = qi - kj
    mask = (rel >= 0) & (rel <= max_dist)
    if has_prev is not True:
        mask = mask & ((kj >= BLOCK) | has_prev)
    return jnp.concatenate([mask, mask], axis=0)


def _band_blocks(qb, prev_ref, cur_ref, lanes):
    own = cur_ref[qb * BLOCK:(qb + 1) * BLOCK, lanes]
    before = prev_ref[:, lanes] if qb == 0 else cur_ref[(qb - 1) * BLOCK:qb * BLOCK, lanes]
    return jnp.concatenate([before, own], axis=0)


def _lane_masks():
    lane = lax.broadcasted_iota(jnp.int32, (BLOCK, LANES), 1)
    first_qk = (lane % HEAD_DIM) < HALF_DIM
    sel_a = jnp.where(first_qk, 1.0, 0.0).astype(BF16)
    sel_b = jnp.where(first_qk, 0.0, 1.0).astype(BF16)
    return lane, sel_a, sel_b, lane < HEAD_DIM


def _pair_scores(q2, kcat, sel_a, sel_b):
    qs = jnp.concatenate([q2 * sel_a, q2 * sel_b], axis=0)
    return lax.dot_general(qs, kcat, (((1,), (1,)), ((), ())), preferred_element_type=F32)


def _dil_attn_kernel(q_ref, kp_ref, kc_ref, vp_ref, vc_ref, o_ref, st_ref, *, steps):
    lane, sel_a, sel_b, first_v = _lane_masks()
    for qb in range(Q_BLOCKS):
        rows = slice(qb * BLOCK, (qb + 1) * BLOCK)
        mask2 = _band_mask(True if qb else pl.program_id(2) > 0, steps)
        stats = jnp.zeros((BLOCK, LANES), F32)
        for p in range(q_ref.shape[1] // LANES):
            sl = slice(p * LANES, (p + 1) * LANES)
            kcat = _band_blocks(qb, kp_ref, kc_ref, sl)
            vcat = _band_blocks(qb, vp_ref, vc_ref, sl)
            s = jnp.where(mask2, _pair_scores(q_ref[rows, sl], kcat, sel_a, sel_b), NEG_INF)
            m = jnp.max(s, axis=-1, keepdims=True)
            pe = jnp.exp2(s - m)
            den = jnp.sum(pe, axis=-1, keepdims=True)
            pv = jnp.dot(pe.astype(BF16), vcat, preferred_element_type=F32)
            pv = pv * (1.0 / den)
            o_ref[rows, sl] = jnp.where(first_v, pv[:BLOCK], pv[BLOCK:]).astype(o_ref.dtype)
            lse = m * LN_2 + jnp.log(den)
            stats = jnp.where(lane == 2 * p, lse[:BLOCK], stats)
            stats = jnp.where(lane == 2 * p + 1, lse[BLOCK:], stats)
        st_ref[rows, :] = stats


def _dilated_group(qkv, batch, seq, g):
    win, dil = A_CONFIGS[g]
    n = seq // dil
    d = qkv.shape[-1] // 3
    view = qkv.reshape(batch, dil, n, 3 * d)

    rows = Q_BLOCKS * BLOCK

    def spec(kind, prev):
        if prev:
            return pl.BlockSpec((None, None, BLOCK, d),
                                lambda b, r, blk: (b, r, jnp.maximum(blk * Q_BLOCKS - 1, 0), kind))
        return pl.BlockSpec((None, None, rows, d), lambda b, r, blk: (b, r, blk, kind))

    return pl.pallas_call(
        functools.partial(_dil_attn_kernel, steps=win // dil),
        grid=(batch, dil, n // rows),
        in_specs=[spec(0, False), spec(1, True), spec(1, False), spec(2, True), spec(2, False)],
        out_specs=[pl.BlockSpec((None, None, rows, d), lambda b, r, blk: (b, r, blk, 0)),
                   pl.BlockSpec((None, None, rows, LANES), lambda b, r, blk: (b, r, blk, 0))],
        out_shape=[jax.ShapeDtypeStruct((batch, dil, n, d), BF16),
                   jax.ShapeDtypeStruct((batch, dil, n, LANES), F32)],
        compiler_params=_params("parallel", "parallel", "arbitrary"),
        name=f"dilated_attn_g{g}",
    )(view, view, view, view, view)


def _swa_kernel(fill_ref, q_ref, kp_ref, kc_ref, vp_ref, vc_ref, o_ref):
    _, sel_a, sel_b, first_v = _lane_masks()
    all_lanes = slice(0, LANES)
    key_row = lax.broadcasted_iota(jnp.int32, (2 * BLOCK, LANES), 0)
    for qb in range(Q_BLOCKS):
        rows = slice(qb * BLOCK, (qb + 1) * BLOCK)
        mask2 = _band_mask(True if qb else pl.program_id(1) > 0, B_WINDOW - 1)
        kcat = _band_blocks(qb, kp_ref, kc_ref, all_lanes)
        vcat = _band_blocks(qb, vp_ref, vc_ref, all_lanes)
        vcat = jnp.where(key_row == 0, jnp.zeros_like(vcat), vcat)
        for j in range(q_ref.shape[1] // LANES):
            sl = slice(j * LANES, (j + 1) * LANES)
            s = jnp.where(mask2, _pair_scores(q_ref[rows, sl], kcat, sel_a, sel_b), fill_ref[j])
            m = jnp.max(s, axis=-1, keepdims=True)
            pe = jnp.exp2(s - m)
            den = jnp.sum(pe, axis=-1, keepdims=True)
            pv = jnp.dot(pe.astype(BF16), vcat, preferred_element_type=F32)
            pv = pv * (1.0 / den)
            o_ref[rows, sl] = jnp.where(first_v, pv[:BLOCK], pv[BLOCK:]).astype(o_ref.dtype)


def _swa_attention(q, kv, sinks, batch, seq):
    d = q.shape[1]
    rows = Q_BLOCKS * BLOCK
    qv = q.reshape(batch, seq, d)
    kvv = kv.reshape(batch, seq, 2 * LANES)

    def kv_spec(col, prev):
        if prev:
            return pl.BlockSpec((None, BLOCK, LANES),
                                lambda b, blk: (b, jnp.maximum(blk * Q_BLOCKS - 1, 0), col))
        return pl.BlockSpec((None, rows, LANES), lambda b, blk: (b, blk, col))

    half = sinks.shape[0] // 2
    sink_rows = jnp.repeat(jnp.stack([sinks[:half], sinks[half:]], axis=1), BLOCK, axis=1)
    first_key = jnp.arange(2 * BLOCK)[None, None, :] == 0
    fill = jnp.where(first_key, sink_rows[:, :, None].astype(F32) * LOG2_E, NEG_INF)

    o = pl.pallas_call(
        _swa_kernel,
        grid=(batch, seq // rows),
        in_specs=[pl.BlockSpec(fill.shape, lambda b, blk: (0, 0, 0)),
                  pl.BlockSpec((None, rows, d), lambda b, blk: (b, blk, 0)),
                  kv_spec(0, True), kv_spec(0, False), kv_spec(1, True), kv_spec(1, False)],
        out_specs=pl.BlockSpec((None, rows, d), lambda b, blk: (b, blk, 0)),
        out_shape=jax.ShapeDtypeStruct((batch, seq, d), BF16),
        compiler_params=_params("parallel", "arbitrary"),
        name="swa_sink_attn",
    )(fill, qv, kvv, kvv, kvv, kvv)
    return o.reshape(batch * seq, d)


def _interleave(src_ref, dst_ref):
    dil, sub = src_ref.shape[0], src_ref.shape[1]
    chunks = dst_ref.shape[0]
    for r in range(dil):
        for c in range(chunks):
            part = src_ref[r, :, c * LANES:(c + 1) * LANES].astype(F32)
            dst_ref[c, pl.ds(r, sub, stride=dil), :] = part
    return jnp.concatenate([dst_ref[c] for c in range(chunks)], axis=1)


def _merge_wo_kernel(x_ref, o0_ref, o1_ref, o2_ref, s0_ref, s1_ref, s2_ref, e_ref, w_ref, out_ref,
                     ob_ref, sb1_ref, sb2_ref):
    s0, s1, s2 = s0_ref[0], _interleave(s1_ref, sb1_ref), _interleave(s2_ref, sb2_ref)
    top = jnp.maximum(jnp.maximum(s0, s1), s2)
    e0, e1, e2 = jnp.exp(s0 - top), jnp.exp(s1 - top), jnp.exp(s2 - top)
    inv = 1.0 / (e0 + e1 + e2)
    expand = e_ref[...]

    def weight(e):
        return jnp.dot((e * inv).astype(BF16), expand, preferred_element_type=F32)

    merged = weight(e0) * o0_ref[0].astype(F32)
    merged = merged + weight(e1) * _interleave(o1_ref, ob_ref)
    merged = merged + weight(e2) * _interleave(o2_ref, ob_ref)
    out_ref[...] = x_ref[...] + jnp.dot(merged.astype(BF16), w_ref[...], preferred_element_type=F32)


def _merge_wo(x, outs, stats, expand, w, batch, seq):
    t, d = x.shape
    tiles_per_seq = seq // TM
    row = lambda i: (i, 0)
    fixed = lambda i: (0, 0)

    def sub(arr):
        dil = arr.shape[1]
        return pl.BlockSpec((None, dil, TM // dil, arr.shape[3]),
                            lambda i: (i // tiles_per_seq, 0, i % tiles_per_seq, 0))

    return pl.pallas_call(
        _merge_wo_kernel,
        grid=(t // TM,),
        in_specs=[pl.BlockSpec((TM, d), row)]
        + [sub(o) for o in outs] + [sub(s) for s in stats]
        + [pl.BlockSpec(expand.shape, fixed), pl.BlockSpec(w.shape, fixed)],
        out_specs=pl.BlockSpec((TM, d), row),
        out_shape=jax.ShapeDtypeStruct((t, d), F32),
        scratch_shapes=[pltpu.VMEM((d // LANES, TM, LANES), F32), pltpu.VMEM((1, TM, LANES), F32),
                        pltpu.VMEM((1, TM, LANES), F32)],
        compiler_params=_params("parallel"),
        name="merge_wo",
    )(x, *outs, *stats, expand, w)


def _wo_router_kernel(x_ref, o_ref, w_ref, g_ref, wr_ref, x1_ref, x1t_ref, ids_ref, gates_ref):
    x1 = x_ref[...] + jnp.dot(o_ref[...], w_ref[...], preferred_element_type=F32)
    x1_ref[...] = x1
    _to_token_tiles(x1, x1t_ref)
    h = _rms(x1, g_ref[...]).astype(BF16)
    logits = jnp.dot(h, wr_ref[...], preferred_element_type=F32)
    lane = lax.broadcasted_iota(jnp.int32, logits.shape, 1)
    logits = jnp.where(lane < N_EXPERTS, logits, -jnp.inf)
    v1 = jnp.max(logits, axis=-1, keepdims=True)
    i1 = jnp.min(jnp.where(logits == v1, lane, LANES), axis=-1, keepdims=True)
    rest = jnp.where(lane == i1, -jnp.inf, logits)
    v2 = jnp.max(rest, axis=-1, keepdims=True)
    i2 = jnp.min(jnp.where(rest == v2, lane, LANES), axis=-1, keepdims=True)
    e2 = jnp.exp(v2 - v1)
    inv = 1.0 / (1.0 + e2)
    ids = jnp.where(lane == 0, i1.astype(F32), 0.0)
    ids_ref[...] = jnp.where(lane == 1, i2.astype(F32), ids)
    gates_ref[:, :LANES] = jnp.broadcast_to(inv, logits.shape)
    gates_ref[:, LANES:] = jnp.broadcast_to(e2 * inv, logits.shape)


def _wo_router(x, o, w, g, wr):
    t, d = x.shape
    row = lambda i: (i, 0)
    fixed = lambda i: (0, 0)
    return pl.pallas_call(
        _wo_router_kernel,
        grid=(t // TM,),
        in_specs=[pl.BlockSpec((TM, d), row), pl.BlockSpec((TM, o.shape[1]), row),
                  pl.BlockSpec(w.shape, fixed), pl.BlockSpec((1, d), fixed),
                  pl.BlockSpec(wr.shape, fixed)],
        out_specs=[pl.BlockSpec((TM, d), row), pl.BlockSpec((TM * SUBLANES, LANES), row),
                   pl.BlockSpec((TM, LANES), row), pl.BlockSpec((TM, 2 * LANES), row)],
        out_shape=[jax.ShapeDtypeStruct((t, d), F32),
                   jax.ShapeDtypeStruct((t * SUBLANES, LANES), F32),
                   jax.ShapeDtypeStruct((t, LANES), F32),
                   jax.ShapeDtypeStruct((t, 2 * LANES), F32)],
        compiler_params=_params("parallel"),
        name="wo_router",
    )(x, o, w, g, wr)


def _swiglu(h, w1_ref, w3_ref, w2_ref):
    width = w1_ref.shape[1] // SWIGLU_SPLITS
    y = None
    for c in range(SWIGLU_SPLITS):
        sl = slice(c * width, (c + 1) * width)
        a = jnp.dot(h, w1_ref[:, sl], preferred_element_type=F32)
        b = jnp.dot(h, w3_ref[:, sl], preferred_element_type=F32)
        act = (a * jax.nn.sigmoid(a) * b).astype(BF16)
        part = jnp.dot(act, w2_ref[sl, :], preferred_element_type=F32)
        y = part if y is None else y + part
    return y


def _ffn_kernel(x_ref, g_ref, w1_ref, w3_ref, w2_ref, o_ref):
    x = x_ref[...]
    o_ref[...] = x + _swiglu(_rms(x, g_ref[...]).astype(BF16), w1_ref, w3_ref, w2_ref)


def _ffn(x, g, w1, w3, w2):
    t, d = x.shape
    fixed = lambda i: (0, 0)
    resident = pl.Buffered(1)
    return pl.pallas_call(
        _ffn_kernel,
        grid=(t // TM,),
        in_specs=[pl.BlockSpec((TM, d), lambda i: (i, 0)),
                  pl.BlockSpec((1, d), fixed),
                  pl.BlockSpec(w1.shape, fixed, pipeline_mode=resident),
                  pl.BlockSpec(w3.shape, fixed, pipeline_mode=resident),
                  pl.BlockSpec(w2.shape, fixed, pipeline_mode=resident)],
        out_specs=pl.BlockSpec((TM, d), lambda i: (i, 0)),
        out_shape=jax.ShapeDtypeStruct((t, d), F32),
        compiler_params=_params("parallel"),
        name="dense_swiglu",
    )(x, g, w1, w3, w2)


def _moe_kernel(te_ref, nu_ref, xs_ref, g_ref, w1_ref, w3_ref, w2_ref, o_ref):
    used = pl.program_id(0) < nu_ref[0]

    @pl.when(used)
    def _():
        x = _from_token_tiles(xs_ref, xs_ref.shape[0] // SUBLANES)
        _to_token_tiles(_swiglu(_rms(x, g_ref[...]).astype(BF16), w1_ref, w3_ref, w2_ref), o_ref)

    @pl.when(jnp.logical_not(used))
    def _():
        o_ref[...] = jnp.zeros_like(o_ref)


def _moe_experts(xs, g, w1, w3, w2, tile_expert, n_used, n_tiles):
    d, f = w1.shape[1], w1.shape[2]
    lines = TM * SUBLANES
    resident = pl.Buffered(1)
    grid_spec = pltpu.PrefetchScalarGridSpec(
        num_scalar_prefetch=2,
        grid=(n_tiles,),
        in_specs=[pl.BlockSpec((lines, LANES), lambda i, te, nu: (i, 0)),
                  pl.BlockSpec((1, d), lambda i, te, nu: (0, 0)),
                  pl.BlockSpec((None, d, f), lambda i, te, nu: (te[i], 0, 0), pipeline_mode=resident),
                  pl.BlockSpec((None, d, f), lambda i, te, nu: (te[i], 0, 0), pipeline_mode=resident),
                  pl.BlockSpec((None, f, d), lambda i, te, nu: (te[i], 0, 0), pipeline_mode=resident)],
        out_specs=pl.BlockSpec((lines, LANES), lambda i, te, nu: (i, 0)),
    )
    return pl.pallas_call(
        _moe_kernel,
        grid_spec=grid_spec,
        out_shape=jax.ShapeDtypeStruct((n_tiles * lines, LANES), F32),
        compiler_params=_params("arbitrary"),
        name="moe_experts",
    )(tile_expert, n_used, xs, g, w1, w3, w2)


def _dispatch_kernel(dst_ref, pad_ref, src_ref, zero_ref, out_hbm, sem):
    rows = dst_ref.shape[2] // 2
    n_pad = pad_ref.shape[0]

    def issue(r, carry):
        base = r * DMA_UNROLL
        idx = [dst_ref[0, 0, 2 * base + q] for q in range(2 * DMA_UNROLL)]
        for u in range(DMA_UNROLL):
            src = _tile_at(src_ref, (base + u) * SUBLANES)
            for k in range(2):
                pltpu.make_async_copy(src, _tile_at(out_hbm, idx[2 * u + k]), sem).start(priority=k)
        return carry

    lax.fori_loop(0, rows // DMA_UNROLL, issue, 0)
    lines = 2 * rows * SUBLANES
    pltpu.make_async_copy(out_hbm.at[pl.ds(0, lines)], out_hbm.at[pl.ds(0, lines)], sem).wait()

    @pl.when(pl.program_id(0) == 0)
    def _():
        def fill(r, carry):
            base = r * DMA_UNROLL
            idx = [pad_ref[base + q] for q in range(DMA_UNROLL)]
            for q in range(DMA_UNROLL):
                pltpu.make_async_copy(zero_ref, _tile_at(out_hbm, idx[q]), sem).start()
            return carry

        lax.fori_loop(0, n_pad // DMA_UNROLL, fill, 0)
        pad_lines = n_pad * SUBLANES
        pltpu.make_async_copy(out_hbm.at[pl.ds(0, pad_lines)], out_hbm.at[pl.ds(0, pad_lines)],
                              sem).wait()


def _dispatch(x_tiles, dest_line, pad_line, n_slots):
    t = x_tiles.shape[0] // SUBLANES
    steps = t // DISPATCH_ROWS
    zero = jnp.zeros((SUBLANES, LANES), x_tiles.dtype)
    return pl.pallas_call(
        _dispatch_kernel,
        grid=(steps,),
        in_specs=[pl.BlockSpec((1, 1, 2 * DISPATCH_ROWS), lambda i: (i, 0, 0), memory_space=pltpu.SMEM),
                  pl.BlockSpec(memory_space=pltpu.SMEM),
                  pl.BlockSpec((DISPATCH_ROWS * SUBLANES, LANES), lambda i: (i, 0)),
                  pl.BlockSpec((SUBLANES, LANES), lambda i: (0, 0))],
        out_specs=pl.BlockSpec(memory_space=pl.ANY),
        out_shape=jax.ShapeDtypeStruct((n_slots * SUBLANES, LANES), x_tiles.dtype),
        scratch_shapes=[pltpu.SemaphoreType.DMA(())],
        compiler_params=_params("arbitrary"),
        name="moe_dispatch",
    )(dest_line.reshape(steps, 1, 2 * DISPATCH_ROWS), pad_line, x_tiles, zero)


def _ple_rows(x, p, g_ref, wg_ref, wp_ref):
    h = _rms(x, g_ref[...]).astype(BF16)
    gate = jax.nn.sigmoid(jnp.dot(h, wg_ref[...], preferred_element_type=F32))
    proj = jnp.dot(p.astype(BF16), wp_ref[...], preferred_element_type=F32)
    return x + proj * gate


def _combine_ple_kernel(cur_ref, nxt_ref, x_ref, gates_ref, p_ref, g_ref, wg_ref, wp_ref, fg_ref,
                        ys_hbm, o_ref, a_ref, b_ref, sem):
    i = pl.program_id(0)
    tm = a_ref.shape[1] // SUBLANES

    def issue(idx_ref, offset, slot):
        def body(r, carry):
            base = r * DMA_UNROLL
            idx = [idx_ref[0, 0, offset + 2 * base + q] for q in range(2 * DMA_UNROLL)]
            for u in range(DMA_UNROLL):
                line = (base + u) * SUBLANES
                pltpu.make_async_copy(_tile_at(ys_hbm, idx[2 * u]), _tile_at(a_ref.at[slot], line),
                                      sem.at[slot]).start(priority=0)
                pltpu.make_async_copy(_tile_at(ys_hbm, idx[2 * u + 1]), _tile_at(b_ref.at[slot], line),
                                      sem.at[slot]).start(priority=1)
            return carry

        lax.fori_loop(0, tm // DMA_UNROLL, body, 0)

    def wait(slot):
        whole = ys_hbm.at[pl.ds(0, tm * SUBLANES)]
        pltpu.make_async_copy(whole, a_ref.at[slot], sem.at[slot]).wait()
        pltpu.make_async_copy(whole, b_ref.at[slot], sem.at[slot]).wait()

    def compute(slot):
        rows = slice(slot * tm, (slot + 1) * tm)
        g1, g2 = gates_ref[rows, :LANES], gates_ref[rows, LANES:]
        y = jnp.concatenate(
            [g1 * a_ref[slot, pl.ds(c, tm, stride=SUBLANES), :]
             + g2 * b_ref[slot, pl.ds(c, tm, stride=SUBLANES), :] for c in range(SUBLANES)], axis=1)
        x = _ple_rows(x_ref[rows, :] + y, p_ref[rows, :], g_ref, wg_ref, wp_ref)
        o_ref[rows, :] = _rms(x, fg_ref[...])

    @pl.when(i == 0)
    def _():
        issue(cur_ref, 0, 0)

    issue(cur_ref, 2 * tm, 1)
    wait(0)
    compute(0)

    @pl.when(i + 1 < pl.num_programs(0))
    def _():
        issue(nxt_ref, 0, 0)

    wait(1)
    compute(1)


def _combine_ple(x, gates, ys, dest_line, p, layer, g, wg, wp, fg):
    t, d = x.shape
    tm = TM_COMBINE
    steps = t // (2 * tm)
    row = lambda i: (i, 0)
    fixed = lambda i: (0, 0)
    idx = dest_line.reshape(steps, 1, 4 * tm)
    idx_spec = lambda index: pl.BlockSpec((1, 1, 4 * tm), index, memory_space=pltpu.SMEM)
    return pl.pallas_call(
        _combine_ple_kernel,
        grid=(steps,),
        in_specs=[idx_spec(lambda i: (i, 0, 0)),
                  idx_spec(lambda i: (jnp.minimum(i + 1, steps - 1), 0, 0)),
                  pl.BlockSpec((2 * tm, d), row),
                  pl.BlockSpec((2 * tm, 2 * LANES), row),
                  pl.BlockSpec((None, 2 * tm, p.shape[2]), lambda i: (layer, i, 0)),
                  pl.BlockSpec((1, d), fixed),
                  pl.BlockSpec(wg.shape, fixed),
                  pl.BlockSpec(wp.shape, fixed),
                  pl.BlockSpec((1, d), fixed),
                  pl.BlockSpec(memory_space=pl.ANY)],
        out_specs=pl.BlockSpec((2 * tm, d), row),
        out_shape=jax.ShapeDtypeStruct((t, d), F32),
        scratch_shapes=[pltpu.VMEM((2, tm * SUBLANES, LANES), F32),
                        pltpu.VMEM((2, tm * SUBLANES, LANES), F32),
                        pltpu.SemaphoreType.DMA((2,))],
        compiler_params=_params("arbitrary"),
        name="combine_ple_final",
    )(idx, idx, x, gates, p, g, wg, wp, fg, ys)


def _ple_kernel(x_ref, p_ref, g_ref, wg_ref, wp_ref, o_ref):
    o_ref[...] = _ple_rows(x_ref[...], p_ref[...], g_ref, wg_ref, wp_ref)


def _ple(x, p, layer, g, wg, wp):
    t, d = x.shape
    fixed = lambda i: (0, 0)
    return pl.pallas_call(
        _ple_kernel,
        grid=(t // TM,),
        in_specs=[pl.BlockSpec((TM, d), lambda i: (i, 0)),
                  pl.BlockSpec((None, TM, p.shape[2]), lambda i: (layer, i, 0)),
                  pl.BlockSpec((1, d), fixed),
                  pl.BlockSpec(wg.shape, fixed),
                  pl.BlockSpec(wp.shape, fixed)],
        out_specs=pl.BlockSpec((TM, d), lambda i: (i, 0)),
        out_shape=jax.ShapeDtypeStruct((t, d), F32),
        compiler_params=_params("parallel"),
        name="ple",
    )(x, p, g, wg, wp)


def _routing_tables(ids_f):
    t = ids_f.shape[0]
    flat = ids_f[:, :2].astype(jnp.int32).reshape(-1)
    experts = jnp.arange(N_EXPERTS)
    onehot = (flat[:, None] == experts[None, :]).astype(jnp.int32)
    csum = jnp.cumsum(onehot, axis=0)
    rank = jnp.take_along_axis(csum, flat[:, None], axis=1)[:, 0] - 1
    counts = csum[-1]
    tiles_per = (counts + TM - 1) // TM
    tile_end = jnp.cumsum(tiles_per)
    tile_start = tile_end - tiles_per
    dest = tile_start[flat] * TM + rank
    n_tiles = (2 * t) // TM + N_EXPERTS
    n_used = tile_end[-1]
    tile_ids = jnp.minimum(jnp.arange(n_tiles), n_used - 1)
    tile_expert = jnp.minimum(jnp.sum((tile_ids[:, None] >= tile_end[None, :]).astype(jnp.int32), axis=1),
                              N_EXPERTS - 1)
    pad_sizes = jnp.concatenate([tiles_per * TM - counts, ((n_tiles - n_used) * TM).reshape(1)])
    pad_end = jnp.cumsum(pad_sizes)
    seg_first = jnp.concatenate([tile_start * TM + counts, (n_used * TM).reshape(1)])
    k = jnp.arange(N_EXPERTS * TM)
    seg = jnp.sum((k[:, None] >= pad_end[None, :]).astype(jnp.int32), axis=1)
    pad = seg_first[seg] + k - (pad_end - pad_sizes)[seg]
    to_line = lambda v: (v * SUBLANES).astype(jnp.int32)
    return (to_line(dest), to_line(pad), tile_expert.astype(jnp.int32),
            n_used.reshape(1).astype(jnp.int32), n_tiles)


def kernel(x, p, attn_norm, ffn_norm, a_w_qkv, a_w_o, kv_norm, kv_w, b_w_q, b_sinks, b_w_o,
           dense_w1, dense_w3, dense_w2, moe_router, moe_w1, moe_w3, moe_w2,
           ple_norm, ple_w_gate, ple_w_proj, final_norm):
    batch, seq, d = x.shape
    t = batch * seq
    n_heads = d // HEAD_DIM
    n_groups = len(A_CONFIGS)
    assert d == SUBLANES * LANES and seq % TM == 0 and t % DISPATCH_ROWS == 0
    xr = x.reshape(t, d)
    pr = p.reshape(p.shape[0], t, p.shape[3])
    cos, sin = _rope_tables(seq)
    row = lambda v: v.reshape(1, -1)

    perm_a = _pair_perm([(2 * j, 2 * j + 1) for j in range(n_heads // 2)])
    coef_a, coef_b = _rope_coeffs(cos, sin)
    w = a_w_qkv[0].reshape(d, n_groups, 3, d)
    outs, stats = [], []
    for g, (_, dil) in enumerate(A_CONFIGS):
        w_g = jnp.concatenate([w[:, g, 0][:, perm_a], w[:, g, 1][:, perm_a], w[:, g, 2]],
                              axis=1).astype(BF16)
        qkv = _group_proj(xr, row(attn_norm[0]), w_g, coef_a, coef_b, batch, seq, dil)
        o_g, st_g = _dilated_group(qkv, batch, seq, g)
        outs.append(o_g)
        stats.append(st_g)
    head_of_col = jnp.arange(d) // HEAD_DIM
    expand = (jnp.arange(LANES)[:, None] == head_of_col[None, :]).astype(BF16)
    xr = _merge_wo(xr, outs, stats, expand, a_w_o[0].astype(BF16), batch, seq)
    xr = _ffn(xr, row(ffn_norm[0]), dense_w1[0].astype(BF16), dense_w3[0].astype(BF16),
              dense_w2[0].astype(BF16))
    xr = _ple(xr, pr, 0, row(ple_norm[0]), ple_w_gate[0].astype(BF16), ple_w_proj[0].astype(BF16))

    half = n_heads // 2
    perm_q = _pair_perm([(j, half + j) for j in range(half)])
    perm_k = _pair_perm([(0, 1)])
    kv_cols = B_KV_HEADS * HEAD_DIM
    w_kv = jnp.concatenate([kv_w[:, :kv_cols][:, perm_k], kv_w[:, kv_cols:]], axis=1).astype(BF16)
    q1, kv1 = _qkv1(xr, row(attn_norm[1]), row(kv_norm), b_w_q[0][:, perm_q].astype(BF16), w_kv,
                    cos, sin)
    o1 = _swa_attention(q1, kv1, b_sinks[0], batch, seq)
    o_rows = np.concatenate([np.concatenate([np.arange(j * HEAD_DIM, (j + 1) * HEAD_DIM),
                                             np.arange((half + j) * HEAD_DIM, (half + j + 1) * HEAD_DIM)])
                             for j in range(half)])
    wr = jnp.zeros((d, LANES), F32).at[:, :N_EXPERTS].set(moe_router[0]).astype(BF16)
    xr, x_tiles, ids, gates = _wo_router(xr, o1, b_w_o[0][o_rows].astype(BF16), row(ffn_norm[1]), wr)

    dest_line, pad_line, tile_expert, n_used, n_tiles = _routing_tables(ids)
    xs = _dispatch(x_tiles, dest_line, pad_line, n_tiles * TM)
    ys = _moe_experts(xs, row(ffn_norm[1]), moe_w1[0].astype(BF16), moe_w3[0].astype(BF16),
                      moe_w2[0].astype(BF16), tile_expert, n_used, n_tiles)
    xr = _combine_ple(xr, gates, ys, dest_line, pr, 1, row(ple_norm[1]), ple_w_gate[1].astype(BF16),
                      ple_w_proj[1].astype(BF16), row(final_norm))
    return xr.reshape(batch, seq, d)
```

```python
import functools

import numpy as np
import jax
import jax.numpy as jnp
from jax import lax
from jax.experimental import pallas as pl
from jax.experimental.pallas import tpu as pltpu

F32 = jnp.float32
BF16 = jnp.bfloat16

LANES = 128
SUBLANES = 8
MXU_COLS = 256
HEAD_DIM = 64
HALF_DIM = HEAD_DIM // 2
BLOCK = 128
Q_BLOCKS = 2
A_CONFIGS = ((128, 1), (512, 4), (2048, 16))
B_KV_HEADS = 2
B_WINDOW = 128
N_EXPERTS = 8
ROPE_THETA = 10000.0
EPS = 1e-6
NEG_INF = -1e30
VMEM_LIMIT = 56 * 1024 * 1024

TM = 512
TM_COMBINE = 256
DISPATCH_ROWS = 2048
DMA_UNROLL = 8
SWIGLU_SPLITS = 2

PLAIN, ROPE, ROPE_SCALED = 0, 1, 2
LOG2_E = 1.4426950408889634
LN_2 = 0.6931471805599453
Q_SCALE = HEAD_DIM ** -0.5 * LOG2_E


def _params(*sem):
    return pltpu.CompilerParams(dimension_semantics=sem, vmem_limit_bytes=VMEM_LIMIT)


def _rms(x, g):
    ms = jnp.mean(x * x, axis=-1, keepdims=True)
    return x * lax.rsqrt(ms + EPS) * g


def _pair_perm(pairs):
    idx = []
    for ha, hb in pairs:
        for base in (0, HALF_DIM):
            idx += [ha * HEAD_DIM + base + d for d in range(HALF_DIM)]
            idx += [hb * HEAD_DIM + base + d for d in range(HALF_DIM)]
    return np.asarray(idx, np.int32)


def _rope_tables(seq):
    pos = jnp.arange(seq, dtype=F32)
    inv = 1.0 / (ROPE_THETA ** (jnp.arange(HALF_DIM, dtype=F32) / HALF_DIM))
    ang = pos[:, None] * inv[None, :]
    cos, sin = jnp.cos(ang), jnp.sin(ang)
    return jnp.tile(cos, (1, 4)), jnp.concatenate([-sin, -sin, sin, sin], axis=1)


def _rope_coeffs(cos, sin):
    a = jnp.stack([cos * Q_SCALE, cos, jnp.ones_like(cos)])
    b = jnp.stack([sin * Q_SCALE, sin, jnp.zeros_like(sin)])
    return a, b


def _store_chunks(acc, cos, sin, o_ref, kinds):
    for c, kind in enumerate(kinds):
        seg = acc[:, c * LANES:(c + 1) * LANES]
        if kind != PLAIN:
            seg = seg * cos + pltpu.roll(seg, LANES // 2, 1) * sin
            if kind == ROPE_SCALED:
                seg = seg * Q_SCALE
        o_ref[:, c * LANES:(c + 1) * LANES] = seg.astype(o_ref.dtype)


def _to_token_tiles(val, ref):
    rows = val.shape[0]
    for c in range(val.shape[1] // LANES):
        ref[pl.ds(c, rows, stride=SUBLANES), :] = val[:, c * LANES:(c + 1) * LANES]


def _from_token_tiles(ref, rows):
    return jnp.concatenate([ref[pl.ds(c, rows, stride=SUBLANES), :] for c in range(SUBLANES)], axis=1)


def _tile_at(ref, line):
    return ref.at[pl.ds(pl.multiple_of(line, SUBLANES), SUBLANES)]


def _group_proj_kernel(x_ref, g_ref, w_ref, a_ref, b_ref, perm_ref, cast_ref, o_ref, cast_out_ref, *, dil):
    cast_out_ref[...] = cast_ref[...].astype(cast_out_ref.dtype)
    h = _rms(x_ref[...], g_ref[...]).astype(BF16)
    if dil > 1:
        h = jnp.dot(perm_ref[...], h, preferred_element_type=F32).astype(BF16)
    sub = h.shape[0] // dil
    tn = w_ref.shape[1] // 3
    for kind in range(3):
        acc = jnp.dot(h, w_ref[:, kind * tn:(kind + 1) * tn], preferred_element_type=F32)
        a, b = a_ref[kind], b_ref[kind]
        for c in range(tn // LANES):
            seg = acc[:, c * LANES:(c + 1) * LANES]
            seg = (seg * a + pltpu.roll(seg, LANES // 2, 1) * b).astype(o_ref.dtype)
            cols = slice(kind * tn + c * LANES, kind * tn + (c + 1) * LANES)
            if dil == 1:
                o_ref[:, cols] = seg
            else:
                for r in range(dil):
                    o_ref[r, :, cols] = seg[r * sub:(r + 1) * sub]


def _residue_major_rows(dil):
    q = np.arange(TM)
    sub = TM // dil
    return (q % sub) * dil + q // sub


def _group_proj(x, g, w, coef_a, coef_b, batch, seq, dil, cast_src):
    t, d = x.shape
    n_out = w.shape[1]
    tiles_per_seq = seq // TM
    steps = t // TM
    cast2d = cast_src.reshape(-1, cast_src.shape[-1])
    slab = cast2d.shape[0] // steps
    assert cast2d.shape[0] % steps == 0 and slab % (2 * SUBLANES) == 0
    cast_spec = pl.BlockSpec((slab, cast2d.shape[1]), lambda i: (i, 0))
    if dil == 1:
        out_spec = pl.BlockSpec((TM, n_out), lambda i: (i, 0))
        out_shape = jax.ShapeDtypeStruct((t, n_out), BF16)
    else:
        out_spec = pl.BlockSpec((None, dil, TM // dil, n_out),
                                lambda i: (i // tiles_per_seq, 0, i % tiles_per_seq, 0))
        out_shape = jax.ShapeDtypeStruct((batch, dil, seq // dil, n_out), BF16)
    nat = _residue_major_rows(dil)
    perm = jnp.asarray(nat[:, None] == np.arange(TM)[None, :], BF16)
    rows = (np.arange(seq) // TM) * TM + nat[np.arange(seq) % TM]
    coef_spec = pl.BlockSpec((3, TM, LANES), lambda i: (0, i % tiles_per_seq, 0))
    qkv, cast = pl.pallas_call(
        functools.partial(_group_proj_kernel, dil=dil),
        grid=(steps,),
        in_specs=[
            pl.BlockSpec((TM, d), lambda i: (i, 0)),
            pl.BlockSpec((1, d), lambda i: (0, 0)),
            pl.BlockSpec(w.shape, lambda i: (0, 0), pipeline_mode=pl.Buffered(1)),
            coef_spec, coef_spec,
            pl.BlockSpec((TM, TM), lambda i: (0, 0), pipeline_mode=pl.Buffered(1)),
            cast_spec,
        ],
        out_specs=[out_spec, cast_spec],
        out_shape=[out_shape, jax.ShapeDtypeStruct(cast2d.shape, BF16)],
        compiler_params=_params("parallel"),
        name=f"group_proj_d{dil}",
    )(x, g, w, coef_a[:, rows], coef_b[:, rows], perm, cast2d)
    return qkv, cast.reshape(cast_src.shape)


def _qkv1_kernel(x_ref, gq_ref, gkv_ref, wq_ref, wkv_ref, cos_ref, sin_ref, q_ref, kv_ref):
    x = x_ref[...]
    xn = x * lax.rsqrt(jnp.mean(x * x, axis=-1, keepdims=True) + EPS)
    hq = (xn * gq_ref[...]).astype(BF16)
    hkv = (xn * gkv_ref[...]).astype(BF16)
    cos, sin = cos_ref[...], sin_ref[...]
    accq = jnp.dot(hq, wq_ref[...], preferred_element_type=F32)
    _store_chunks(accq, cos, sin, q_ref, (ROPE_SCALED,) * (accq.shape[1] // LANES))
    acckv = jnp.dot(hkv, wkv_ref[...], preferred_element_type=F32)
    _store_chunks(acckv, cos, sin, kv_ref, (ROPE, PLAIN))


def _qkv1(x, gq, gkv, wq, wkv, cos, sin):
    t, d = x.shape
    s_tiles = cos.shape[0] // TM
    row = lambda i: (i, 0)
    fixed = lambda i: (0, 0)
    return pl.pallas_call(
        _qkv1_kernel,
        grid=(t // TM,),
        in_specs=[
            pl.BlockSpec((TM, d), row),
            pl.BlockSpec((1, d), fixed),
            pl.BlockSpec((1, d), fixed),
            pl.BlockSpec(wq.shape, fixed),
            pl.BlockSpec(wkv.shape, fixed),
            pl.BlockSpec((TM, LANES), lambda i: (i % s_tiles, 0)),
            pl.BlockSpec((TM, LANES), lambda i: (i % s_tiles, 0)),
        ],
        out_specs=[pl.BlockSpec((TM, wq.shape[1]), row), pl.BlockSpec((TM, wkv.shape[1]), row)],
        out_shape=[jax.ShapeDtypeStruct((t, wq.shape[1]), BF16),
                   jax.ShapeDtypeStruct((t, wkv.shape[1]), BF16)],
        compiler_params=_params("parallel"),
        name="qkv_layer1",
    )(x, gq, gkv, wq, wkv, cos, sin)


def _band_mask(has_prev, max_dist):
    qi = lax.broadcasted_iota(jnp.int32, (BLOCK, 2 * BLOCK), 0) + BLOCK
    kj = lax.broadcasted_iota(jnp.int32, (BLOCK, 2 * BLOCK), 1)
    rel = qi - kj
    mask = (rel >= 0) & (rel <= max_dist)
    if has_prev is not True:
        mask = mask & ((kj >= BLOCK) | has_prev)
    return jnp.concatenate([mask, mask], axis=0)


def _band_blocks(qb, prev_ref, cur_ref, lanes):
    own = cur_ref[qb * BLOCK:(qb + 1) * BLOCK, lanes]
    before = prev_ref[:, lanes] if qb == 0 else cur_ref[(qb - 1) * BLOCK:qb * BLOCK, lanes]
    return jnp.concatenate([before, own], axis=0)


def _lane_masks():
    lane = lax.broadcasted_iota(jnp.int32, (BLOCK, LANES), 1)
    first_qk = (lane % HEAD_DIM) < HALF_DIM
    sel_a = jnp.where(first_qk, 1.0, 0.0).astype(BF16)
    sel_b = jnp.where(first_qk, 0.0, 1.0).astype(BF16)
    return lane, sel_a, sel_b, lane < HEAD_DIM


def _pair_scores(q2, kcat, sel_a, sel_b):
    qs = jnp.concatenate([q2 * sel_a, q2 * sel_b], axis=0)
    return lax.dot_general(qs, kcat, (((1,), (1,)), ((), ())), preferred_element_type=F32)


def _dil_attn_kernel(q_ref, kp_ref, kc_ref, vp_ref, vc_ref, o_ref, st_ref, *, steps):
    lane, sel_a, sel_b, first_v = _lane_masks()
    for qb in range(Q_BLOCKS):
        rows = slice(qb * BLOCK, (qb + 1) * BLOCK)
        mask2 = _band_mask(True if qb else pl.program_id(2) > 0, steps)
        stats = jnp.zeros((BLOCK, LANES), F32)
        for p in range(q_ref.shape[1] // LANES):
            sl = slice(p * LANES, (p + 1) * LANES)
            kcat = _band_blocks(qb, kp_ref, kc_ref, sl)
            vcat = _band_blocks(qb, vp_ref, vc_ref, sl)
            s = jnp.where(mask2, _pair_scores(q_ref[rows, sl], kcat, sel_a, sel_b), NEG_INF)
            m = jnp.max(s, axis=-1, keepdims=True)
            pe = jnp.exp2(s - m)
            den = jnp.sum(pe, axis=-1, keepdims=True)
            pv = jnp.dot(pe.astype(BF16), vcat, preferred_element_type=F32)
            pv = pv * (1.0 / den)
            o_ref[rows, sl] = jnp.where(first_v, pv[:BLOCK], pv[BLOCK:]).astype(o_ref.dtype)
            lse = m * LN_2 + jnp.log(den)
            stats = jnp.where(lane == 2 * p, lse[:BLOCK], stats)
            stats = jnp.where(lane == 2 * p + 1, lse[BLOCK:], stats)
        st_ref[rows, :] = stats


def _dilated_group(qkv, batch, seq, g):
    win, dil = A_CONFIGS[g]
    n = seq // dil
    d = qkv.shape[-1] // 3
    view = qkv.reshape(batch, dil, n, 3 * d)

    rows = Q_BLOCKS * BLOCK

    def spec(kind, prev):
        if prev:
            return pl.BlockSpec((None, None, BLOCK, d),
                                lambda b, r, blk: (b, r, jnp.maximum(blk * Q_BLOCKS - 1, 0), kind))
        return pl.BlockSpec((None, None, rows, d), lambda b, r, blk: (b, r, blk, kind))

    return pl.pallas_call(
        functools.partial(_dil_attn_kernel, steps=win // dil),
        grid=(batch, dil, n // rows),
        in_specs=[spec(0, False), spec(1, True), spec(1, False), spec(2, True), spec(2, False)],
        out_specs=[pl.BlockSpec((None, None, rows, d), lambda b, r, blk: (b, r, blk, 0)),
                   pl.BlockSpec((None, None, rows, LANES), lambda b, r, blk: (b, r, blk, 0))],
        out_shape=[jax.ShapeDtypeStruct((batch, dil, n, d), BF16),
                   jax.ShapeDtypeStruct((batch, dil, n, LANES), F32)],
        compiler_params=_params("parallel", "parallel", "arbitrary"),
        name=f"dilated_attn_g{g}",
    )(view, view, view, view, view)


def _swa_kernel(fill_ref, q_ref, kp_ref, kc_ref, vp_ref, vc_ref, o_ref):
    _, sel_a, sel_b, first_v = _lane_masks()
    all_lanes = slice(0, LANES)
    key_row = lax.broadcasted_iota(jnp.int32, (2 * BLOCK, LANES), 0)
    for qb in range(Q_BLOCKS):
        rows = slice(qb * BLOCK, (qb + 1) * BLOCK)
        mask2 = _band_mask(True if qb else pl.program_id(1) > 0, B_WINDOW - 1)
        kcat = _band_blocks(qb, kp_ref, kc_ref, all_lanes)
        vcat = _band_blocks(qb, vp_ref, vc_ref, all_lanes)
        vcat = jnp.where(key_row == 0, jnp.zeros_like(vcat), vcat)
        for j in range(q_ref.shape[1] // LANES):
            sl = slice(j * LANES, (j + 1) * LANES)
            s = jnp.where(mask2, _pair_scores(q_ref[rows, sl], kcat, sel_a, sel_b), fill_ref[j])
            m = jnp.max(s, axis=-1, keepdims=True)
            pe = jnp.exp2(s - m)
            den = jnp.sum(pe, axis=-1, keepdims=True)
            pv = jnp.dot(pe.astype(BF16), vcat, preferred_element_type=F32)
            pv = pv * (1.0 / den)
            o_ref[rows, sl] = jnp.where(first_v, pv[:BLOCK], pv[BLOCK:]).astype(o_ref.dtype)


def _swa_attention(q, kv, sinks, batch, seq):
    d = q.shape[1]
    rows = Q_BLOCKS * BLOCK
    qv = q.reshape(batch, seq, d)
    kvv = kv.reshape(batch, seq, 2 * LANES)

    def kv_spec(col, prev):
        if prev:
            return pl.BlockSpec((None, BLOCK, LANES),
                                lambda b, blk: (b, jnp.maximum(blk * Q_BLOCKS - 1, 0), col))
        return pl.BlockSpec((None, rows, LANES), lambda b, blk: (b, blk, col))

    half = sinks.shape[0] // 2
    sink_rows = jnp.repeat(jnp.stack([sinks[:half], sinks[half:]], axis=1), BLOCK, axis=1)
    first_key = jnp.arange(2 * BLOCK)[None, None, :] == 0
    fill = jnp.where(first_key, sink_rows[:, :, None].astype(F32) * LOG2_E, NEG_INF)

    o = pl.pallas_call(
        _swa_kernel,
        grid=(batch, seq // rows),
        in_specs=[pl.BlockSpec(fill.shape, lambda b, blk: (0, 0, 0)),
                  pl.BlockSpec((None, rows, d), lambda b, blk: (b, blk, 0)),
                  kv_spec(0, True), kv_spec(0, False), kv_spec(1, True), kv_spec(1, False)],
        out_specs=pl.BlockSpec((None, rows, d), lambda b, blk: (b, blk, 0)),
        out_shape=jax.ShapeDtypeStruct((batch, seq, d), BF16),
        compiler_params=_params("parallel", "arbitrary"),
        name="swa_sink_attn",
    )(fill, qv, kvv, kvv, kvv, kvv)
    return o.reshape(batch * seq, d)


def _interleave(src_ref, dst_ref):
    dil, sub = src_ref.shape[0], src_ref.shape[1]
    chunks = dst_ref.shape[0]
    for r in range(dil):
        for c in range(chunks):
            part = src_ref[r, :, c * LANES:(c + 1) * LANES].astype(F32)
            dst_ref[c, pl.ds(r, sub, stride=dil), :] = part
    return jnp.concatenate([dst_ref[c] for c in range(chunks)], axis=1)


def _merge_wo_kernel(x_ref, o0_ref, o1_ref, o2_ref, s0_ref, s1_ref, s2_ref, e_ref, w_ref, out_ref,
                     ob_ref, sb1_ref, sb2_ref):
    s0, s1, s2 = s0_ref[0], _interleave(s1_ref, sb1_ref), _interleave(s2_ref, sb2_ref)
    top = jnp.maximum(jnp.maximum(s0, s1), s2)
    e0, e1, e2 = jnp.exp(s0 - top), jnp.exp(s1 - top), jnp.exp(s2 - top)
    inv = 1.0 / (e0 + e1 + e2)
    expand = e_ref[...]

    def weight(e):
        return jnp.dot((e * inv).astype(BF16), expand, preferred_element_type=F32)

    merged = weight(e0) * o0_ref[0].astype(F32)
    merged = merged + weight(e1) * _interleave(o1_ref, ob_ref)
    merged = merged + weight(e2) * _interleave(o2_ref, ob_ref)
    out_ref[...] = x_ref[...] + jnp.dot(merged.astype(BF16), w_ref[...], preferred_element_type=F32)


def _merge_wo(x, outs, stats, expand, w, batch, seq):
    t, d = x.shape
    tiles_per_seq = seq // TM
    row = lambda i: (i, 0)
    fixed = lambda i: (0, 0)

    def sub(arr):
        dil = arr.shape[1]
        return pl.BlockSpec((None, dil, TM // dil, arr.shape[3]),
                            lambda i: (i // tiles_per_seq, 0, i % tiles_per_seq, 0))

    return pl.pallas_call(
        _merge_wo_kernel,
        grid=(t // TM,),
        in_specs=[pl.BlockSpec((TM, d), row)]
        + [sub(o) for o in outs] + [sub(s) for s in stats]
        + [pl.BlockSpec(expand.shape, fixed), pl.BlockSpec(w.shape, fixed)],
        out_specs=pl.BlockSpec((TM, d), row),
        out_shape=jax.ShapeDtypeStruct((t, d), F32),
        scratch_shapes=[pltpu.VMEM((d // LANES, TM, LANES), F32), pltpu.VMEM((1, TM, LANES), F32),
                        pltpu.VMEM((1, TM, LANES), F32)],
        compiler_params=_params("parallel"),
        name="merge_wo",
    )(x, *outs, *stats, expand, w)


def _wo_router_kernel(x_ref, o_ref, w_ref, g_ref, wr_ref, x1_ref, x1t_ref, ids_ref, gates_ref):
    x1 = x_ref[...] + jnp.dot(o_ref[...], w_ref[...], preferred_element_type=F32)
    x1_ref[...] = x1
    _to_token_tiles(x1, x1t_ref)
    h = _rms(x1, g_ref[...]).astype(BF16)
    logits = jnp.dot(h, wr_ref[...], preferred_element_type=F32)
    lane = lax.broadcasted_iota(jnp.int32, logits.shape, 1)
    logits = jnp.where(lane < N_EXPERTS, logits, -jnp.inf)
    v1 = jnp.max(logits, axis=-1, keepdims=True)
    i1 = jnp.min(jnp.where(logits == v1, lane, LANES), axis=-1, keepdims=True)
    rest = jnp.where(lane == i1, -jnp.inf, logits)
    v2 = jnp.max(rest, axis=-1, keepdims=True)
    i2 = jnp.min(jnp.where(rest == v2, lane, LANES), axis=-1, keepdims=True)
    e2 = jnp.exp(v2 - v1)
    inv = 1.0 / (1.0 + e2)
    ids = jnp.where(lane == 0, i1.astype(F32), 0.0)
    ids_ref[...] = jnp.where(lane == 1, i2.astype(F32), ids)
    gates_ref[:, :LANES] = jnp.broadcast_to(inv, logits.shape)
    gates_ref[:, LANES:] = jnp.broadcast_to(e2 * inv, logits.shape)


def _wo_router(x, o, w, g, wr):
    t, d = x.shape
    row = lambda i: (i, 0)
    fixed = lambda i: (0, 0)
    return pl.pallas_call(
        _wo_router_kernel,
        grid=(t // TM,),
        in_specs=[pl.BlockSpec((TM, d), row), pl.BlockSpec((TM, o.shape[1]), row),
                  pl.BlockSpec(w.shape, fixed), pl.BlockSpec((1, d), fixed),
                  pl.BlockSpec(wr.shape, fixed)],
        out_specs=[pl.BlockSpec((TM, d), row), pl.BlockSpec((TM * SUBLANES, LANES), row),
                   pl.BlockSpec((TM, LANES), row), pl.BlockSpec((TM, 2 * LANES), row)],
        out_shape=[jax.ShapeDtypeStruct((t, d), F32),
                   jax.ShapeDtypeStruct((t * SUBLANES, LANES), F32),
                   jax.ShapeDtypeStruct((t, LANES), F32),
                   jax.ShapeDtypeStruct((t, 2 * LANES), F32)],
        compiler_params=_params("parallel"),
        name="wo_router",
    )(x, o, w, g, wr)


def _swiglu(h, w1_ref, w3_ref, w2_ref):
    width = w1_ref.shape[1] // SWIGLU_SPLITS
    y = None
    for c in range(SWIGLU_SPLITS):
        sl = slice(c * width, (c + 1) * width)
        a = jnp.dot(h, w1_ref[:, sl], preferred_element_type=F32)
        b = jnp.dot(h, w3_ref[:, sl], preferred_element_type=F32)
        act = (a * jax.nn.sigmoid(a) * b).astype(BF16)
        part = jnp.dot(act, w2_ref[sl, :], preferred_element_type=F32)
        y = part if y is None else y + part
    return y


def _ffn_kernel(x_ref, g_ref, w1_ref, w3_ref, w2_ref, o_ref):
    x = x_ref[...]
    o_ref[...] = x + _swiglu(_rms(x, g_ref[...]).astype(BF16), w1_ref, w3_ref, w2_ref)


def _ffn(x, g, w1, w3, w2):
    t, d = x.shape
    fixed = lambda i: (0, 0)
    resident = pl.Buffered(1)
    return pl.pallas_call(
        _ffn_kernel,
        grid=(t // TM,),
        in_specs=[pl.BlockSpec((TM, d), lambda i: (i, 0)),
                  pl.BlockSpec((1, d), fixed),
                  pl.BlockSpec(w1.shape, fixed, pipeline_mode=resident),
                  pl.BlockSpec(w3.shape, fixed, pipeline_mode=resident),
                  pl.BlockSpec(w2.shape, fixed, pipeline_mode=resident)],
        out_specs=pl.BlockSpec((TM, d), lambda i: (i, 0)),
        out_shape=jax.ShapeDtypeStruct((t, d), F32),
        compiler_params=_params("parallel"),
        name="dense_swiglu",
    )(x, g, w1, w3, w2)


def _moe_kernel(te_ref, nu_ref, xs_ref, g_ref, w1_ref, w3_ref, w2_ref, o_ref):
    used = pl.program_id(0) < nu_ref[0]

    @pl.when(used)
    def _():
        x = _from_token_tiles(xs_ref, xs_ref.shape[0] // SUBLANES)
        _to_token_tiles(_swiglu(_rms(x, g_ref[...]).astype(BF16), w1_ref, w3_ref, w2_ref), o_ref)

    @pl.when(jnp.logical_not(used))
    def _():
        o_ref[...] = jnp.zeros_like(o_ref)


def _moe_experts(xs, g, w1, w3, w2, tile_expert, n_used, n_tiles):
    d, f = w1.shape[1], w1.shape[2]
    lines = TM * SUBLANES
    resident = pl.Buffered(1)
    grid_spec = pltpu.PrefetchScalarGridSpec(
        num_scalar_prefetch=2,
        grid=(n_tiles,),
        in_specs=[pl.BlockSpec((lines, LANES), lambda i, te, nu: (i, 0)),
                  pl.BlockSpec((1, d), lambda i, te, nu: (0, 0)),
                  pl.BlockSpec((None, d, f), lambda i, te, nu: (te[i], 0, 0), pipeline_mode=resident),
                  pl.BlockSpec((None, d, f), lambda i, te, nu: (te[i], 0, 0), pipeline_mode=resident),
                  pl.BlockSpec((None, f, d), lambda i, te, nu: (te[i], 0, 0), pipeline_mode=resident)],
        out_specs=pl.BlockSpec((lines, LANES), lambda i, te, nu: (i, 0)),
    )
    return pl.pallas_call(
        _moe_kernel,
        grid_spec=grid_spec,
        out_shape=jax.ShapeDtypeStruct((n_tiles * lines, LANES), F32),
        compiler_params=_params("arbitrary"),
        name="moe_experts",
    )(tile_expert, n_used, xs, g, w1, w3, w2)


def _dispatch_kernel(dst_ref, pad_ref, src_ref, zero_ref, out_hbm, sem):
    rows = dst_ref.shape[2] // 2
    n_pad = pad_ref.shape[0]

    def issue(r, carry):
        base = r * DMA_UNROLL
        idx = [dst_ref[0, 0, 2 * base + q] for q in range(2 * DMA_UNROLL)]
        for u in range(DMA_UNROLL):
            src = _tile_at(src_ref, (base + u) * SUBLANES)
            for k in range(2):
                pltpu.make_async_copy(src, _tile_at(out_hbm, idx[2 * u + k]), sem).start(priority=k)
        return carry

    lax.fori_loop(0, rows // DMA_UNROLL, issue, 0)
    lines = 2 * rows * SUBLANES
    pltpu.make_async_copy(out_hbm.at[pl.ds(0, lines)], out_hbm.at[pl.ds(0, lines)], sem).wait()

    @pl.when(pl.program_id(0) == 0)
    def _():
        def fill(r, carry):
            base = r * DMA_UNROLL
            idx = [pad_ref[base + q] for q in range(DMA_UNROLL)]
            for q in range(DMA_UNROLL):
                pltpu.make_async_copy(zero_ref, _tile_at(out_hbm, idx[q]), sem).start()
            return carry

        lax.fori_loop(0, n_pad // DMA_UNROLL, fill, 0)
        pad_lines = n_pad * SUBLANES
        pltpu.make_async_copy(out_hbm.at[pl.ds(0, pad_lines)], out_hbm.at[pl.ds(0, pad_lines)],
                              sem).wait()


def _dispatch(x_tiles, dest_line, pad_line, n_slots):
    t = x_tiles.shape[0] // SUBLANES
    steps = t // DISPATCH_ROWS
    zero = jnp.zeros((SUBLANES, LANES), x_tiles.dtype)
    return pl.pallas_call(
        _dispatch_kernel,
        grid=(steps,),
        in_specs=[pl.BlockSpec((1, 1, 2 * DISPATCH_ROWS), lambda i: (i, 0, 0), memory_space=pltpu.SMEM),
                  pl.BlockSpec(memory_space=pltpu.SMEM),
                  pl.BlockSpec((DISPATCH_ROWS * SUBLANES, LANES), lambda i: (i, 0)),
                  pl.BlockSpec((SUBLANES, LANES), lambda i: (0, 0))],
        out_specs=pl.BlockSpec(memory_space=pl.ANY),
        out_shape=jax.ShapeDtypeStruct((n_slots * SUBLANES, LANES), x_tiles.dtype),
        scratch_shapes=[pltpu.SemaphoreType.DMA(())],
        compiler_params=_params("arbitrary"),
        name="moe_dispatch",
    )(dest_line.reshape(steps, 1, 2 * DISPATCH_ROWS), pad_line, x_tiles, zero)


def _ple_rows(x, p, g_ref, wg_ref, wp_ref, between=None):
    h = _rms(x, g_ref[...]).astype(BF16)
    gate = []
    for c in range(wg_ref.shape[1] // MXU_COLS):
        if between is not None:
            between(c)
        logits = jnp.dot(h, wg_ref[:, c * MXU_COLS:(c + 1) * MXU_COLS], preferred_element_type=F32)
        gate.append(jax.nn.sigmoid(logits))
    proj = jnp.dot(p.astype(BF16), wp_ref[...], preferred_element_type=F32)
    return x + proj * jnp.concatenate(gate, axis=1)


def _combine_ple_kernel(cur_ref, nxt_ref, x_ref, gates_ref, p_ref, g_ref, wg_ref, wp_ref, fg_ref,
                        ys_hbm, o_ref, a_ref, b_ref, sem):
    i = pl.program_id(0)
    last = pl.num_programs(0) - 1
    tm = a_ref.shape[1] // SUBLANES
    groups = tm // DMA_UNROLL

    def issue_group(idx_ref, offset, slot, base):
        idx = [idx_ref[0, 0, offset + 2 * base + q] for q in range(2 * DMA_UNROLL)]
        for u in range(DMA_UNROLL):
            line = (base + u) * SUBLANES
            pltpu.make_async_copy(_tile_at(ys_hbm, idx[2 * u]), _tile_at(a_ref.at[slot], line),
                                  sem.at[slot]).start(priority=0)
            pltpu.make_async_copy(_tile_at(ys_hbm, idx[2 * u + 1]), _tile_at(b_ref.at[slot], line),
                                  sem.at[slot]).start(priority=1)

    def wait(slot):
        whole = ys_hbm.at[pl.ds(0, tm * SUBLANES)]
        pltpu.make_async_copy(whole, a_ref.at[slot], sem.at[slot]).wait()
        pltpu.make_async_copy(whole, b_ref.at[slot], sem.at[slot]).wait()

    def compute(slot, idx_ref, offset):
        rows = slice(slot * tm, (slot + 1) * tm)
        g1, g2 = gates_ref[rows, :LANES], gates_ref[rows, LANES:]
        y = jnp.concatenate(
            [g1 * a_ref[slot, pl.ds(c, tm, stride=SUBLANES), :]
             + g2 * b_ref[slot, pl.ds(c, tm, stride=SUBLANES), :] for c in range(SUBLANES)], axis=1)
        chunks = wg_ref.shape[1] // MXU_COLS

        def between(c):
            for grp in range(c * groups // chunks, (c + 1) * groups // chunks):
                issue_group(idx_ref, offset, 1 - slot, grp * DMA_UNROLL)

        x = _ple_rows(x_ref[rows, :] + y, p_ref[rows, :], g_ref, wg_ref, wp_ref, between)
        o_ref[rows, :] = _rms(x, fg_ref[...])

    @pl.when(i == 0)
    def _():
        lax.fori_loop(0, groups, lambda r, c: (issue_group(cur_ref, 0, 0, r * DMA_UNROLL), c)[1], 0)

    wait(0)
    compute(0, cur_ref, 2 * tm)
    wait(1)
    compute(1, nxt_ref, 0)

    @pl.when(i == last)
    def _():
        wait(0)


def _combine_ple(x, gates, ys, dest_line, p, layer, g, wg, wp, fg):
    t, d = x.shape
    tm = TM_COMBINE
    steps = t // (2 * tm)
    row = lambda i: (i, 0)
    fixed = lambda i: (0, 0)
    idx = dest_line.reshape(steps, 1, 4 * tm)
    idx_spec = lambda index: pl.BlockSpec((1, 1, 4 * tm), index, memory_space=pltpu.SMEM)
    return pl.pallas_call(
        _combine_ple_kernel,
        grid=(steps,),
        in_specs=[idx_spec(lambda i: (i, 0, 0)),
                  idx_spec(lambda i: (jnp.minimum(i + 1, steps - 1), 0, 0)),
                  pl.BlockSpec((2 * tm, d), row),
                  pl.BlockSpec((2 * tm, 2 * LANES), row),
                  pl.BlockSpec((None, 2 * tm, p.shape[2]), lambda i: (layer, i, 0)),
                  pl.BlockSpec((1, d), fixed),
                  pl.BlockSpec(wg.shape, fixed),
                  pl.BlockSpec(wp.shape, fixed),
                  pl.BlockSpec((1, d), fixed),
                  pl.BlockSpec(memory_space=pl.ANY)],
        out_specs=pl.BlockSpec((2 * tm, d), row),
        out_shape=jax.ShapeDtypeStruct((t, d), F32),
        scratch_shapes=[pltpu.VMEM((2, tm * SUBLANES, LANES), F32),
                        pltpu.VMEM((2, tm * SUBLANES, LANES), F32),
                        pltpu.SemaphoreType.DMA((2,))],
        compiler_params=_params("arbitrary"),
        name="combine_ple_final",
    )(idx, idx, x, gates, p, g, wg, wp, fg, ys)


def _ple_kernel(x_ref, p_ref, g_ref, wg_ref, wp_ref, o_ref):
    o_ref[...] = _ple_rows(x_ref[...], p_ref[...], g_ref, wg_ref, wp_ref)


def _ple(x, p, layer, g, wg, wp):
    t, d = x.shape
    fixed = lambda i: (0, 0)
    return pl.pallas_call(
        _ple_kernel,
        grid=(t // TM,),
        in_specs=[pl.BlockSpec((TM, d), lambda i: (i, 0)),
                  pl.BlockSpec((None, TM, p.shape[2]), lambda i: (layer, i, 0)),
                  pl.BlockSpec((1, d), fixed),
                  pl.BlockSpec(wg.shape, fixed),
                  pl.BlockSpec(wp.shape, fixed)],
        out_specs=pl.BlockSpec((TM, d), lambda i: (i, 0)),
        out_shape=jax.ShapeDtypeStruct((t, d), F32),
        compiler_params=_params("parallel"),
        name="ple",
    )(x, p, g, wg, wp)


def _routing_tables(ids_f):
    t = ids_f.shape[0]
    flat = ids_f[:, :2].astype(jnp.int32).reshape(-1)
    experts = jnp.arange(N_EXPERTS)
    onehot = (flat[:, None] == experts[None, :]).astype(jnp.int32)
    csum = jnp.cumsum(onehot, axis=0)
    rank = jnp.take_along_axis(csum, flat[:, None], axis=1)[:, 0] - 1
    counts = csum[-1]
    tiles_per = (counts + TM - 1) // TM
    tile_end = jnp.cumsum(tiles_per)
    tile_start = tile_end - tiles_per
    dest = tile_start[flat] * TM + rank
    n_tiles = (2 * t) // TM + N_EXPERTS
    n_used = tile_end[-1]
    tile_ids = jnp.minimum(jnp.arange(n_tiles), n_used - 1)
    tile_expert = jnp.minimum(jnp.sum((tile_ids[:, None] >= tile_end[None, :]).astype(jnp.int32), axis=1),
                              N_EXPERTS - 1)
    pad_sizes = jnp.concatenate([tiles_per * TM - counts, ((n_tiles - n_used) * TM).reshape(1)])
    pad_end = jnp.cumsum(pad_sizes)
    seg_first = jnp.concatenate([tile_start * TM + counts, (n_used * TM).reshape(1)])
    k = jnp.arange(N_EXPERTS * TM)
    seg = jnp.sum((k[:, None] >= pad_end[None, :]).astype(jnp.int32), axis=1)
    pad = seg_first[seg] + k - (pad_end - pad_sizes)[seg]
    to_line = lambda v: (v * SUBLANES).astype(jnp.int32)
    return (to_line(dest), to_line(pad), tile_expert.astype(jnp.int32),
            n_used.reshape(1).astype(jnp.int32), n_tiles)


def kernel(x, p, attn_norm, ffn_norm, a_w_qkv, a_w_o, kv_norm, kv_w, b_w_q, b_sinks, b_w_o,
           dense_w1, dense_w3, dense_w2, moe_router, moe_w1, moe_w3, moe_w2,
           ple_norm, ple_w_gate, ple_w_proj, final_norm):
    batch, seq, d = x.shape
    t = batch * seq
    n_heads = d // HEAD_DIM
    n_groups = len(A_CONFIGS)
    assert d == SUBLANES * LANES and seq % TM == 0 and t % DISPATCH_ROWS == 0
    xr = x.reshape(t, d)
    pr = p.reshape(p.shape[0], t, p.shape[3])
    cos, sin = _rope_tables(seq)
    row = lambda v: v.reshape(1, -1)

    perm_a = _pair_perm([(2 * j, 2 * j + 1) for j in range(n_heads // 2)])
    coef_a, coef_b = _rope_coeffs(cos, sin)
    w = a_w_qkv[0].reshape(d, n_groups, 3, d)
    outs, stats, moe_bf16 = [], [], []
    for g, (_, dil) in enumerate(A_CONFIGS):
        w_g = jnp.concatenate([w[:, g, 0][:, perm_a], w[:, g, 1][:, perm_a], w[:, g, 2]],
                              axis=1).astype(BF16)
        qkv, cast = _group_proj(xr, row(attn_norm[0]), w_g, coef_a, coef_b, batch, seq, dil,
                                (moe_w1, moe_w3, moe_w2)[g][0])
        moe_bf16.append(cast)
        o_g, st_g = _dilated_group(qkv, batch, seq, g)
        outs.append(o_g)
        stats.append(st_g)
    head_of_col = jnp.arange(d) // HEAD_DIM
    expand = (jnp.arange(LANES)[:, None] == head_of_col[None, :]).astype(BF16)
    xr = _merge_wo(xr, outs, stats, expand, a_w_o[0].astype(BF16), batch, seq)
    xr = _ffn(xr, row(ffn_norm[0]), dense_w1[0].astype(BF16), dense_w3[0].astype(BF16),
              dense_w2[0].astype(BF16))
    xr = _ple(xr, pr, 0, row(ple_norm[0]), ple_w_gate[0].astype(BF16), ple_w_proj[0].astype(BF16))

    half = n_heads // 2
    perm_q = _pair_perm([(j, half + j) for j in range(half)])
    perm_k = _pair_perm([(0, 1)])
    kv_cols = B_KV_HEADS * HEAD_DIM
    w_kv = jnp.concatenate([kv_w[:, :kv_cols][:, perm_k], kv_w[:, kv_cols:]], axis=1).astype(BF16)
    q1, kv1 = _qkv1(xr, row(attn_norm[1]), row(kv_norm), b_w_q[0][:, perm_q].astype(BF16), w_kv,
                    cos, sin)
    o1 = _swa_attention(q1, kv1, b_sinks[0], batch, seq)
    o_rows = np.concatenate([np.concatenate([np.arange(j * HEAD_DIM, (j + 1) * HEAD_DIM),
                                             np.arange((half + j) * HEAD_DIM, (half + j + 1) * HEAD_DIM)])
                             for j in range(half)])
    wr = jnp.zeros((d, LANES), F32).at[:, :N_EXPERTS].set(moe_router[0]).astype(BF16)
    xr, x_tiles, ids, gates = _wo_router(xr, o1, b_w_o[0][o_rows].astype(BF16), row(ffn_norm[1]), wr)

    dest_line, pad_line, tile_expert, n_used, n_tiles = _routing_tables(ids)
    xs = _dispatch(x_tiles, dest_line, pad_line, n_tiles * TM)
    ys = _moe_experts(xs, row(ffn_norm[1]), *moe_bf16, tile_expert, n_used, n_tiles)
    xr = _combine_ple(xr, gates, ys, dest_line, pr, 1, row(ple_norm[1]), ple_w_gate[1].astype(BF16),
                      ple_w_proj[1].astype(BF16), row(final_norm))
    return xr.reshape(batch, seq, d)
```

```python
import functools

import numpy as np
import jax
import jax.numpy as jnp
from jax import lax
from jax.experimental import pallas as pl
from jax.experimental.pallas import tpu as pltpu

F32 = jnp.float32
BF16 = jnp.bfloat16

LANES = 128
SUBLANES = 8
MXU_COLS = 256
HEAD_DIM = 64
HALF_DIM = HEAD_DIM // 2
BLOCK = 128
Q_BLOCKS = 2
A_CONFIGS = ((128, 1), (512, 4), (2048, 16))
B_KV_HEADS = 2
B_WINDOW = 128
N_EXPERTS = 8
ROPE_THETA = 10000.0
EPS = 1e-6
NEG_INF = -1e30
VMEM_LIMIT = 56 * 1024 * 1024

TM = 512
TM_COMBINE = 256
DISPATCH_ROWS = 2048
DMA_UNROLL = 8
SWIGLU_SPLITS = 2

PLAIN, ROPE, ROPE_SCALED = 0, 1, 2
LOG2_E = 1.4426950408889634
LN_2 = 0.6931471805599453
Q_SCALE = HEAD_DIM ** -0.5 * LOG2_E


def _params(*sem):
    return pltpu.CompilerParams(dimension_semantics=sem, vmem_limit_bytes=VMEM_LIMIT)


def _rms(x, g):
    ms = jnp.mean(x * x, axis=-1, keepdims=True)
    return x * lax.rsqrt(ms + EPS) * g


def _pair_columns(w, adjacent):
    d, heads = w.shape[0], w.shape[1] // HEAD_DIM
    if adjacent:
        return w.reshape(d, heads // 2, 2, 2, HALF_DIM).transpose(0, 1, 3, 2, 4).reshape(w.shape)
    return w.reshape(d, 2, heads // 2, 2, HALF_DIM).transpose(0, 2, 3, 1, 4).reshape(w.shape)


def _rope_tables(positions):
    pos = jnp.asarray(positions, F32)
    inv = 1.0 / (ROPE_THETA ** (jnp.arange(HALF_DIM, dtype=F32) / HALF_DIM))
    ang = pos[:, None] * inv[None, :]
    cos, sin = jnp.cos(ang), jnp.sin(ang)
    return jnp.tile(cos, (1, 4)), jnp.concatenate([-sin, -sin, sin, sin], axis=1)


def _rope_coeffs(cos, sin):
    a = jnp.stack([cos * Q_SCALE, cos, jnp.ones_like(cos)])
    b = jnp.stack([sin * Q_SCALE, sin, jnp.zeros_like(sin)])
    return a, b


def _store_chunks(acc, cos, sin, o_ref, kinds):
    for c, kind in enumerate(kinds):
        seg = acc[:, c * LANES:(c + 1) * LANES]
        if kind != PLAIN:
            seg = seg * cos + pltpu.roll(seg, LANES // 2, 1) * sin
            if kind == ROPE_SCALED:
                seg = seg * Q_SCALE
        o_ref[:, c * LANES:(c + 1) * LANES] = seg.astype(o_ref.dtype)


def _to_token_tiles(val, ref):
    rows = val.shape[0]
    for c in range(val.shape[1] // LANES):
        ref[pl.ds(c, rows, stride=SUBLANES), :] = val[:, c * LANES:(c + 1) * LANES]


def _from_token_tiles(ref, rows):
    return jnp.concatenate([ref[pl.ds(c, rows, stride=SUBLANES), :] for c in range(SUBLANES)], axis=1)


def _tile_at(ref, line):
    return ref.at[pl.ds(pl.multiple_of(line, SUBLANES), SUBLANES)]


def _group_proj_kernel(x_ref, g_ref, w_ref, a_ref, b_ref, perm_ref, cast_ref, o_ref, cast_out_ref, *, dil):
    cast_out_ref[...] = cast_ref[...].astype(cast_out_ref.dtype)
    h = _rms(x_ref[...], g_ref[...]).astype(BF16)
    if dil > 1:
        h = jnp.dot(perm_ref[...], h, preferred_element_type=F32).astype(BF16)
    sub = h.shape[0] // dil
    tn = w_ref.shape[1] // 3
    for kind in range(3):
        acc = jnp.dot(h, w_ref[:, kind * tn:(kind + 1) * tn], preferred_element_type=F32)
        a, b = a_ref[kind], b_ref[kind]
        for c in range(tn // LANES):
            seg = acc[:, c * LANES:(c + 1) * LANES]
            seg = (seg * a + pltpu.roll(seg, LANES // 2, 1) * b).astype(o_ref.dtype)
            cols = slice(kind * tn + c * LANES, kind * tn + (c + 1) * LANES)
            if dil == 1:
                o_ref[:, cols] = seg
            else:
                for r in range(dil):
                    o_ref[r, :, cols] = seg[r * sub:(r + 1) * sub]


def _residue_major_rows(dil):
    q = np.arange(TM)
    sub = TM // dil
    return (q % sub) * dil + q // sub


def _group_proj(x, g, w, batch, seq, dil, cast_src):
    t, d = x.shape
    n_out = w.shape[1]
    tiles_per_seq = seq // TM
    steps = t // TM
    cast2d = cast_src.reshape(-1, cast_src.shape[-1])
    slab = cast2d.shape[0] // steps
    assert cast2d.shape[0] % steps == 0 and slab % (2 * SUBLANES) == 0
    cast_spec = pl.BlockSpec((slab, cast2d.shape[1]), lambda i: (i, 0))
    if dil == 1:
        out_spec = pl.BlockSpec((TM, n_out), lambda i: (i, 0))
        out_shape = jax.ShapeDtypeStruct((t, n_out), BF16)
    else:
        out_spec = pl.BlockSpec((None, dil, TM // dil, n_out),
                                lambda i: (i // tiles_per_seq, 0, i % tiles_per_seq, 0))
        out_shape = jax.ShapeDtypeStruct((batch, dil, seq // dil, n_out), BF16)
    nat = _residue_major_rows(dil)
    perm = jnp.asarray(nat[:, None] == np.arange(TM)[None, :], BF16)
    coef_a, coef_b = _rope_coeffs(*_rope_tables((np.arange(seq) // TM) * TM + nat[np.arange(seq) % TM]))
    coef_spec = pl.BlockSpec((3, TM, LANES), lambda i: (0, i % tiles_per_seq, 0))
    qkv, cast = pl.pallas_call(
        functools.partial(_group_proj_kernel, dil=dil),
        grid=(steps,),
        in_specs=[
            pl.BlockSpec((TM, d), lambda i: (i, 0)),
            pl.BlockSpec((1, d), lambda i: (0, 0)),
            pl.BlockSpec(w.shape, lambda i: (0, 0), pipeline_mode=pl.Buffered(1)),
            coef_spec, coef_spec,
            pl.BlockSpec((TM, TM), lambda i: (0, 0), pipeline_mode=pl.Buffered(1)),
            cast_spec,
        ],
        out_specs=[out_spec, cast_spec],
        out_shape=[out_shape, jax.ShapeDtypeStruct(cast2d.shape, BF16)],
        compiler_params=_params("parallel"),
        name=f"group_proj_d{dil}",
    )(x, g, w, coef_a, coef_b, perm, cast2d)
    return qkv, cast.reshape(cast_src.shape)


def _qkv1_kernel(x_ref, gq_ref, gkv_ref, wq_ref, wkv_ref, cos_ref, sin_ref, q_ref, kv_ref):
    x = x_ref[...]
    xn = x * lax.rsqrt(jnp.mean(x * x, axis=-1, keepdims=True) + EPS)
    hq = (xn * gq_ref[...]).astype(BF16)
    hkv = (xn * gkv_ref[...]).astype(BF16)
    cos, sin = cos_ref[...], sin_ref[...]
    accq = jnp.dot(hq, wq_ref[...], preferred_element_type=F32)
    _store_chunks(accq, cos, sin, q_ref, (ROPE_SCALED,) * (accq.shape[1] // LANES))
    acckv = jnp.dot(hkv, wkv_ref[...], preferred_element_type=F32)
    _store_chunks(acckv, cos, sin, kv_ref, (ROPE, PLAIN))


def _qkv1(x, gq, gkv, wq, wkv, cos, sin):
    t, d = x.shape
    s_tiles = cos.shape[0] // TM
    row = lambda i: (i, 0)
    fixed = lambda i: (0, 0)
    return pl.pallas_call(
        _qkv1_kernel,
        grid=(t // TM,),
        in_specs=[
            pl.BlockSpec((TM, d), row),
            pl.BlockSpec((1, d), fixed),
            pl.BlockSpec((1, d), fixed),
            pl.BlockSpec(wq.shape, fixed),
            pl.BlockSpec(wkv.shape, fixed),
            pl.BlockSpec((TM, LANES), lambda i: (i % s_tiles, 0)),
            pl.BlockSpec((TM, LANES), lambda i: (i % s_tiles, 0)),
        ],
        out_specs=[pl.BlockSpec((TM, wq.shape[1]), row), pl.BlockSpec((TM, wkv.shape[1]), row)],
        out_shape=[jax.ShapeDtypeStruct((t, wq.shape[1]), BF16),
                   jax.ShapeDtypeStruct((t, wkv.shape[1]), BF16)],
        compiler_params=_params("parallel"),
        name="qkv_layer1",
    )(x, gq, gkv, wq, wkv, cos, sin)


def _band_mask(has_prev, max_dist):
    qi = lax.broadcasted_iota(jnp.int32, (BLOCK, 2 * BLOCK), 0) + BLOCK
    kj = lax.broadcasted_iota(jnp.int32, (BLOCK, 2 * BLOCK), 1)
    rel = qi - kj
    mask = (rel >= 0) & (rel <= max_dist)
    if has_prev is not True:
        mask = mask & ((kj >= BLOCK) | has_prev)
    return jnp.concatenate([mask, mask], axis=0)


def _band_blocks(qb, prev_ref, cur_ref, lanes):
    own = cur_ref[qb * BLOCK:(qb + 1) * BLOCK, lanes]
    before = prev_ref[:, lanes] if qb == 0 else cur_ref[(qb - 1) * BLOCK:qb * BLOCK, lanes]
    return jnp.concatenate([before, own], axis=0)


def _lane_masks():
    lane = lax.broadcasted_iota(jnp.int32, (BLOCK, LANES), 1)
    first_qk = (lane % HEAD_DIM) < HALF_DIM
    sel_a = jnp.where(first_qk, 1.0, 0.0).astype(BF16)
    sel_b = jnp.where(first_qk, 0.0, 1.0).astype(BF16)
    return lane, sel_a, sel_b, lane < HEAD_DIM


def _pair_scores(q2, kcat, sel_a, sel_b):
    qs = jnp.concatenate([q2 * sel_a, q2 * sel_b], axis=0)
    return lax.dot_general(qs, kcat, (((1,), (1,)), ((), ())), preferred_element_type=F32)


def _dil_attn_kernel(q_ref, kp_ref, kc_ref, vp_ref, vc_ref, o_ref, st_ref, *, steps):
    lane, sel_a, sel_b, first_v = _lane_masks()
    for qb in range(Q_BLOCKS):
        rows = slice(qb * BLOCK, (qb + 1) * BLOCK)
        mask2 = _band_mask(True if qb else pl.program_id(2) > 0, steps)
        stats = jnp.zeros((BLOCK, LANES), F32)
        for p in range(q_ref.shape[1] // LANES):
            sl = slice(p * LANES, (p + 1) * LANES)
            kcat = _band_blocks(qb, kp_ref, kc_ref, sl)
            vcat = _band_blocks(qb, vp_ref, vc_ref, sl)
            s = jnp.where(mask2, _pair_scores(q_ref[rows, sl], kcat, sel_a, sel_b), NEG_INF)
            m = jnp.max(s, axis=-1, keepdims=True)
            pe = jnp.exp2(s - m)
            den = jnp.sum(pe, axis=-1, keepdims=True)
            pv = jnp.dot(pe.astype(BF16), vcat, preferred_element_type=F32)
            pv = pv * (1.0 / den)
            o_ref[rows, sl] = jnp.where(first_v, pv[:BLOCK], pv[BLOCK:]).astype(o_ref.dtype)
            lse = m * LN_2 + jnp.log(den)
            stats = jnp.where(lane == 2 * p, lse[:BLOCK], stats)
            stats = jnp.where(lane == 2 * p + 1, lse[BLOCK:], stats)
        st_ref[rows, :] = stats


def _dilated_group(qkv, batch, seq, g):
    win, dil = A_CONFIGS[g]
    n = seq // dil
    d = qkv.shape[-1] // 3
    view = qkv.reshape(batch, dil, n, 3 * d)

    rows = Q_BLOCKS * BLOCK

    def spec(kind, prev):
        if prev:
            return pl.BlockSpec((None, None, BLOCK, d),
                                lambda b, r, blk: (b, r, jnp.maximum(blk * Q_BLOCKS - 1, 0), kind))
        return pl.BlockSpec((None, None, rows, d), lambda b, r, blk: (b, r, blk, kind))

    return pl.pallas_call(
        functools.partial(_dil_attn_kernel, steps=win // dil),
        grid=(batch, dil, n // rows),
        in_specs=[spec(0, False), spec(1, True), spec(1, False), spec(2, True), spec(2, False)],
        out_specs=[pl.BlockSpec((None, None, rows, d), lambda b, r, blk: (b, r, blk, 0)),
                   pl.BlockSpec((None, None, rows, LANES), lambda b, r, blk: (b, r, blk, 0))],
        out_shape=[jax.ShapeDtypeStruct((batch, dil, n, d), BF16),
                   jax.ShapeDtypeStruct((batch, dil, n, LANES), F32)],
        compiler_params=_params("parallel", "parallel", "arbitrary"),
        name=f"dilated_attn_g{g}",
    )(view, view, view, view, view)


def _swa_kernel(fill_ref, q_ref, kp_ref, kc_ref, vp_ref, vc_ref, o_ref):
    _, sel_a, sel_b, first_v = _lane_masks()
    all_lanes = slice(0, LANES)
    key_row = lax.broadcasted_iota(jnp.int32, (2 * BLOCK, LANES), 0)
    for qb in range(Q_BLOCKS):
        rows = slice(qb * BLOCK, (qb + 1) * BLOCK)
        mask2 = _band_mask(True if qb else pl.program_id(1) > 0, B_WINDOW - 1)
        kcat = _band_blocks(qb, kp_ref, kc_ref, all_lanes)
        vcat = _band_blocks(qb, vp_ref, vc_ref, all_lanes)
        vcat = jnp.where(key_row == 0, jnp.zeros_like(vcat), vcat)
        for j in range(q_ref.shape[1] // LANES):
            sl = slice(j * LANES, (j + 1) * LANES)
            s = jnp.where(mask2, _pair_scores(q_ref[rows, sl], kcat, sel_a, sel_b), fill_ref[j])
            m = jnp.max(s, axis=-1, keepdims=True)
            pe = jnp.exp2(s - m)
            den = jnp.sum(pe, axis=-1, keepdims=True)
            pv = jnp.dot(pe.astype(BF16), vcat, preferred_element_type=F32)
            pv = pv * (1.0 / den)
            o_ref[rows, sl] = jnp.where(first_v, pv[:BLOCK], pv[BLOCK:]).astype(o_ref.dtype)


def _swa_attention(q, kv, sinks, batch, seq):
    d = q.shape[1]
    rows = Q_BLOCKS * BLOCK
    qv = q.reshape(batch, seq, d)
    kvv = kv.reshape(batch, seq, 2 * LANES)

    def kv_spec(col, prev):
        if prev:
            return pl.BlockSpec((None, BLOCK, LANES),
                                lambda b, blk: (b, jnp.maximum(blk * Q_BLOCKS - 1, 0), col))
        return pl.BlockSpec((None, rows, LANES), lambda b, blk: (b, blk, col))

    half = sinks.shape[0] // 2
    sink_rows = jnp.repeat(jnp.stack([sinks[:half], sinks[half:]], axis=1), BLOCK, axis=1)
    first_key = jnp.arange(2 * BLOCK)[None, None, :] == 0
    fill = jnp.where(first_key, sink_rows[:, :, None].astype(F32) * LOG2_E, NEG_INF)

    o = pl.pallas_call(
        _swa_kernel,
        grid=(batch, seq // rows),
        in_specs=[pl.BlockSpec(fill.shape, lambda b, blk: (0, 0, 0)),
                  pl.BlockSpec((None, rows, d), lambda b, blk: (b, blk, 0)),
                  kv_spec(0, True), kv_spec(0, False), kv_spec(1, True), kv_spec(1, False)],
        out_specs=pl.BlockSpec((None, rows, d), lambda b, blk: (b, blk, 0)),
        out_shape=jax.ShapeDtypeStruct((batch, seq, d), BF16),
        compiler_params=_params("parallel", "arbitrary"),
        name="swa_sink_attn",
    )(fill, qv, kvv, kvv, kvv, kvv)
    return o.reshape(batch * seq, d)


def _interleave(src_ref, dst_ref):
    dil, sub = src_ref.shape[0], src_ref.shape[1]
    chunks = dst_ref.shape[0]
    for r in range(dil):
        for c in range(chunks):
            part = src_ref[r, :, c * LANES:(c + 1) * LANES].astype(F32)
            dst_ref[c, pl.ds(r, sub, stride=dil), :] = part
    return jnp.concatenate([dst_ref[c] for c in range(chunks)], axis=1)


def _merge_wo_kernel(x_ref, o0_ref, o1_ref, o2_ref, s0_ref, s1_ref, s2_ref, e_ref, w_ref, out_ref,
                     ob_ref, sb1_ref, sb2_ref):
    s0, s1, s2 = s0_ref[0], _interleave(s1_ref, sb1_ref), _interleave(s2_ref, sb2_ref)
    top = jnp.maximum(jnp.maximum(s0, s1), s2)
    e0, e1, e2 = jnp.exp(s0 - top), jnp.exp(s1 - top), jnp.exp(s2 - top)
    inv = 1.0 / (e0 + e1 + e2)
    expand = e_ref[...]

    def weight(e):
        return jnp.dot((e * inv).astype(BF16), expand, preferred_element_type=F32)

    merged = weight(e0) * o0_ref[0].astype(F32)
    merged = merged + weight(e1) * _interleave(o1_ref, ob_ref)
    merged = merged + weight(e2) * _interleave(o2_ref, ob_ref)
    out_ref[...] = x_ref[...] + jnp.dot(merged.astype(BF16), w_ref[...], preferred_element_type=F32)


def _merge_wo(x, outs, stats, expand, w, batch, seq):
    t, d = x.shape
    tiles_per_seq = seq // TM
    row = lambda i: (i, 0)
    fixed = lambda i: (0, 0)

    def sub(arr):
        dil = arr.shape[1]
        return pl.BlockSpec((None, dil, TM // dil, arr.shape[3]),
                            lambda i: (i // tiles_per_seq, 0, i % tiles_per_seq, 0))

    return pl.pallas_call(
        _merge_wo_kernel,
        grid=(t // TM,),
        in_specs=[pl.BlockSpec((TM, d), row)]
        + [sub(o) for o in outs] + [sub(s) for s in stats]
        + [pl.BlockSpec(expand.shape, fixed), pl.BlockSpec(w.shape, fixed)],
        out_specs=pl.BlockSpec((TM, d), row),
        out_shape=jax.ShapeDtypeStruct((t, d), F32),
        scratch_shapes=[pltpu.VMEM((d // LANES, TM, LANES), F32), pltpu.VMEM((1, TM, LANES), F32),
                        pltpu.VMEM((1, TM, LANES), F32)],
        compiler_params=_params("parallel"),
        name="merge_wo",
    )(x, *outs, *stats, expand, w)


def _wo_router_kernel(x_ref, o_ref, w_ref, g_ref, wr_ref, tri_ref, x1_ref, x1t_ref, route_ref, gates_ref,
                      counts_ref, run_ref):
    @pl.when(pl.program_id(0) == 0)
    def _():
        run_ref[...] = jnp.zeros_like(run_ref)

    x1 = x_ref[...] + jnp.dot(o_ref[...], w_ref[...], preferred_element_type=F32)
    x1_ref[...] = x1
    _to_token_tiles(x1, x1t_ref)
    h = _rms(x1, g_ref[...]).astype(BF16)
    logits = jnp.dot(h, wr_ref[...], preferred_element_type=F32)
    lane = lax.broadcasted_iota(jnp.int32, logits.shape, 1)
    logits = jnp.where(lane < N_EXPERTS, logits, -jnp.inf)
    v1 = jnp.max(logits, axis=-1, keepdims=True)
    i1 = jnp.min(jnp.where(logits == v1, lane, LANES), axis=-1, keepdims=True)
    rest = jnp.where(lane == i1, -jnp.inf, logits)
    v2 = jnp.max(rest, axis=-1, keepdims=True)
    i2 = jnp.min(jnp.where(rest == v2, lane, LANES), axis=-1, keepdims=True)
    e2 = jnp.exp(v2 - v1)
    inv = 1.0 / (1.0 + e2)
    gates_ref[:, :LANES] = jnp.broadcast_to(inv, logits.shape)
    gates_ref[:, LANES:] = jnp.broadcast_to(e2 * inv, logits.shape)
    picked = jnp.where(lane == i1, 1.0, 0.0) + jnp.where(lane == i2, 1.0, 0.0)
    before = run_ref[0:1, :] + jnp.dot(tri_ref[...], picked.astype(BF16), preferred_element_type=F32)
    rank1 = jnp.sum(jnp.where(lane == i1, before, 0.0), axis=-1, keepdims=True)
    rank2 = jnp.sum(jnp.where(lane == i2, before, 0.0), axis=-1, keepdims=True)
    total = run_ref[0:1, :] + jnp.sum(picked, axis=0, keepdims=True)
    run_ref[0:1, :] = total
    counts_ref[...] = jnp.broadcast_to(total, counts_ref.shape)
    info = jnp.where(lane == 0, i1.astype(F32), 0.0)
    info = jnp.where(lane == 1, i2.astype(F32), info)
    info = jnp.where(lane == 2, rank1, info)
    info = jnp.where(lane == 3, rank2, info)
    route_ref[...] = info.T[:SUBLANES, :]


def _wo_router(x, o, w, g, wr):
    t, d = x.shape
    row = lambda i: (i, 0)
    fixed = lambda i: (0, 0)
    tri = jnp.asarray(np.tril(np.ones((TM, TM), np.float32), -1), BF16)
    return pl.pallas_call(
        _wo_router_kernel,
        grid=(t // TM,),
        in_specs=[pl.BlockSpec((TM, d), row), pl.BlockSpec((TM, o.shape[1]), row),
                  pl.BlockSpec(w.shape, fixed), pl.BlockSpec((1, d), fixed),
                  pl.BlockSpec(wr.shape, fixed), pl.BlockSpec(tri.shape, fixed)],
        out_specs=[pl.BlockSpec((TM, d), row), pl.BlockSpec((TM * SUBLANES, LANES), row),
                   pl.BlockSpec((SUBLANES, TM), lambda i: (0, i)), pl.BlockSpec((TM, 2 * LANES), row),
                   pl.BlockSpec((SUBLANES, LANES), fixed)],
        out_shape=[jax.ShapeDtypeStruct((t, d), F32),
                   jax.ShapeDtypeStruct((t * SUBLANES, LANES), F32),
                   jax.ShapeDtypeStruct((SUBLANES, t), F32),
                   jax.ShapeDtypeStruct((t, 2 * LANES), F32),
                   jax.ShapeDtypeStruct((SUBLANES, LANES), F32)],
        scratch_shapes=[pltpu.VMEM((SUBLANES, LANES), F32)],
        compiler_params=_params("arbitrary"),
        name="wo_router",
    )(x, o, w, g, wr, tri)


def _swiglu(h, w1_ref, w3_ref, w2_ref):
    width = w1_ref.shape[1] // SWIGLU_SPLITS
    y = None
    for c in range(SWIGLU_SPLITS):
        sl = slice(c * width, (c + 1) * width)
        a = jnp.dot(h, w1_ref[:, sl], preferred_element_type=F32)
        b = jnp.dot(h, w3_ref[:, sl], preferred_element_type=F32)
        act = (a * jax.nn.sigmoid(a) * b).astype(BF16)
        part = jnp.dot(act, w2_ref[sl, :], preferred_element_type=F32)
        y = part if y is None else y + part
    return y


def _ffn_kernel(x_ref, g_ref, w1_ref, w3_ref, w2_ref, o_ref):
    x = x_ref[...]
    o_ref[...] = x + _swiglu(_rms(x, g_ref[...]).astype(BF16), w1_ref, w3_ref, w2_ref)


def _ffn(x, g, w1, w3, w2):
    t, d = x.shape
    fixed = lambda i: (0, 0)
    resident = pl.Buffered(1)
    return pl.pallas_call(
        _ffn_kernel,
        grid=(t // TM,),
        in_specs=[pl.BlockSpec((TM, d), lambda i: (i, 0)),
                  pl.BlockSpec((1, d), fixed),
                  pl.BlockSpec(w1.shape, fixed, pipeline_mode=resident),
                  pl.BlockSpec(w3.shape, fixed, pipeline_mode=resident),
                  pl.BlockSpec(w2.shape, fixed, pipeline_mode=resident)],
        out_specs=pl.BlockSpec((TM, d), lambda i: (i, 0)),
        out_shape=jax.ShapeDtypeStruct((t, d), F32),
        compiler_params=_params("parallel"),
        name="dense_swiglu",
    )(x, g, w1, w3, w2)


def _moe_kernel(te_ref, nu_ref, xs_ref, g_ref, w1_ref, w3_ref, w2_ref, o_ref):
    used = pl.program_id(0) < nu_ref[0]

    @pl.when(used)
    def _():
        x = _from_token_tiles(xs_ref, xs_ref.shape[0] // SUBLANES)
        _to_token_tiles(_swiglu(_rms(x, g_ref[...]).astype(BF16), w1_ref, w3_ref, w2_ref), o_ref)

    @pl.when(jnp.logical_not(used))
    def _():
        o_ref[...] = jnp.zeros_like(o_ref)


def _moe_experts(xs, g, w1, w3, w2, tile_expert, n_used, n_tiles):
    d, f = w1.shape[1], w1.shape[2]
    lines = TM * SUBLANES
    resident = pl.Buffered(1)
    grid_spec = pltpu.PrefetchScalarGridSpec(
        num_scalar_prefetch=2,
        grid=(n_tiles,),
        in_specs=[pl.BlockSpec((lines, LANES), lambda i, te, nu: (i, 0)),
                  pl.BlockSpec((1, d), lambda i, te, nu: (0, 0)),
                  pl.BlockSpec((None, d, f), lambda i, te, nu: (te[i], 0, 0), pipeline_mode=resident),
                  pl.BlockSpec((None, d, f), lambda i, te, nu: (te[i], 0, 0), pipeline_mode=resident),
                  pl.BlockSpec((None, f, d), lambda i, te, nu: (te[i], 0, 0), pipeline_mode=resident)],
        out_specs=pl.BlockSpec((lines, LANES), lambda i, te, nu: (i, 0)),
    )
    return pl.pallas_call(
        _moe_kernel,
        grid_spec=grid_spec,
        out_shape=jax.ShapeDtypeStruct((n_tiles * lines, LANES), F32),
        compiler_params=_params("arbitrary"),
        name="moe_experts",
    )(tile_expert, n_used, xs, g, w1, w3, w2)


def _dispatch_kernel(dst0_ref, dst1_ref, pad_ref, src_ref, zero_ref, out_hbm, sem):
    rows = dst0_ref.shape[2]
    n_pad = pad_ref.shape[0]

    def issue(r, carry):
        base = r * DMA_UNROLL
        idx = [[ref[0, 0, base + q] for q in range(DMA_UNROLL)] for ref in (dst0_ref, dst1_ref)]
        for u in range(DMA_UNROLL):
            src = _tile_at(src_ref, (base + u) * SUBLANES)
            for k in range(2):
                pltpu.make_async_copy(src, _tile_at(out_hbm, idx[k][u]), sem).start(priority=k)
        return carry

    lax.fori_loop(0, rows // DMA_UNROLL, issue, 0)
    lines = 2 * rows * SUBLANES
    pltpu.make_async_copy(out_hbm.at[pl.ds(0, lines)], out_hbm.at[pl.ds(0, lines)], sem).wait()

    @pl.when(pl.program_id(0) == 0)
    def _():
        def fill(r, carry):
            base = r * DMA_UNROLL
            idx = [pad_ref[base + q] for q in range(DMA_UNROLL)]
            for q in range(DMA_UNROLL):
                pltpu.make_async_copy(zero_ref, _tile_at(out_hbm, idx[q]), sem).start()
            return carry

        lax.fori_loop(0, n_pad // DMA_UNROLL, fill, 0)
        pad_lines = n_pad * SUBLANES
        pltpu.make_async_copy(out_hbm.at[pl.ds(0, pad_lines)], out_hbm.at[pl.ds(0, pad_lines)],
                              sem).wait()


def _dispatch(x_tiles, dest_line, pad_line, n_slots):
    t = x_tiles.shape[0] // SUBLANES
    steps = t // DISPATCH_ROWS
    zero = jnp.zeros((SUBLANES, LANES), x_tiles.dtype)
    idx = dest_line.reshape(2 * steps, 1, DISPATCH_ROWS)
    idx_spec = lambda pick: pl.BlockSpec((1, 1, DISPATCH_ROWS), lambda i: (pick * steps + i, 0, 0),
                                         memory_space=pltpu.SMEM)
    return pl.pallas_call(
        _dispatch_kernel,
        grid=(steps,),
        in_specs=[idx_spec(0), idx_spec(1),
                  pl.BlockSpec(memory_space=pltpu.SMEM),
                  pl.BlockSpec((DISPATCH_ROWS * SUBLANES, LANES), lambda i: (i, 0)),
                  pl.BlockSpec((SUBLANES, LANES), lambda i: (0, 0))],
        out_specs=pl.BlockSpec(memory_space=pl.ANY),
        out_shape=jax.ShapeDtypeStruct((n_slots * SUBLANES, LANES), x_tiles.dtype),
        scratch_shapes=[pltpu.SemaphoreType.DMA(())],
        compiler_params=_params("arbitrary"),
        name="moe_dispatch",
    )(idx, idx, pad_line, x_tiles, zero)


def _ple_rows(x, p, g_ref, wg_ref, wp_ref, between=None):
    h = _rms(x, g_ref[...]).astype(BF16)
    gate = []
    for c in range(wg_ref.shape[1] // MXU_COLS):
        if between is not None:
            between(c)
        logits = jnp.dot(h, wg_ref[:, c * MXU_COLS:(c + 1) * MXU_COLS], preferred_element_type=F32)
        gate.append(jax.nn.sigmoid(logits))
    proj = jnp.dot(p.astype(BF16), wp_ref[...], preferred_element_type=F32)
    return x + proj * jnp.concatenate(gate, axis=1)


def _combine_ple_kernel(cur0_ref, cur1_ref, nxt0_ref, nxt1_ref, x_ref, gates_ref, p_ref, g_ref, wg_ref,
                        wp_ref, fg_ref, ys_hbm, o_ref, a_ref, b_ref, sem):
    i = pl.program_id(0)
    last = pl.num_programs(0) - 1
    tm = a_ref.shape[1] // SUBLANES
    groups = tm // DMA_UNROLL
    cur_ref, nxt_ref = (cur0_ref, cur1_ref), (nxt0_ref, nxt1_ref)

    def issue_group(idx_ref, offset, slot, base):
        idx = [[ref[0, 0, offset + base + q] for q in range(DMA_UNROLL)] for ref in idx_ref]
        for u in range(DMA_UNROLL):
            line = (base + u) * SUBLANES
            pltpu.make_async_copy(_tile_at(ys_hbm, idx[0][u]), _tile_at(a_ref.at[slot], line),
                                  sem.at[slot]).start(priority=0)
            pltpu.make_async_copy(_tile_at(ys_hbm, idx[1][u]), _tile_at(b_ref.at[slot], line),
                                  sem.at[slot]).start(priority=1)

    def wait(slot):
        whole = ys_hbm.at[pl.ds(0, tm * SUBLANES)]
        pltpu.make_async_copy(whole, a_ref.at[slot], sem.at[slot]).wait()
        pltpu.make_async_copy(whole, b_ref.at[slot], sem.at[slot]).wait()

    def compute(slot, idx_ref, offset):
        rows = slice(slot * tm, (slot + 1) * tm)
        g1, g2 = gates_ref[rows, :LANES], gates_ref[rows, LANES:]
        y = jnp.concatenate(
            [g1 * a_ref[slot, pl.ds(c, tm, stride=SUBLANES), :]
             + g2 * b_ref[slot, pl.ds(c, tm, stride=SUBLANES), :] for c in range(SUBLANES)], axis=1)
        chunks = wg_ref.shape[1] // MXU_COLS

        def between(c):
            for grp in range(c * groups // chunks, (c + 1) * groups // chunks):
                issue_group(idx_ref, offset, 1 - slot, grp * DMA_UNROLL)

        x = _ple_rows(x_ref[rows, :] + y, p_ref[rows, :], g_ref, wg_ref, wp_ref, between)
        o_ref[rows, :] = _rms(x, fg_ref[...])

    @pl.when(i == 0)
    def _():
        lax.fori_loop(0, groups, lambda r, c: (issue_group(cur_ref, 0, 0, r * DMA_UNROLL), c)[1], 0)

    wait(0)
    compute(0, cur_ref, tm)
    wait(1)
    compute(1, nxt_ref, 0)

    @pl.when(i == last)
    def _():
        wait(0)


def _combine_ple(x, gates, ys, dest_line, p, layer, g, wg, wp, fg):
    t, d = x.shape
    tm = TM_COMBINE
    steps = t // (2 * tm)
    row = lambda i: (i, 0)
    fixed = lambda i: (0, 0)
    idx = dest_line.reshape(2 * steps, 1, 2 * tm)
    idx_spec = lambda index: pl.BlockSpec((1, 1, 2 * tm), index, memory_space=pltpu.SMEM)
    nxt = lambda i: jnp.minimum(i + 1, steps - 1)
    return pl.pallas_call(
        _combine_ple_kernel,
        grid=(steps,),
        in_specs=[idx_spec(lambda i: (i, 0, 0)), idx_spec(lambda i: (steps + i, 0, 0)),
                  idx_spec(lambda i: (nxt(i), 0, 0)), idx_spec(lambda i: (steps + nxt(i), 0, 0)),
                  pl.BlockSpec((2 * tm, d), row),
                  pl.BlockSpec((2 * tm, 2 * LANES), row),
                  pl.BlockSpec((None, 2 * tm, p.shape[2]), lambda i: (layer, i, 0)),
                  pl.BlockSpec((1, d), fixed),
                  pl.BlockSpec(wg.shape, fixed),
                  pl.BlockSpec(wp.shape, fixed),
                  pl.BlockSpec((1, d), fixed),
                  pl.BlockSpec(memory_space=pl.ANY)],
        out_specs=pl.BlockSpec((2 * tm, d), row),
        out_shape=jax.ShapeDtypeStruct((t, d), F32),
        scratch_shapes=[pltpu.VMEM((2, tm * SUBLANES, LANES), F32),
                        pltpu.VMEM((2, tm * SUBLANES, LANES), F32),
                        pltpu.SemaphoreType.DMA((2,))],
        compiler_params=_params("arbitrary"),
        name="combine_ple_final",
    )(idx, idx, idx, idx, x, gates, p, g, wg, wp, fg, ys)


def _ple_kernel(x_ref, p_ref, g_ref, wg_ref, wp_ref, o_ref):
    o_ref[...] = _ple_rows(x_ref[...], p_ref[...], g_ref, wg_ref, wp_ref)


def _ple(x, p, layer, g, wg, wp):
    t, d = x.shape
    fixed = lambda i: (0, 0)
    return pl.pallas_call(
        _ple_kernel,
        grid=(t // TM,),
        in_specs=[pl.BlockSpec((TM, d), lambda i: (i, 0)),
                  pl.BlockSpec((None, TM, p.shape[2]), lambda i: (layer, i, 0)),
                  pl.BlockSpec((1, d), fixed),
                  pl.BlockSpec(wg.shape, fixed),
                  pl.BlockSpec(wp.shape, fixed)],
        out_specs=pl.BlockSpec((TM, d), lambda i: (i, 0)),
        out_shape=jax.ShapeDtypeStruct((t, d), F32),
        compiler_params=_params("parallel"),
        name="ple",
    )(x, p, g, wg, wp)


def _routing_tables(route, counts_f):
    t = route.shape[1]
    picked = route[0:2].astype(jnp.int32)
    rank = route[2:4].astype(jnp.int32)
    counts = counts_f[0, :N_EXPERTS].astype(jnp.int32)
    tiles_per = (counts + TM - 1) // TM
    tile_end = jnp.cumsum(tiles_per)
    tile_start = tile_end - tiles_per
    first_slot = jnp.zeros_like(picked)
    for e in range(N_EXPERTS):
        first_slot = jnp.where(picked == e, tile_start[e] * TM, first_slot)
    dest = first_slot + rank
    n_tiles = (2 * t) // TM + N_EXPERTS
    n_used = tile_end[-1]
    tile_ids = jnp.minimum(jnp.arange(n_tiles), n_used - 1)
    tile_expert = jnp.minimum(jnp.sum((tile_ids[:, None] >= tile_end[None, :]).astype(jnp.int32), axis=1),
                              N_EXPERTS - 1)
    pad_sizes = jnp.concatenate([tiles_per * TM - counts, ((n_tiles - n_used) * TM).reshape(1)])
    pad_end = jnp.cumsum(pad_sizes)
    seg_first = jnp.concatenate([tile_start * TM + counts, (n_used * TM).reshape(1)])
    k = jnp.arange(N_EXPERTS * TM)
    seg = jnp.sum((k[:, None] >= pad_end[None, :]).astype(jnp.int32), axis=1)
    pad = seg_first[seg] + k - (pad_end - pad_sizes)[seg]
    to_line = lambda v: (v * SUBLANES).astype(jnp.int32)
    return (to_line(dest), to_line(pad), tile_expert.astype(jnp.int32),
            n_used.reshape(1).astype(jnp.int32), n_tiles)


def kernel(x, p, attn_norm, ffn_norm, a_w_qkv, a_w_o, kv_norm, kv_w, b_w_q, b_sinks, b_w_o,
           dense_w1, dense_w3, dense_w2, moe_router, moe_w1, moe_w3, moe_w2,
           ple_norm, ple_w_gate, ple_w_proj, final_norm):
    batch, seq, d = x.shape
    t = batch * seq
    n_heads = d // HEAD_DIM
    n_groups = len(A_CONFIGS)
    assert d == SUBLANES * LANES and seq % TM == 0 and t % DISPATCH_ROWS == 0
    xr = x.reshape(t, d)
    pr = p.reshape(p.shape[0], t, p.shape[3])
    cos, sin = _rope_tables(np.arange(seq))
    row = lambda v: v.reshape(1, -1)

    w = a_w_qkv[0].astype(BF16).reshape(d, n_groups, 3, d)
    outs, stats, moe_bf16 = [], [], []
    for g, (_, dil) in enumerate(A_CONFIGS):
        w_g = jnp.concatenate([_pair_columns(w[:, g, 0], True), _pair_columns(w[:, g, 1], True),
                               w[:, g, 2]], axis=1)
        qkv, cast = _group_proj(xr, row(attn_norm[0]), w_g, batch, seq, dil,
                                (moe_w1, moe_w3, moe_w2)[g][0])
        moe_bf16.append(cast)
        o_g, st_g = _dilated_group(qkv, batch, seq, g)
        outs.append(o_g)
        stats.append(st_g)
    head_of_col = jnp.arange(d) // HEAD_DIM
    expand = (jnp.arange(LANES)[:, None] == head_of_col[None, :]).astype(BF16)
    xr = _merge_wo(xr, outs, stats, expand, a_w_o[0].astype(BF16), batch, seq)
    xr = _ffn(xr, row(ffn_norm[0]), dense_w1[0].astype(BF16), dense_w3[0].astype(BF16),
              dense_w2[0].astype(BF16))
    xr = _ple(xr, pr, 0, row(ple_norm[0]), ple_w_gate[0].astype(BF16), ple_w_proj[0].astype(BF16))

    half = n_heads // 2
    kv_cols = B_KV_HEADS * HEAD_DIM
    kv_bf = kv_w.astype(BF16)
    w_kv = jnp.concatenate([_pair_columns(kv_bf[:, :kv_cols], True), kv_bf[:, kv_cols:]], axis=1)
    q1, kv1 = _qkv1(xr, row(attn_norm[1]), row(kv_norm), _pair_columns(b_w_q[0].astype(BF16), False),
                    w_kv, cos, sin)
    o1 = _swa_attention(q1, kv1, b_sinks[0], batch, seq)
    w_o1 = b_w_o[0].astype(BF16).reshape(2, half, HEAD_DIM, d).transpose(1, 0, 2, 3).reshape(d, d)
    wr = jnp.pad(moe_router[0], ((0, 0), (0, LANES - N_EXPERTS))).astype(BF16)
    xr, x_tiles, route, gates, counts = _wo_router(xr, o1, w_o1, row(ffn_norm[1]), wr)

    dest_line, pad_line, tile_expert, n_used, n_tiles = _routing_tables(route, counts)
    xs = _dispatch(x_tiles, dest_line, pad_line, n_tiles * TM)
    ys = _moe_experts(xs, row(ffn_norm[1]), *moe_bf16, tile_expert, n_used, n_tiles)
    xr = _combine_ple(xr, gates, ys, dest_line, pr, 1, row(ple_norm[1]), ple_w_gate[1].astype(BF16),
                      ple_w_proj[1].astype(BF16), row(final_norm))
    return xr.reshape(batch, seq, d)
```

```python
import functools

import numpy as np
import jax
import jax.numpy as jnp
from jax import lax
from jax.experimental import pallas as pl
from jax.experimental.pallas import tpu as pltpu

F32 = jnp.float32
BF16 = jnp.bfloat16

LANES = 128
SUBLANES = 8
MXU_COLS = 256
HEAD_DIM = 64
HALF_DIM = HEAD_DIM // 2
BLOCK = 128
Q_BLOCKS = 2
A_CONFIGS = ((128, 1), (512, 4), (2048, 16))
B_KV_HEADS = 2
B_WINDOW = 128
N_EXPERTS = 8
ROPE_THETA = 10000.0
EPS = 1e-6
NEG_INF = -1e30
VMEM_LIMIT = 56 * 1024 * 1024

TM = 512
TM_COMBINE = 256
DISPATCH_ROWS = 2048
DMA_UNROLL = 8
SWIGLU_SPLITS = 2

PLAIN, ROPE, ROPE_SCALED = 0, 1, 2
LOG2_E = 1.4426950408889634
LN_2 = 0.6931471805599453
Q_SCALE = HEAD_DIM ** -0.5 * LOG2_E


def _params(*sem):
    return pltpu.CompilerParams(dimension_semantics=sem, vmem_limit_bytes=VMEM_LIMIT)


def _rms(x, g):
    ms = jnp.mean(x * x, axis=-1, keepdims=True)
    return x * lax.rsqrt(ms + EPS) * g


def _pair_columns(w, adjacent):
    d, heads = w.shape[0], w.shape[1] // HEAD_DIM
    if adjacent:
        return w.reshape(d, heads // 2, 2, 2, HALF_DIM).transpose(0, 1, 3, 2, 4).reshape(w.shape)
    return w.reshape(d, 2, heads // 2, 2, HALF_DIM).transpose(0, 2, 3, 1, 4).reshape(w.shape)


def _rope_tables(positions):
    pos = jnp.asarray(positions, F32)
    inv = 1.0 / (ROPE_THETA ** (jnp.arange(HALF_DIM, dtype=F32) / HALF_DIM))
    ang = pos[:, None] * inv[None, :]
    cos, sin = jnp.cos(ang), jnp.sin(ang)
    return jnp.tile(cos, (1, 4)), jnp.concatenate([-sin, -sin, sin, sin], axis=1)


def _rope_coeffs(cos, sin):
    a = jnp.stack([cos * Q_SCALE, cos, jnp.ones_like(cos)])
    b = jnp.stack([sin * Q_SCALE, sin, jnp.zeros_like(sin)])
    return a, b


def _store_chunks(acc, cos, sin, o_ref, kinds):
    for c, kind in enumerate(kinds):
        seg = acc[:, c * LANES:(c + 1) * LANES]
        if kind != PLAIN:
            seg = seg * cos + pltpu.roll(seg, LANES // 2, 1) * sin
            if kind == ROPE_SCALED:
                seg = seg * Q_SCALE
        o_ref[:, c * LANES:(c + 1) * LANES] = seg.astype(o_ref.dtype)


def _to_token_tiles(val, ref):
    rows = val.shape[0]
    for c in range(val.shape[1] // LANES):
        ref[pl.ds(c, rows, stride=SUBLANES), :] = val[:, c * LANES:(c + 1) * LANES]


def _from_token_tiles(ref, rows):
    return jnp.concatenate([ref[pl.ds(c, rows, stride=SUBLANES), :] for c in range(SUBLANES)], axis=1)


def _tile_at(ref, line):
    return ref.at[pl.ds(pl.multiple_of(line, SUBLANES), SUBLANES)]


def _group_proj_kernel(x_ref, g_ref, w_ref, a_ref, b_ref, perm_ref, cast_ref, o_ref, cast_out_ref, *, dil):
    cast_out_ref[...] = cast_ref[...].astype(cast_out_ref.dtype)
    h = _rms(x_ref[...], g_ref[...]).astype(BF16)
    if dil > 1:
        h = jnp.dot(perm_ref[...], h, preferred_element_type=F32).astype(BF16)
    sub = h.shape[0] // dil
    tn = w_ref.shape[1] // 3
    for kind in range(3):
        acc = jnp.dot(h, w_ref[:, kind * tn:(kind + 1) * tn], preferred_element_type=F32)
        a, b = a_ref[kind], b_ref[kind]
        for c in range(tn // LANES):
            seg = acc[:, c * LANES:(c + 1) * LANES]
            seg = (seg * a + pltpu.roll(seg, LANES // 2, 1) * b).astype(o_ref.dtype)
            cols = slice(kind * tn + c * LANES, kind * tn + (c + 1) * LANES)
            if dil == 1:
                o_ref[:, cols] = seg
            else:
                for r in range(dil):
                    o_ref[r, :, cols] = seg[r * sub:(r + 1) * sub]


def _residue_major_rows(dil):
    q = np.arange(TM)
    sub = TM // dil
    return (q % sub) * dil + q // sub


def _group_proj(x, g, w, coef_a, coef_b, batch, seq, dil, cast_src):
    t, d = x.shape
    n_out = w.shape[1]
    tiles_per_seq = seq // TM
    steps = t // TM
    cast2d = cast_src.reshape(-1, cast_src.shape[-1])
    slab = cast2d.shape[0] // steps
    assert cast2d.shape[0] % steps == 0 and slab % (2 * SUBLANES) == 0
    cast_spec = pl.BlockSpec((slab, cast2d.shape[1]), lambda i: (i, 0))
    if dil == 1:
        out_spec = pl.BlockSpec((TM, n_out), lambda i: (i, 0))
        out_shape = jax.ShapeDtypeStruct((t, n_out), BF16)
    else:
        out_spec = pl.BlockSpec((None, dil, TM // dil, n_out),
                                lambda i: (i // tiles_per_seq, 0, i % tiles_per_seq, 0))
        out_shape = jax.ShapeDtypeStruct((batch, dil, seq // dil, n_out), BF16)
    nat = _residue_major_rows(dil)
    perm = jnp.asarray(nat[:, None] == np.arange(TM)[None, :], BF16)
    if dil > 1:
        tile_order = lambda c: c.reshape(3, seq // TM, TM // dil, dil, LANES).transpose(0, 1, 3, 2, 4) \
            .reshape(3, seq, LANES)
        coef_a, coef_b = tile_order(coef_a), tile_order(coef_b)
    coef_spec = pl.BlockSpec((3, TM, LANES), lambda i: (0, i % tiles_per_seq, 0))
    qkv, cast = pl.pallas_call(
        functools.partial(_group_proj_kernel, dil=dil),
        grid=(steps,),
        in_specs=[
            pl.BlockSpec((TM, d), lambda i: (i, 0)),
            pl.BlockSpec((1, d), lambda i: (0, 0)),
            pl.BlockSpec(w.shape, lambda i: (0, 0), pipeline_mode=pl.Buffered(1)),
            coef_spec, coef_spec,
            pl.BlockSpec((TM, TM), lambda i: (0, 0), pipeline_mode=pl.Buffered(1)),
            cast_spec,
        ],
        out_specs=[out_spec, cast_spec],
        out_shape=[out_shape, jax.ShapeDtypeStruct(cast2d.shape, BF16)],
        compiler_params=_params("parallel"),
        name=f"group_proj_d{dil}",
    )(x, g, w, coef_a, coef_b, perm, cast2d)
    return qkv, cast.reshape(cast_src.shape)


def _ple_qkv1_kernel(x_ref, p_ref, g_ref, wg_ref, wp_ref, gq_ref, gkv_ref, wq_ref, wkv_ref, cos_ref, sin_ref,
                     x_out_ref, q_ref, kv_ref):
    x = _ple_rows(x_ref[...], p_ref[...], g_ref, wg_ref, wp_ref)
    x_out_ref[...] = x
    xn = x * lax.rsqrt(jnp.mean(x * x, axis=-1, keepdims=True) + EPS)
    hq = (xn * gq_ref[...]).astype(BF16)
    hkv = (xn * gkv_ref[...]).astype(BF16)
    cos, sin = cos_ref[...], sin_ref[...]
    accq = jnp.dot(hq, wq_ref[...], preferred_element_type=F32)
    _store_chunks(accq, cos, sin, q_ref, (ROPE_SCALED,) * (accq.shape[1] // LANES))
    acckv = jnp.dot(hkv, wkv_ref[...], preferred_element_type=F32)
    _store_chunks(acckv, cos, sin, kv_ref, (ROPE, PLAIN))


def _ple_qkv1(x, p, layer, g, wg, wp, gq, gkv, wq, wkv, cos, sin):
    t, d = x.shape
    s_tiles = cos.shape[0] // TM
    row = lambda i: (i, 0)
    fixed = lambda i: (0, 0)
    return pl.pallas_call(
        _ple_qkv1_kernel,
        grid=(t // TM,),
        in_specs=[
            pl.BlockSpec((TM, d), row),
            pl.BlockSpec((None, TM, p.shape[2]), lambda i: (layer, i, 0)),
            pl.BlockSpec((1, d), fixed),
            pl.BlockSpec(wg.shape, fixed),
            pl.BlockSpec(wp.shape, fixed),
            pl.BlockSpec((1, d), fixed),
            pl.BlockSpec((1, d), fixed),
            pl.BlockSpec(wq.shape, fixed),
            pl.BlockSpec(wkv.shape, fixed),
            pl.BlockSpec((TM, LANES), lambda i: (i % s_tiles, 0)),
            pl.BlockSpec((TM, LANES), lambda i: (i % s_tiles, 0)),
        ],
        out_specs=[pl.BlockSpec((TM, d), row), pl.BlockSpec((TM, wq.shape[1]), row),
                   pl.BlockSpec((TM, wkv.shape[1]), row)],
        out_shape=[jax.ShapeDtypeStruct((t, d), F32),
                   jax.ShapeDtypeStruct((t, wq.shape[1]), BF16),
                   jax.ShapeDtypeStruct((t, wkv.shape[1]), BF16)],
        compiler_params=_params("parallel"),
        name="ple_qkv_layer1",
    )(x, p, g, wg, wp, gq, gkv, wq, wkv, cos, sin)


def _band_mask(has_prev, max_dist):
    qi = lax.broadcasted_iota(jnp.int32, (BLOCK, 2 * BLOCK), 0) + BLOCK
    kj = lax.broadcasted_iota(jnp.int32, (BLOCK, 2 * BLOCK), 1)
    rel = qi - kj
    mask = (rel >= 0) & (rel <= max_dist)
    if has_prev is not True:
        mask = mask & ((kj >= BLOCK) | has_prev)
    return jnp.concatenate([mask, mask], axis=0)


def _band_blocks(qb, prev_ref, cur_ref, lanes):
    own = cur_ref[qb * BLOCK:(qb + 1) * BLOCK, lanes]
    before = prev_ref[:, lanes] if qb == 0 else cur_ref[(qb - 1) * BLOCK:qb * BLOCK, lanes]
    return jnp.concatenate([before, own], axis=0)


def _lane_masks():
    lane = lax.broadcasted_iota(jnp.int32, (BLOCK, LANES), 1)
    first_qk = (lane % HEAD_DIM) < HALF_DIM
    sel_a = jnp.where(first_qk, 1.0, 0.0).astype(BF16)
    sel_b = jnp.where(first_qk, 0.0, 1.0).astype(BF16)
    return lane, sel_a, sel_b, lane < HEAD_DIM


def _pair_scores(q2, kcat, sel_a, sel_b):
    qs = jnp.concatenate([q2 * sel_a, q2 * sel_b], axis=0)
    return lax.dot_general(qs, kcat, (((1,), (1,)), ((), ())), preferred_element_type=F32)


def _dil_attn_kernel(q_ref, kp_ref, kc_ref, vp_ref, vc_ref, o_ref, st_ref, *, steps):
    lane, sel_a, sel_b, first_v = _lane_masks()
    for qb in range(Q_BLOCKS):
        rows = slice(qb * BLOCK, (qb + 1) * BLOCK)
        mask2 = _band_mask(True if qb else pl.program_id(2) > 0, steps)
        stats = jnp.zeros((BLOCK, LANES), F32)
        for p in range(q_ref.shape[1] // LANES):
            sl = slice(p * LANES, (p + 1) * LANES)
            kcat = _band_blocks(qb, kp_ref, kc_ref, sl)
            vcat = _band_blocks(qb, vp_ref, vc_ref, sl)
            s = jnp.where(mask2, _pair_scores(q_ref[rows, sl], kcat, sel_a, sel_b), NEG_INF)
            m = jnp.max(s, axis=-1, keepdims=True)
            pe = jnp.exp2(s - m)
            den = jnp.sum(pe, axis=-1, keepdims=True)
            pv = jnp.dot(pe.astype(BF16), vcat, preferred_element_type=F32)
            pv = pv * (1.0 / den)
            o_ref[rows, sl] = jnp.where(first_v, pv[:BLOCK], pv[BLOCK:]).astype(o_ref.dtype)
            lse = m * LN_2 + jnp.log(den)
            stats = jnp.where(lane == 2 * p, lse[:BLOCK], stats)
            stats = jnp.where(lane == 2 * p + 1, lse[BLOCK:], stats)
        st_ref[rows, :] = stats


def _dilated_group(qkv, batch, seq, g):
    win, dil = A_CONFIGS[g]
    n = seq // dil
    d = qkv.shape[-1] // 3
    view = qkv.reshape(batch, dil, n, 3 * d)

    rows = Q_BLOCKS * BLOCK

    def spec(kind, prev):
        if prev:
            return pl.BlockSpec((None, None, BLOCK, d),
                                lambda b, r, blk: (b, r, jnp.maximum(blk * Q_BLOCKS - 1, 0), kind))
        return pl.BlockSpec((None, None, rows, d), lambda b, r, blk: (b, r, blk, kind))

    return pl.pallas_call(
        functools.partial(_dil_attn_kernel, steps=win // dil),
        grid=(batch, dil, n // rows),
        in_specs=[spec(0, False), spec(1, True), spec(1, False), spec(2, True), spec(2, False)],
        out_specs=[pl.BlockSpec((None, None, rows, d), lambda b, r, blk: (b, r, blk, 0)),
                   pl.BlockSpec((None, None, rows, LANES), lambda b, r, blk: (b, r, blk, 0))],
        out_shape=[jax.ShapeDtypeStruct((batch, dil, n, d), BF16),
                   jax.ShapeDtypeStruct((batch, dil, n, LANES), F32)],
        compiler_params=_params("parallel", "parallel", "arbitrary"),
        name=f"dilated_attn_g{g}",
    )(view, view, view, view, view)


def _swa_kernel(fill_ref, q_ref, kp_ref, kc_ref, vp_ref, vc_ref, o_ref):
    _, sel_a, sel_b, first_v = _lane_masks()
    all_lanes = slice(0, LANES)
    key_row = lax.broadcasted_iota(jnp.int32, (2 * BLOCK, LANES), 0)
    for qb in range(Q_BLOCKS):
        rows = slice(qb * BLOCK, (qb + 1) * BLOCK)
        mask2 = _band_mask(True if qb else pl.program_id(1) > 0, B_WINDOW - 1)
        kcat = _band_blocks(qb, kp_ref, kc_ref, all_lanes)
        vcat = _band_blocks(qb, vp_ref, vc_ref, all_lanes)
        vcat = jnp.where(key_row == 0, jnp.zeros_like(vcat), vcat)
        for j in range(q_ref.shape[1] // LANES):
            sl = slice(j * LANES, (j + 1) * LANES)
            s = jnp.where(mask2, _pair_scores(q_ref[rows, sl], kcat, sel_a, sel_b), fill_ref[j])
            m = jnp.max(s, axis=-1, keepdims=True)
            pe = jnp.exp2(s - m)
            den = jnp.sum(pe, axis=-1, keepdims=True)
            pv = jnp.dot(pe.astype(BF16), vcat, preferred_element_type=F32)
            pv = pv * (1.0 / den)
            o_ref[rows, sl] = jnp.where(first_v, pv[:BLOCK], pv[BLOCK:]).astype(o_ref.dtype)


def _swa_attention(q, kv, sinks, batch, seq):
    d = q.shape[1]
    rows = Q_BLOCKS * BLOCK
    qv = q.reshape(batch, seq, d)
    kvv = kv.reshape(batch, seq, 2 * LANES)

    def kv_spec(col, prev):
        if prev:
            return pl.BlockSpec((None, BLOCK, LANES),
                                lambda b, blk: (b, jnp.maximum(blk * Q_BLOCKS - 1, 0), col))
        return pl.BlockSpec((None, rows, LANES), lambda b, blk: (b, blk, col))

    half = sinks.shape[0] // 2
    sink_rows = jnp.repeat(jnp.stack([sinks[:half], sinks[half:]], axis=1), BLOCK, axis=1)
    first_key = jnp.arange(2 * BLOCK)[None, None, :] == 0
    fill = jnp.where(first_key, sink_rows[:, :, None].astype(F32) * LOG2_E, NEG_INF)

    o = pl.pallas_call(
        _swa_kernel,
        grid=(batch, seq // rows),
        in_specs=[pl.BlockSpec(fill.shape, lambda b, blk: (0, 0, 0)),
                  pl.BlockSpec((None, rows, d), lambda b, blk: (b, blk, 0)),
                  kv_spec(0, True), kv_spec(0, False), kv_spec(1, True), kv_spec(1, False)],
        out_specs=pl.BlockSpec((None, rows, d), lambda b, blk: (b, blk, 0)),
        out_shape=jax.ShapeDtypeStruct((batch, seq, d), BF16),
        compiler_params=_params("parallel", "arbitrary"),
        name="swa_sink_attn",
    )(fill, qv, kvv, kvv, kvv, kvv)
    return o.reshape(batch * seq, d)


def _interleave(src_ref, dst_ref):
    dil, sub = src_ref.shape[0], src_ref.shape[1]
    chunks = dst_ref.shape[0]
    for r in range(dil):
        for c in range(chunks):
            part = src_ref[r, :, c * LANES:(c + 1) * LANES].astype(F32)
            dst_ref[c, pl.ds(r, sub, stride=dil), :] = part
    return jnp.concatenate([dst_ref[c] for c in range(chunks)], axis=1)


def _merge_ffn_kernel(x_ref, o0_ref, o1_ref, o2_ref, s0_ref, s1_ref, s2_ref, e_ref, w_ref, g_ref, w1_ref,
                      w3_ref, w2_ref, out_ref, ob_ref, sb1_ref, sb2_ref):
    s0, s1, s2 = s0_ref[0], _interleave(s1_ref, sb1_ref), _interleave(s2_ref, sb2_ref)
    top = jnp.maximum(jnp.maximum(s0, s1), s2)
    e0, e1, e2 = jnp.exp(s0 - top), jnp.exp(s1 - top), jnp.exp(s2 - top)
    inv = 1.0 / (e0 + e1 + e2)
    expand = e_ref[...]

    def weight(e):
        return jnp.dot((e * inv).astype(BF16), expand, preferred_element_type=F32)

    merged = weight(e0) * o0_ref[0].astype(F32)
    merged = merged + weight(e1) * _interleave(o1_ref, ob_ref)
    merged = merged + weight(e2) * _interleave(o2_ref, ob_ref)
    x = x_ref[...] + jnp.dot(merged.astype(BF16), w_ref[...], preferred_element_type=F32)
    out_ref[...] = x + _swiglu(_rms(x, g_ref[...]).astype(BF16), w1_ref, w3_ref, w2_ref)


def _merge_ffn(x, outs, stats, expand, w, g, w1, w3, w2, batch, seq):
    t, d = x.shape
    tiles_per_seq = seq // TM
    row = lambda i: (i, 0)
    fixed = lambda i: (0, 0)
    resident = lambda a: pl.BlockSpec(a.shape, fixed, pipeline_mode=pl.Buffered(1))

    def sub(arr):
        dil = arr.shape[1]
        return pl.BlockSpec((None, dil, TM // dil, arr.shape[3]),
                            lambda i: (i // tiles_per_seq, 0, i % tiles_per_seq, 0))

    return pl.pallas_call(
        _merge_ffn_kernel,
        grid=(t // TM,),
        in_specs=[pl.BlockSpec((TM, d), row)]
        + [sub(o) for o in outs] + [sub(s) for s in stats]
        + [resident(expand), resident(w), pl.BlockSpec((1, d), fixed),
           resident(w1), resident(w3), resident(w2)],
        out_specs=pl.BlockSpec((TM, d), row),
        out_shape=jax.ShapeDtypeStruct((t, d), F32),
        scratch_shapes=[pltpu.VMEM((d // LANES, TM, LANES), F32), pltpu.VMEM((1, TM, LANES), F32),
                        pltpu.VMEM((1, TM, LANES), F32)],
        compiler_params=_params("parallel"),
        name="merge_wo_swiglu",
    )(x, *outs, *stats, expand, w, g, w1, w3, w2)


def _wo_router_kernel(x_ref, o_ref, w_ref, g_ref, wr_ref, tri_ref, x1_ref, x1t_ref, route_ref, gates_ref,
                      counts_ref, run_ref):
    @pl.when(pl.program_id(0) == 0)
    def _():
        run_ref[...] = jnp.zeros_like(run_ref)

    x1 = x_ref[...] + jnp.dot(o_ref[...], w_ref[...], preferred_element_type=F32)
    x1_ref[...] = x1
    _to_token_tiles(x1, x1t_ref)
    h = _rms(x1, g_ref[...]).astype(BF16)
    logits = jnp.dot(h, wr_ref[...], preferred_element_type=F32)
    lane = lax.broadcasted_iota(jnp.int32, logits.shape, 1)
    logits = jnp.where(lane < N_EXPERTS, logits, -jnp.inf)
    v1 = jnp.max(logits, axis=-1, keepdims=True)
    i1 = jnp.min(jnp.where(logits == v1, lane, LANES), axis=-1, keepdims=True)
    rest = jnp.where(lane == i1, -jnp.inf, logits)
    v2 = jnp.max(rest, axis=-1, keepdims=True)
    i2 = jnp.min(jnp.where(rest == v2, lane, LANES), axis=-1, keepdims=True)
    e2 = jnp.exp(v2 - v1)
    inv = 1.0 / (1.0 + e2)
    gates_ref[:, :LANES] = jnp.broadcast_to(inv, logits.shape)
    gates_ref[:, LANES:] = jnp.broadcast_to(e2 * inv, logits.shape)
    picked = jnp.where(lane == i1, 1.0, 0.0) + jnp.where(lane == i2, 1.0, 0.0)
    before = run_ref[0:1, :] + jnp.dot(tri_ref[...], picked.astype(BF16), preferred_element_type=F32)
    rank1 = jnp.sum(jnp.where(lane == i1, before, 0.0), axis=-1, keepdims=True)
    rank2 = jnp.sum(jnp.where(lane == i2, before, 0.0), axis=-1, keepdims=True)
    total = run_ref[0:1, :] + jnp.sum(picked, axis=0, keepdims=True)
    run_ref[0:1, :] = total
    counts_ref[...] = jnp.broadcast_to(total, counts_ref.shape)
    info = jnp.where(lane == 0, i1.astype(F32), 0.0)
    info = jnp.where(lane == 1, i2.astype(F32), info)
    info = jnp.where(lane == 2, rank1, info)
    info = jnp.where(lane == 3, rank2, info)
    route_ref[...] = info.T[:SUBLANES, :]


def _wo_router(x, o, w, g, wr):
    t, d = x.shape
    row = lambda i: (i, 0)
    fixed = lambda i: (0, 0)
    tri = jnp.asarray(np.tril(np.ones((TM, TM), np.float32), -1), BF16)
    return pl.pallas_call(
        _wo_router_kernel,
        grid=(t // TM,),
        in_specs=[pl.BlockSpec((TM, d), row), pl.BlockSpec((TM, o.shape[1]), row),
                  pl.BlockSpec(w.shape, fixed), pl.BlockSpec((1, d), fixed),
                  pl.BlockSpec(wr.shape, fixed), pl.BlockSpec(tri.shape, fixed)],
        out_specs=[pl.BlockSpec((TM, d), row), pl.BlockSpec((TM * SUBLANES, LANES), row),
                   pl.BlockSpec((SUBLANES, TM), lambda i: (0, i)), pl.BlockSpec((TM, 2 * LANES), row),
                   pl.BlockSpec((SUBLANES, LANES), fixed)],
        out_shape=[jax.ShapeDtypeStruct((t, d), F32),
                   jax.ShapeDtypeStruct((t * SUBLANES, LANES), F32),
                   jax.ShapeDtypeStruct((SUBLANES, t), F32),
                   jax.ShapeDtypeStruct((t, 2 * LANES), F32),
                   jax.ShapeDtypeStruct((SUBLANES, LANES), F32)],
        scratch_shapes=[pltpu.VMEM((SUBLANES, LANES), F32)],
        compiler_params=_params("arbitrary"),
        name="wo_router",
    )(x, o, w, g, wr, tri)


def _swiglu(h, w1_ref, w3_ref, w2_ref):
    width = w1_ref.shape[1] // SWIGLU_SPLITS
    y = None
    for c in range(SWIGLU_SPLITS):
        sl = slice(c * width, (c + 1) * width)
        a = jnp.dot(h, w1_ref[:, sl], preferred_element_type=F32)
        b = jnp.dot(h, w3_ref[:, sl], preferred_element_type=F32)
        act = (a * jax.nn.sigmoid(a) * b).astype(BF16)
        part = jnp.dot(act, w2_ref[sl, :], preferred_element_type=F32)
        y = part if y is None else y + part
    return y


def _moe_kernel(te_ref, nu_ref, xs_ref, g_ref, w1_ref, w3_ref, w2_ref, o_ref):
    used = pl.program_id(0) < nu_ref[0]

    @pl.when(used)
    def _():
        x = _from_token_tiles(xs_ref, xs_ref.shape[0] // SUBLANES)
        _to_token_tiles(_swiglu(_rms(x, g_ref[...]).astype(BF16), w1_ref, w3_ref, w2_ref), o_ref)

    @pl.when(jnp.logical_not(used))
    def _():
        o_ref[...] = jnp.zeros_like(o_ref)


def _moe_experts(xs, g, w1, w3, w2, tile_expert, n_used, n_tiles):
    d, f = w1.shape[1], w1.shape[2]
    lines = TM * SUBLANES
    resident = pl.Buffered(1)
    grid_spec = pltpu.PrefetchScalarGridSpec(
        num_scalar_prefetch=2,
        grid=(n_tiles,),
        in_specs=[pl.BlockSpec((lines, LANES), lambda i, te, nu: (i, 0)),
                  pl.BlockSpec((1, d), lambda i, te, nu: (0, 0)),
                  pl.BlockSpec((None, d, f), lambda i, te, nu: (te[i], 0, 0), pipeline_mode=resident),
                  pl.BlockSpec((None, d, f), lambda i, te, nu: (te[i], 0, 0), pipeline_mode=resident),
                  pl.BlockSpec((None, f, d), lambda i, te, nu: (te[i], 0, 0), pipeline_mode=resident)],
        out_specs=pl.BlockSpec((lines, LANES), lambda i, te, nu: (i, 0)),
    )
    return pl.pallas_call(
        _moe_kernel,
        grid_spec=grid_spec,
        out_shape=jax.ShapeDtypeStruct((n_tiles * lines, LANES), F32),
        compiler_params=_params("arbitrary"),
        name="moe_experts",
    )(tile_expert, n_used, xs, g, w1, w3, w2)


def _dispatch_kernel(dst0_ref, dst1_ref, pad_ref, src_ref, zero_ref, out_hbm, sem):
    rows = dst0_ref.shape[2]
    n_pad = pad_ref.shape[0]

    def issue(r, carry):
        base = r * DMA_UNROLL
        idx = [[ref[0, 0, base + q] for q in range(DMA_UNROLL)] for ref in (dst0_ref, dst1_ref)]
        for u in range(DMA_UNROLL):
            src = _tile_at(src_ref, (base + u) * SUBLANES)
            for k in range(2):
                pltpu.make_async_copy(src, _tile_at(out_hbm, idx[k][u]), sem).start(priority=k)
        return carry

    lax.fori_loop(0, rows // DMA_UNROLL, issue, 0)
    lines = 2 * rows * SUBLANES
    pltpu.make_async_copy(out_hbm.at[pl.ds(0, lines)], out_hbm.at[pl.ds(0, lines)], sem).wait()

    @pl.when(pl.program_id(0) == 0)
    def _():
        def fill(r, carry):
            base = r * DMA_UNROLL
            idx = [pad_ref[base + q] for q in range(DMA_UNROLL)]
            for q in range(DMA_UNROLL):
                pltpu.make_async_copy(zero_ref, _tile_at(out_hbm, idx[q]), sem).start()
            return carry

        lax.fori_loop(0, n_pad // DMA_UNROLL, fill, 0)
        pad_lines = n_pad * SUBLANES
        pltpu.make_async_copy(out_hbm.at[pl.ds(0, pad_lines)], out_hbm.at[pl.ds(0, pad_lines)],
                              sem).wait()


def _dispatch(x_tiles, dest_line, pad_line, n_slots):
    t = x_tiles.shape[0] // SUBLANES
    steps = t // DISPATCH_ROWS
    zero = jnp.zeros((SUBLANES, LANES), x_tiles.dtype)
    idx = dest_line.reshape(2 * steps, 1, DISPATCH_ROWS)
    idx_spec = lambda pick: pl.BlockSpec((1, 1, DISPATCH_ROWS), lambda i: (pick * steps + i, 0, 0),
                                         memory_space=pltpu.SMEM)
    return pl.pallas_call(
        _dispatch_kernel,
        grid=(steps,),
        in_specs=[idx_spec(0), idx_spec(1),
                  pl.BlockSpec(memory_space=pltpu.SMEM),
                  pl.BlockSpec((DISPATCH_ROWS * SUBLANES, LANES), lambda i: (i, 0)),
                  pl.BlockSpec((SUBLANES, LANES), lambda i: (0, 0))],
        out_specs=pl.BlockSpec(memory_space=pl.ANY),
        out_shape=jax.ShapeDtypeStruct((n_slots * SUBLANES, LANES), x_tiles.dtype),
        scratch_shapes=[pltpu.SemaphoreType.DMA(())],
        compiler_params=_params("arbitrary"),
        name="moe_dispatch",
    )(idx, idx, pad_line, x_tiles, zero)


def _ple_rows(x, p, g_ref, wg_ref, wp_ref, between=None):
    h = _rms(x, g_ref[...]).astype(BF16)
    gate = []
    for c in range(wg_ref.shape[1] // MXU_COLS):
        if between is not None:
            between(c)
        logits = jnp.dot(h, wg_ref[:, c * MXU_COLS:(c + 1) * MXU_COLS], preferred_element_type=F32)
        gate.append(jax.nn.sigmoid(logits))
    proj = jnp.dot(p.astype(BF16), wp_ref[...], preferred_element_type=F32)
    return x + proj * jnp.concatenate(gate, axis=1)


def _combine_ple_kernel(cur0_ref, cur1_ref, nxt0_ref, nxt1_ref, x_ref, gates_ref, p_ref, g_ref, wg_ref,
                        wp_ref, fg_ref, ys_hbm, o_ref, a_ref, b_ref, sem):
    i = pl.program_id(0)
    last = pl.num_programs(0) - 1
    tm = a_ref.shape[1] // SUBLANES
    groups = tm // DMA_UNROLL
    cur_ref, nxt_ref = (cur0_ref, cur1_ref), (nxt0_ref, nxt1_ref)

    def issue_group(idx_ref, offset, slot, base):
        idx = [[ref[0, 0, offset + base + q] for q in range(DMA_UNROLL)] for ref in idx_ref]
        for u in range(DMA_UNROLL):
            line = (base + u) * SUBLANES
            pltpu.make_async_copy(_tile_at(ys_hbm, idx[0][u]), _tile_at(a_ref.at[slot], line),
                                  sem.at[slot]).start(priority=0)
            pltpu.make_async_copy(_tile_at(ys_hbm, idx[1][u]), _tile_at(b_ref.at[slot], line),
                                  sem.at[slot]).start(priority=1)

    def wait(slot):
        whole = ys_hbm.at[pl.ds(0, tm * SUBLANES)]
        pltpu.make_async_copy(whole, a_ref.at[slot], sem.at[slot]).wait()
        pltpu.make_async_copy(whole, b_ref.at[slot], sem.at[slot]).wait()

    def compute(slot, idx_ref, offset):
        rows = slice(slot * tm, (slot + 1) * tm)
        g1, g2 = gates_ref[rows, :LANES], gates_ref[rows, LANES:]
        y = jnp.concatenate(
            [g1 * a_ref[slot, pl.ds(c, tm, stride=SUBLANES), :]
             + g2 * b_ref[slot, pl.ds(c, tm, stride=SUBLANES), :] for c in range(SUBLANES)], axis=1)
        chunks = wg_ref.shape[1] // MXU_COLS

        def between(c):
            for grp in range(c * groups // chunks, (c + 1) * groups // chunks):
                issue_group(idx_ref, offset, 1 - slot, grp * DMA_UNROLL)

        x = _ple_rows(x_ref[rows, :] + y, p_ref[rows, :], g_ref, wg_ref, wp_ref, between)
        o_ref[rows, :] = _rms(x, fg_ref[...])

    @pl.when(i == 0)
    def _():
        lax.fori_loop(0, groups, lambda r, c: (issue_group(cur_ref, 0, 0, r * DMA_UNROLL), c)[1], 0)

    wait(0)
    compute(0, cur_ref, tm)
    wait(1)
    compute(1, nxt_ref, 0)

    @pl.when(i == last)
    def _():
        wait(0)


def _combine_ple(x, gates, ys, dest_line, p, layer, g, wg, wp, fg):
    t, d = x.shape
    tm = TM_COMBINE
    steps = t // (2 * tm)
    row = lambda i: (i, 0)
    fixed = lambda i: (0, 0)
    idx = dest_line.reshape(2 * steps, 1, 2 * tm)
    idx_spec = lambda index: pl.BlockSpec((1, 1, 2 * tm), index, memory_space=pltpu.SMEM)
    nxt = lambda i: jnp.minimum(i + 1, steps - 1)
    return pl.pallas_call(
        _combine_ple_kernel,
        grid=(steps,),
        in_specs=[idx_spec(lambda i: (i, 0, 0)), idx_spec(lambda i: (steps + i, 0, 0)),
                  idx_spec(lambda i: (nxt(i), 0, 0)), idx_spec(lambda i: (steps + nxt(i), 0, 0)),
                  pl.BlockSpec((2 * tm, d), row),
                  pl.BlockSpec((2 * tm, 2 * LANES), row),
                  pl.BlockSpec((None, 2 * tm, p.shape[2]), lambda i: (layer, i, 0)),
                  pl.BlockSpec((1, d), fixed),
                  pl.BlockSpec(wg.shape, fixed),
                  pl.BlockSpec(wp.shape, fixed),
                  pl.BlockSpec((1, d), fixed),
                  pl.BlockSpec(memory_space=pl.ANY)],
        out_specs=pl.BlockSpec((2 * tm, d), row),
        out_shape=jax.ShapeDtypeStruct((t, d), F32),
        scratch_shapes=[pltpu.VMEM((2, tm * SUBLANES, LANES), F32),
                        pltpu.VMEM((2, tm * SUBLANES, LANES), F32),
                        pltpu.SemaphoreType.DMA((2,))],
        compiler_params=_params("arbitrary"),
        name="combine_ple_final",
    )(idx, idx, idx, idx, x, gates, p, g, wg, wp, fg, ys)


def _routing_tables(route, counts_f):
    t = route.shape[1]
    picked = route[0:2].astype(jnp.int32)
    rank = route[2:4].astype(jnp.int32)
    counts = counts_f[0, :N_EXPERTS].astype(jnp.int32)
    tiles_per = (counts + TM - 1) // TM
    tile_end = jnp.cumsum(tiles_per)
    tile_start = tile_end - tiles_per
    first_slot = jnp.zeros_like(picked)
    for e in range(N_EXPERTS):
        first_slot = jnp.where(picked == e, tile_start[e] * TM, first_slot)
    dest = first_slot + rank
    n_tiles = (2 * t) // TM + N_EXPERTS
    n_used = tile_end[-1]
    tile_ids = jnp.minimum(jnp.arange(n_tiles), n_used - 1)
    tile_expert = jnp.minimum(jnp.sum((tile_ids[:, None] >= tile_end[None, :]).astype(jnp.int32), axis=1),
                              N_EXPERTS - 1)
    pad_sizes = jnp.concatenate([tiles_per * TM - counts, ((n_tiles - n_used) * TM).reshape(1)])
    pad_end = jnp.cumsum(pad_sizes)
    seg_first = jnp.concatenate([tile_start * TM + counts, (n_used * TM).reshape(1)])
    k = jnp.arange(N_EXPERTS * TM)
    seg = jnp.sum((k[:, None] >= pad_end[None, :]).astype(jnp.int32), axis=1)
    pad = seg_first[seg] + k - (pad_end - pad_sizes)[seg]
    to_line = lambda v: (v * SUBLANES).astype(jnp.int32)
    return (to_line(dest), to_line(pad), tile_expert.astype(jnp.int32),
            n_used.reshape(1).astype(jnp.int32), n_tiles)


def kernel(x, p, attn_norm, ffn_norm, a_w_qkv, a_w_o, kv_norm, kv_w, b_w_q, b_sinks, b_w_o,
           dense_w1, dense_w3, dense_w2, moe_router, moe_w1, moe_w3, moe_w2,
           ple_norm, ple_w_gate, ple_w_proj, final_norm):
    batch, seq, d = x.shape
    t = batch * seq
    n_heads = d // HEAD_DIM
    n_groups = len(A_CONFIGS)
    assert d == SUBLANES * LANES and seq % TM == 0 and t % DISPATCH_ROWS == 0
    xr = x.reshape(t, d)
    pr = p.reshape(p.shape[0], t, p.shape[3])
    cos, sin = _rope_tables(np.arange(seq))
    row = lambda v: v.reshape(1, -1)

    w = a_w_qkv[0].astype(BF16).reshape(d, n_groups, 3, d)
    coef_a, coef_b = _rope_coeffs(cos, sin)
    outs, stats, moe_bf16 = [], [], []
    for g, (_, dil) in enumerate(A_CONFIGS):
        w_g = jnp.concatenate([_pair_columns(w[:, g, 0], True), _pair_columns(w[:, g, 1], True),
                               w[:, g, 2]], axis=1)
        qkv, cast = _group_proj(xr, row(attn_norm[0]), w_g, coef_a, coef_b, batch, seq, dil,
                                (moe_w1, moe_w3, moe_w2)[g][0])
        moe_bf16.append(cast)
        o_g, st_g = _dilated_group(qkv, batch, seq, g)
        outs.append(o_g)
        stats.append(st_g)
    head_of_col = jnp.arange(d) // HEAD_DIM
    expand = (jnp.arange(LANES)[:, None] == head_of_col[None, :]).astype(BF16)
    xr = _merge_ffn(xr, outs, stats, expand, a_w_o[0].astype(BF16), row(ffn_norm[0]),
                    dense_w1[0].astype(BF16), dense_w3[0].astype(BF16), dense_w2[0].astype(BF16),
                    batch, seq)

    half = n_heads // 2
    kv_cols = B_KV_HEADS * HEAD_DIM
    kv_bf = kv_w.astype(BF16)
    w_kv = jnp.concatenate([_pair_columns(kv_bf[:, :kv_cols], True), kv_bf[:, kv_cols:]], axis=1)
    xr, q1, kv1 = _ple_qkv1(xr, pr, 0, row(ple_norm[0]), ple_w_gate[0].astype(BF16),
                            ple_w_proj[0].astype(BF16), row(attn_norm[1]), row(kv_norm),
                            _pair_columns(b_w_q[0].astype(BF16), False), w_kv, cos, sin)
    o1 = _swa_attention(q1, kv1, b_sinks[0], batch, seq)
    w_o1 = b_w_o[0].astype(BF16).reshape(2, half, HEAD_DIM, d).transpose(1, 0, 2, 3).reshape(d, d)
    wr = jnp.pad(moe_router[0], ((0, 0), (0, LANES - N_EXPERTS))).astype(BF16)
    xr, x_tiles, route, gates, counts = _wo_router(xr, o1, w_o1, row(ffn_norm[1]), wr)

    dest_line, pad_line, tile_expert, n_used, n_tiles = _routing_tables(route, counts)
    xs = _dispatch(x_tiles, dest_line, pad_line, n_tiles * TM)
    ys = _moe_experts(xs, row(ffn_norm[1]), *moe_bf16, tile_expert, n_used, n_tiles)
    xr = _combine_ple(xr, gates, ys, dest_line, pr, 1, row(ple_norm[1]), ple_w_gate[1].astype(BF16),
                      ple_w_proj[1].astype(BF16), row(final_norm))
    return xr.reshape(batch, seq, d)
```

```python
import functools

import numpy as np
import jax
import jax.numpy as jnp
from jax import lax
from jax.experimental import pallas as pl
from jax.experimental.pallas import tpu as pltpu

F32 = jnp.float32
BF16 = jnp.bfloat16

LANES = 128
SUBLANES = 8
MXU_COLS = 256
HEAD_DIM = 64
HALF_DIM = HEAD_DIM // 2
BLOCK = 128
Q_BLOCKS = 4
A_CONFIGS = ((128, 1), (512, 4), (2048, 16))
B_KV_HEADS = 2
B_WINDOW = 128
N_EXPERTS = 8
ROPE_THETA = 10000.0
EPS = 1e-6
NEG_INF = -1e30
VMEM_LIMIT = 56 * 1024 * 1024

TM = 512
TM_COMBINE = 256
DISPATCH_ROWS = 2048
DMA_UNROLL = 8
SWIGLU_SPLITS = 2

PLAIN, ROPE, ROPE_SCALED = 0, 1, 2
LOG2_E = 1.4426950408889634
LN_2 = 0.6931471805599453
Q_SCALE = HEAD_DIM ** -0.5 * LOG2_E


def _params(*sem):
    return pltpu.CompilerParams(dimension_semantics=sem, vmem_limit_bytes=VMEM_LIMIT)


def _rms(x, g):
    ms = jnp.mean(x * x, axis=-1, keepdims=True)
    return x * lax.rsqrt(ms + EPS) * g


def _pair_columns(w, adjacent):
    d, heads = w.shape[0], w.shape[1] // HEAD_DIM
    if adjacent:
        return w.reshape(d, heads // 2, 2, 2, HALF_DIM).transpose(0, 1, 3, 2, 4).reshape(w.shape)
    return w.reshape(d, 2, heads // 2, 2, HALF_DIM).transpose(0, 2, 3, 1, 4).reshape(w.shape)


def _rope_tables(positions):
    pos = jnp.asarray(positions, F32)
    inv = 1.0 / (ROPE_THETA ** (jnp.arange(HALF_DIM, dtype=F32) / HALF_DIM))
    ang = pos[:, None] * inv[None, :]
    cos, sin = jnp.cos(ang), jnp.sin(ang)
    return jnp.tile(cos, (1, 4)), jnp.concatenate([-sin, -sin, sin, sin], axis=1)


def _rope_coeffs(cos, sin):
    a = jnp.stack([cos * Q_SCALE, cos, jnp.ones_like(cos)])
    b = jnp.stack([sin * Q_SCALE, sin, jnp.zeros_like(sin)])
    return a, b


def _store_chunks(acc, cos, sin, o_ref, kinds):
    for c, kind in enumerate(kinds):
        seg = acc[:, c * LANES:(c + 1) * LANES]
        if kind != PLAIN:
            seg = seg * cos + pltpu.roll(seg, LANES // 2, 1) * sin
            if kind == ROPE_SCALED:
                seg = seg * Q_SCALE
        o_ref[:, c * LANES:(c + 1) * LANES] = seg.astype(o_ref.dtype)


def _to_token_tiles(val, ref):
    rows = val.shape[0]
    for c in range(val.shape[1] // LANES):
        ref[pl.ds(c, rows, stride=SUBLANES), :] = val[:, c * LANES:(c + 1) * LANES]


def _from_token_tiles(ref, rows):
    return jnp.concatenate([ref[pl.ds(c, rows, stride=SUBLANES), :] for c in range(SUBLANES)], axis=1)


def _tile_at(ref, line):
    return ref.at[pl.ds(pl.multiple_of(line, SUBLANES), SUBLANES)]


def _group_proj_kernel(x_ref, g_ref, w_ref, a_ref, b_ref, perm_ref, *refs, dil):
    n_cast = (len(refs) - 1) // 2
    cast_refs, o_ref, cast_out_refs = refs[:n_cast], refs[n_cast], refs[n_cast + 1:]
    for src_ref, dst_ref in zip(cast_refs, cast_out_refs):
        dst_ref[...] = src_ref[...].astype(dst_ref.dtype)
    h = _rms(x_ref[...], g_ref[...]).astype(BF16)
    if dil > 1:
        h = jnp.dot(perm_ref[...], h, preferred_element_type=F32).astype(BF16)
    sub = h.shape[0] // dil
    tn = w_ref.shape[1] // 3
    for kind in range(3):
        acc = jnp.dot(h, w_ref[:, kind * tn:(kind + 1) * tn], preferred_element_type=F32)
        a, b = a_ref[kind], b_ref[kind]
        for c in range(tn // LANES):
            seg = acc[:, c * LANES:(c + 1) * LANES]
            seg = (seg * a + pltpu.roll(seg, LANES // 2, 1) * b).astype(o_ref.dtype)
            cols = slice(kind * tn + c * LANES, kind * tn + (c + 1) * LANES)
            if dil == 1:
                o_ref[:, cols] = seg
            else:
                for r in range(dil):
                    o_ref[r, :, cols] = seg[r * sub:(r + 1) * sub]


def _residue_major_rows(dil):
    q = np.arange(TM)
    sub = TM // dil
    return (q % sub) * dil + q // sub


def _group_proj(x, g, w, coef_a, coef_b, batch, seq, dil, cast_srcs):
    t, d = x.shape
    n_out = w.shape[1]
    tiles_per_seq = seq // TM
    steps = t // TM
    cast2d = [c.reshape(-1, c.shape[-1]) for c in cast_srcs]
    for c in cast2d:
        assert c.shape[0] % (steps * 2 * SUBLANES) == 0, c.shape
    cast_specs = [pl.BlockSpec((c.shape[0] // steps, c.shape[1]), lambda i: (i, 0)) for c in cast2d]
    if dil == 1:
        out_spec = pl.BlockSpec((TM, n_out), lambda i: (i, 0))
        out_shape = jax.ShapeDtypeStruct((t, n_out), BF16)
    else:
        out_spec = pl.BlockSpec((None, dil, TM // dil, n_out),
                                lambda i: (i // tiles_per_seq, 0, i % tiles_per_seq, 0))
        out_shape = jax.ShapeDtypeStruct((batch, dil, seq // dil, n_out), BF16)
    nat = _residue_major_rows(dil)
    perm = jnp.asarray(nat[:, None] == np.arange(TM)[None, :], BF16)
    if dil > 1:
        tile_order = lambda c: c.reshape(3, seq // TM, TM // dil, dil, LANES).transpose(0, 1, 3, 2, 4) \
            .reshape(3, seq, LANES)
        coef_a, coef_b = tile_order(coef_a), tile_order(coef_b)
    coef_spec = pl.BlockSpec((3, TM, LANES), lambda i: (0, i % tiles_per_seq, 0))
    qkv, *casts = pl.pallas_call(
        functools.partial(_group_proj_kernel, dil=dil),
        grid=(steps,),
        in_specs=[
            pl.BlockSpec((TM, d), lambda i: (i, 0)),
            pl.BlockSpec((1, d), lambda i: (0, 0)),
            pl.BlockSpec(w.shape, lambda i: (0, 0), pipeline_mode=pl.Buffered(1)),
            coef_spec, coef_spec,
            pl.BlockSpec((TM, TM), lambda i: (0, 0), pipeline_mode=pl.Buffered(1)),
        ] + cast_specs,
        out_specs=[out_spec] + cast_specs,
        out_shape=[out_shape] + [jax.ShapeDtypeStruct(c.shape, BF16) for c in cast2d],
        compiler_params=_params("parallel"),
        name=f"group_proj_d{dil}",
    )(x, g, w, coef_a, coef_b, perm, *cast2d)
    return qkv, [c.reshape(s.shape) for c, s in zip(casts, cast_srcs)]


def _ple_qkv1_kernel(x_ref, p_ref, g_ref, wg_ref, wp_ref, gq_ref, gkv_ref, wq_ref, wkv_ref, cos_ref, sin_ref,
                     x_out_ref, q_ref, kv_ref):
    x = _ple_rows(x_ref[...], p_ref[...], g_ref, wg_ref, wp_ref)
    x_out_ref[...] = x
    xn = x * lax.rsqrt(jnp.mean(x * x, axis=-1, keepdims=True) + EPS)
    hq = (xn * gq_ref[...]).astype(BF16)
    hkv = (xn * gkv_ref[...]).astype(BF16)
    cos, sin = cos_ref[...], sin_ref[...]
    accq = jnp.dot(hq, wq_ref[...], preferred_element_type=F32)
    _store_chunks(accq, cos, sin, q_ref, (ROPE_SCALED,) * (accq.shape[1] // LANES))
    acckv = jnp.dot(hkv, wkv_ref[...], preferred_element_type=F32)
    _store_chunks(acckv, cos, sin, kv_ref, (ROPE, PLAIN))


def _ple_qkv1(x, p, layer, g, wg, wp, gq, gkv, wq, wkv, cos, sin):
    t, d = x.shape
    s_tiles = cos.shape[0] // TM
    row = lambda i: (i, 0)
    fixed = lambda i: (0, 0)
    return pl.pallas_call(
        _ple_qkv1_kernel,
        grid=(t // TM,),
        in_specs=[
            pl.BlockSpec((TM, d), row),
            pl.BlockSpec((None, TM, p.shape[2]), lambda i: (layer, i, 0)),
            pl.BlockSpec((1, d), fixed),
            pl.BlockSpec(wg.shape, fixed),
            pl.BlockSpec(wp.shape, fixed),
            pl.BlockSpec((1, d), fixed),
            pl.BlockSpec((1, d), fixed),
            pl.BlockSpec(wq.shape, fixed),
            pl.BlockSpec(wkv.shape, fixed),
            pl.BlockSpec((TM, LANES), lambda i: (i % s_tiles, 0)),
            pl.BlockSpec((TM, LANES), lambda i: (i % s_tiles, 0)),
        ],
        out_specs=[pl.BlockSpec((TM, d), row), pl.BlockSpec((TM, wq.shape[1]), row),
                   pl.BlockSpec((TM, wkv.shape[1]), row)],
        out_shape=[jax.ShapeDtypeStruct((t, d), F32),
                   jax.ShapeDtypeStruct((t, wq.shape[1]), BF16),
                   jax.ShapeDtypeStruct((t, wkv.shape[1]), BF16)],
        compiler_params=_params("parallel"),
        name="ple_qkv_layer1",
    )(x, p, g, wg, wp, gq, gkv, wq, wkv, cos, sin)


def _band_mask(has_prev, max_dist):
    qi = lax.broadcasted_iota(jnp.int32, (BLOCK, 2 * BLOCK), 0) + BLOCK
    kj = lax.broadcasted_iota(jnp.int32, (BLOCK, 2 * BLOCK), 1)
    rel = qi - kj
    mask = (rel >= 0) & (rel <= max_dist)
    if has_prev is not True:
        mask = mask & ((kj >= BLOCK) | has_prev)
    return jnp.concatenate([mask, mask], axis=0)


def _band_blocks(qb, prev_ref, cur_ref, lanes):
    own = cur_ref[qb * BLOCK:(qb + 1) * BLOCK, lanes]
    before = prev_ref[:, lanes] if qb == 0 else cur_ref[(qb - 1) * BLOCK:qb * BLOCK, lanes]
    return jnp.concatenate([before, own], axis=0)


def _lane_masks():
    lane = lax.broadcasted_iota(jnp.int32, (BLOCK, LANES), 1)
    first_qk = (lane % HEAD_DIM) < HALF_DIM
    sel_a = jnp.where(first_qk, 1.0, 0.0).astype(BF16)
    sel_b = jnp.where(first_qk, 0.0, 1.0).astype(BF16)
    return lane, sel_a, sel_b, lane < HEAD_DIM


def _pair_scores(q2, kcat, sel_a, sel_b):
    qs = jnp.concatenate([q2 * sel_a, q2 * sel_b], axis=0)
    return lax.dot_general(qs, kcat, (((1,), (1,)), ((), ())), preferred_element_type=F32)


def _dil_attn_kernel(q_ref, kp_ref, kc_ref, vp_ref, vc_ref, o_ref, st_ref, *, steps):
    lane, sel_a, sel_b, first_v = _lane_masks()
    for qb in range(q_ref.shape[0] // BLOCK):
        rows = slice(qb * BLOCK, (qb + 1) * BLOCK)
        mask2 = _band_mask(True if qb else pl.program_id(2) > 0, steps)
        stats = jnp.zeros((BLOCK, LANES), F32)
        for p in range(q_ref.shape[1] // LANES):
            sl = slice(p * LANES, (p + 1) * LANES)
            kcat = _band_blocks(qb, kp_ref, kc_ref, sl)
            vcat = _band_blocks(qb, vp_ref, vc_ref, sl)
            s = jnp.where(mask2, _pair_scores(q_ref[rows, sl], kcat, sel_a, sel_b), NEG_INF)
            m = jnp.max(s, axis=-1, keepdims=True)
            pe = jnp.exp2(s - m)
            den = jnp.sum(pe, axis=-1, keepdims=True)
            pv = jnp.dot(pe.astype(BF16), vcat, preferred_element_type=F32)
            pv = pv * (1.0 / den)
            o_ref[rows, sl] = jnp.where(first_v, pv[:BLOCK], pv[BLOCK:]).astype(o_ref.dtype)
            lse = m * LN_2 + jnp.log(den)
            stats = jnp.where(lane == 2 * p, lse[:BLOCK], stats)
            stats = jnp.where(lane == 2 * p + 1, lse[BLOCK:], stats)
        st_ref[rows, :] = stats


def _dilated_group(qkv, batch, seq, g):
    win, dil = A_CONFIGS[g]
    n = seq // dil
    d = qkv.shape[-1] // 3
    view = qkv.reshape(batch, dil, n, 3 * d)

    q_blocks = min(Q_BLOCKS, n // BLOCK)
    rows = q_blocks * BLOCK

    def spec(kind, prev):
        if prev:
            return pl.BlockSpec((None, None, BLOCK, d),
                                lambda b, r, blk: (b, r, jnp.maximum(blk * q_blocks - 1, 0), kind))
        return pl.BlockSpec((None, None, rows, d), lambda b, r, blk: (b, r, blk, kind))

    return pl.pallas_call(
        functools.partial(_dil_attn_kernel, steps=win // dil),
        grid=(batch, dil, n // rows),
        in_specs=[spec(0, False), spec(1, True), spec(1, False), spec(2, True), spec(2, False)],
        out_specs=[pl.BlockSpec((None, None, rows, d), lambda b, r, blk: (b, r, blk, 0)),
                   pl.BlockSpec((None, None, rows, LANES), lambda b, r, blk: (b, r, blk, 0))],
        out_shape=[jax.ShapeDtypeStruct((batch, dil, n, d), BF16),
                   jax.ShapeDtypeStruct((batch, dil, n, LANES), F32)],
        compiler_params=_params("parallel", "parallel", "arbitrary"),
        name=f"dilated_attn_g{g}",
    )(view, view, view, view, view)


def _swa_kernel(fill_ref, q_ref, kp_ref, kc_ref, vp_ref, vc_ref, o_ref):
    _, sel_a, sel_b, first_v = _lane_masks()
    all_lanes = slice(0, LANES)
    key_row = lax.broadcasted_iota(jnp.int32, (2 * BLOCK, LANES), 0)
    for qb in range(q_ref.shape[0] // BLOCK):
        rows = slice(qb * BLOCK, (qb + 1) * BLOCK)
        mask2 = _band_mask(True if qb else pl.program_id(1) > 0, B_WINDOW - 1)
        kcat = _band_blocks(qb, kp_ref, kc_ref, all_lanes)
        vcat = _band_blocks(qb, vp_ref, vc_ref, all_lanes)
        vcat = jnp.where(key_row == 0, jnp.zeros_like(vcat), vcat)
        for j in range(q_ref.shape[1] // LANES):
            sl = slice(j * LANES, (j + 1) * LANES)
            s = jnp.where(mask2, _pair_scores(q_ref[rows, sl], kcat, sel_a, sel_b), fill_ref[j])
            m = jnp.max(s, axis=-1, keepdims=True)
            pe = jnp.exp2(s - m)
            den = jnp.sum(pe, axis=-1, keepdims=True)
            pv = jnp.dot(pe.astype(BF16), vcat, preferred_element_type=F32)
            pv = pv * (1.0 / den)
            o_ref[rows, sl] = jnp.where(first_v, pv[:BLOCK], pv[BLOCK:]).astype(o_ref.dtype)


def _swa_attention(q, kv, sinks, batch, seq):
    d = q.shape[1]
    q_blocks = min(Q_BLOCKS, seq // BLOCK)
    rows = q_blocks * BLOCK
    qv = q.reshape(batch, seq, d)
    kvv = kv.reshape(batch, seq, 2 * LANES)

    def kv_spec(col, prev):
        if prev:
            return pl.BlockSpec((None, BLOCK, LANES),
                                lambda b, blk: (b, jnp.maximum(blk * q_blocks - 1, 0), col))
        return pl.BlockSpec((None, rows, LANES), lambda b, blk: (b, blk, col))

    half = sinks.shape[0] // 2
    sink_rows = jnp.repeat(jnp.stack([sinks[:half], sinks[half:]], axis=1), BLOCK, axis=1)
    first_key = jnp.arange(2 * BLOCK)[None, None, :] == 0
    fill = jnp.where(first_key, sink_rows[:, :, None].astype(F32) * LOG2_E, NEG_INF)

    o = pl.pallas_call(
        _swa_kernel,
        grid=(batch, seq // rows),
        in_specs=[pl.BlockSpec(fill.shape, lambda b, blk: (0, 0, 0)),
                  pl.BlockSpec((None, rows, d), lambda b, blk: (b, blk, 0)),
                  kv_spec(0, True), kv_spec(0, False), kv_spec(1, True), kv_spec(1, False)],
        out_specs=pl.BlockSpec((None, rows, d), lambda b, blk: (b, blk, 0)),
        out_shape=jax.ShapeDtypeStruct((batch, seq, d), BF16),
        compiler_params=_params("parallel", "arbitrary"),
        name="swa_sink_attn",
    )(fill, qv, kvv, kvv, kvv, kvv)
    return o.reshape(batch * seq, d)


def _interleave(src_ref, dst_ref):
    dil, sub = src_ref.shape[0], src_ref.shape[1]
    chunks = dst_ref.shape[0]
    for r in range(dil):
        for c in range(chunks):
            part = src_ref[r, :, c * LANES:(c + 1) * LANES].astype(F32)
            dst_ref[c, pl.ds(r, sub, stride=dil), :] = part
    return jnp.concatenate([dst_ref[c] for c in range(chunks)], axis=1)


def _merge_ffn_kernel(x_ref, o0_ref, o1_ref, o2_ref, s0_ref, s1_ref, s2_ref, e_ref, w_ref, g_ref, w1_ref,
                      w3_ref, w2_ref, out_ref, ob_ref, sb1_ref, sb2_ref):
    s0, s1, s2 = s0_ref[0], _interleave(s1_ref, sb1_ref), _interleave(s2_ref, sb2_ref)
    top = jnp.maximum(jnp.maximum(s0, s1), s2)
    e0, e1, e2 = jnp.exp(s0 - top), jnp.exp(s1 - top), jnp.exp(s2 - top)
    inv = 1.0 / (e0 + e1 + e2)
    expand = e_ref[...]

    def weight(e):
        return jnp.dot((e * inv).astype(BF16), expand, preferred_element_type=F32)

    merged = weight(e0) * o0_ref[0].astype(F32)
    merged = merged + weight(e1) * _interleave(o1_ref, ob_ref)
    merged = merged + weight(e2) * _interleave(o2_ref, ob_ref)
    x = x_ref[...] + jnp.dot(merged.astype(BF16), w_ref[...], preferred_element_type=F32)
    out_ref[...] = x + _swiglu(_rms(x, g_ref[...]).astype(BF16), w1_ref, w3_ref, w2_ref)


def _merge_ffn(x, outs, stats, expand, w, g, w1, w3, w2, batch, seq):
    t, d = x.shape
    tiles_per_seq = seq // TM
    row = lambda i: (i, 0)
    fixed = lambda i: (0, 0)
    resident = lambda a: pl.BlockSpec(a.shape, fixed, pipeline_mode=pl.Buffered(1))

    def sub(arr):
        dil = arr.shape[1]
        return pl.BlockSpec((None, dil, TM // dil, arr.shape[3]),
                            lambda i: (i // tiles_per_seq, 0, i % tiles_per_seq, 0))

    return pl.pallas_call(
        _merge_ffn_kernel,
        grid=(t // TM,),
        in_specs=[pl.BlockSpec((TM, d), row)]
        + [sub(o) for o in outs] + [sub(s) for s in stats]
        + [resident(expand), resident(w), pl.BlockSpec((1, d), fixed),
           resident(w1), resident(w3), resident(w2)],
        out_specs=pl.BlockSpec((TM, d), row),
        out_shape=jax.ShapeDtypeStruct((t, d), F32),
        scratch_shapes=[pltpu.VMEM((d // LANES, TM, LANES), F32), pltpu.VMEM((1, TM, LANES), F32),
                        pltpu.VMEM((1, TM, LANES), F32)],
        compiler_params=_params("parallel"),
        name="merge_wo_swiglu",
    )(x, *outs, *stats, expand, w, g, w1, w3, w2)


def _wo_router_kernel(x_ref, o_ref, w_ref, g_ref, wr_ref, tri_ref, x1_ref, x1t_ref, route_ref, gates_ref,
                      counts_ref, run_ref):
    @pl.when(pl.program_id(0) == 0)
    def _():
        run_ref[...] = jnp.zeros_like(run_ref)

    x1 = x_ref[...] + jnp.dot(o_ref[...], w_ref[...], preferred_element_type=F32)
    x1_ref[...] = x1
    _to_token_tiles(x1, x1t_ref)
    h = _rms(x1, g_ref[...]).astype(BF16)
    logits = jnp.dot(h, wr_ref[...], preferred_element_type=F32)
    lane = lax.broadcasted_iota(jnp.int32, logits.shape, 1)
    logits = jnp.where(lane < N_EXPERTS, logits, -jnp.inf)
    v1 = jnp.max(logits, axis=-1, keepdims=True)
    i1 = jnp.min(jnp.where(logits == v1, lane, LANES), axis=-1, keepdims=True)
    rest = jnp.where(lane == i1, -jnp.inf, logits)
    v2 = jnp.max(rest, axis=-1, keepdims=True)
    i2 = jnp.min(jnp.where(rest == v2, lane, LANES), axis=-1, keepdims=True)
    e2 = jnp.exp(v2 - v1)
    inv = 1.0 / (1.0 + e2)
    gates_ref[:, :LANES] = jnp.broadcast_to(inv, logits.shape)
    gates_ref[:, LANES:] = jnp.broadcast_to(e2 * inv, logits.shape)
    picked = jnp.where(lane == i1, 1.0, 0.0) + jnp.where(lane == i2, 1.0, 0.0)
    before = run_ref[0:1, :] + jnp.dot(tri_ref[...], picked.astype(BF16), preferred_element_type=F32)
    rank1 = jnp.sum(jnp.where(lane == i1, before, 0.0), axis=-1, keepdims=True)
    rank2 = jnp.sum(jnp.where(lane == i2, before, 0.0), axis=-1, keepdims=True)
    total = run_ref[0:1, :] + jnp.sum(picked, axis=0, keepdims=True)
    run_ref[0:1, :] = total
    counts_ref[...] = jnp.broadcast_to(total, counts_ref.shape)
    info = jnp.where(lane == 0, i1.astype(F32), 0.0)
    info = jnp.where(lane == 1, i2.astype(F32), info)
    info = jnp.where(lane == 2, rank1, info)
    info = jnp.where(lane == 3, rank2, info)
    route_ref[...] = info.T[:SUBLANES, :]


def _wo_router(x, o, w, g, wr):
    t, d = x.shape
    row = lambda i: (i, 0)
    fixed = lambda i: (0, 0)
    tri = jnp.asarray(np.tril(np.ones((TM, TM), np.float32), -1), BF16)
    return pl.pallas_call(
        _wo_router_kernel,
        grid=(t // TM,),
        in_specs=[pl.BlockSpec((TM, d), row), pl.BlockSpec((TM, o.shape[1]), row),
                  pl.BlockSpec(w.shape, fixed), pl.BlockSpec((1, d), fixed),
                  pl.BlockSpec(wr.shape, fixed), pl.BlockSpec(tri.shape, fixed)],
        out_specs=[pl.BlockSpec((TM, d), row), pl.BlockSpec((TM * SUBLANES, LANES), row),
                   pl.BlockSpec((SUBLANES, TM), lambda i: (0, i)), pl.BlockSpec((TM, 2 * LANES), row),
                   pl.BlockSpec((SUBLANES, LANES), fixed)],
        out_shape=[jax.ShapeDtypeStruct((t, d), F32),
                   jax.ShapeDtypeStruct((t * SUBLANES, LANES), F32),
                   jax.ShapeDtypeStruct((SUBLANES, t), F32),
                   jax.ShapeDtypeStruct((t, 2 * LANES), F32),
                   jax.ShapeDtypeStruct((SUBLANES, LANES), F32)],
        scratch_shapes=[pltpu.VMEM((SUBLANES, LANES), F32)],
        compiler_params=_params("arbitrary"),
        name="wo_router",
    )(x, o, w, g, wr, tri)


def _swiglu(h, w1_ref, w3_ref, w2_ref):
    width = w1_ref.shape[1] // SWIGLU_SPLITS
    y = None
    for c in range(SWIGLU_SPLITS):
        sl = slice(c * width, (c + 1) * width)
        a = jnp.dot(h, w1_ref[:, sl], preferred_element_type=F32)
        b = jnp.dot(h, w3_ref[:, sl], preferred_element_type=F32)
        act = (a * jax.nn.sigmoid(a) * b).astype(BF16)
        part = jnp.dot(act, w2_ref[sl, :], preferred_element_type=F32)
        y = part if y is None else y + part
    return y


def _moe_kernel(te_ref, nu_ref, xs_ref, g_ref, w1_ref, w3_ref, w2_ref, o_ref):
    used = pl.program_id(0) < nu_ref[0]

    @pl.when(used)
    def _():
        x = _from_token_tiles(xs_ref, xs_ref.shape[0] // SUBLANES)
        _to_token_tiles(_swiglu(_rms(x, g_ref[...]).astype(BF16), w1_ref, w3_ref, w2_ref), o_ref)

    @pl.when(jnp.logical_not(used))
    def _():
        o_ref[...] = jnp.zeros_like(o_ref)


def _moe_experts(xs, g, w1, w3, w2, tile_expert, n_used, n_tiles):
    d, f = w1.shape[1], w1.shape[2]
    lines = TM * SUBLANES
    resident = pl.Buffered(1)
    grid_spec = pltpu.PrefetchScalarGridSpec(
        num_scalar_prefetch=2,
        grid=(n_tiles,),
        in_specs=[pl.BlockSpec((lines, LANES), lambda i, te, nu: (i, 0)),
                  pl.BlockSpec((1, d), lambda i, te, nu: (0, 0)),
                  pl.BlockSpec((None, d, f), lambda i, te, nu: (te[i], 0, 0), pipeline_mode=resident),
                  pl.BlockSpec((None, d, f), lambda i, te, nu: (te[i], 0, 0), pipeline_mode=resident),
                  pl.BlockSpec((None, f, d), lambda i, te, nu: (te[i], 0, 0))],
        out_specs=pl.BlockSpec((lines, LANES), lambda i, te, nu: (i, 0)),
    )
    return pl.pallas_call(
        _moe_kernel,
        grid_spec=grid_spec,
        out_shape=jax.ShapeDtypeStruct((n_tiles * lines, LANES), F32),
        compiler_params=_params("arbitrary"),
        name="moe_experts",
    )(tile_expert, n_used, xs, g, w1, w3, w2)


def _dispatch_kernel(dst0_ref, dst1_ref, pad_ref, src_ref, zero_ref, out_hbm, sem):
    rows = dst0_ref.shape[2]
    n_pad = pad_ref.shape[0]

    def issue(r, carry):
        base = r * DMA_UNROLL
        idx = [[ref[0, 0, base + q] for q in range(DMA_UNROLL)] for ref in (dst0_ref, dst1_ref)]
        for u in range(DMA_UNROLL):
            src = _tile_at(src_ref, (base + u) * SUBLANES)
            for k in range(2):
                pltpu.make_async_copy(src, _tile_at(out_hbm, idx[k][u]), sem).start(priority=k)
        return carry

    lax.fori_loop(0, rows // DMA_UNROLL, issue, 0)
    lines = 2 * rows * SUBLANES
    pltpu.make_async_copy(out_hbm.at[pl.ds(0, lines)], out_hbm.at[pl.ds(0, lines)], sem).wait()

    @pl.when(pl.program_id(0) == 0)
    def _():
        def fill(r, carry):
            base = r * DMA_UNROLL
            idx = [pad_ref[base + q] for q in range(DMA_UNROLL)]
            for q in range(DMA_UNROLL):
                pltpu.make_async_copy(zero_ref, _tile_at(out_hbm, idx[q]), sem).start()
            return carry

        lax.fori_loop(0, n_pad // DMA_UNROLL, fill, 0)
        pad_lines = n_pad * SUBLANES
        pltpu.make_async_copy(out_hbm.at[pl.ds(0, pad_lines)], out_hbm.at[pl.ds(0, pad_lines)],
                              sem).wait()


def _dispatch(x_tiles, dest_line, pad_line, n_slots):
    t = x_tiles.shape[0] // SUBLANES
    steps = t // DISPATCH_ROWS
    zero = jnp.zeros((SUBLANES, LANES), x_tiles.dtype)
    idx = dest_line.reshape(2 * steps, 1, DISPATCH_ROWS)
    idx_spec = lambda pick: pl.BlockSpec((1, 1, DISPATCH_ROWS), lambda i: (pick * steps + i, 0, 0),
                                         memory_space=pltpu.SMEM)
    return pl.pallas_call(
        _dispatch_kernel,
        grid=(steps,),
        in_specs=[idx_spec(0), idx_spec(1),
                  pl.BlockSpec(memory_space=pltpu.SMEM),
                  pl.BlockSpec((DISPATCH_ROWS * SUBLANES, LANES), lambda i: (i, 0)),
                  pl.BlockSpec((SUBLANES, LANES), lambda i: (0, 0))],
        out_specs=pl.BlockSpec(memory_space=pl.ANY),
        out_shape=jax.ShapeDtypeStruct((n_slots * SUBLANES, LANES), x_tiles.dtype),
        scratch_shapes=[pltpu.SemaphoreType.DMA(())],
        compiler_params=_params("arbitrary"),
        name="moe_dispatch",
    )(idx, idx, pad_line, x_tiles, zero)


def _ple_rows(x, p, g_ref, wg_ref, wp_ref, between=None):
    h = _rms(x, g_ref[...]).astype(BF16)
    gate = []
    for c in range(wg_ref.shape[1] // MXU_COLS):
        if between is not None:
            between(c)
        logits = jnp.dot(h, wg_ref[:, c * MXU_COLS:(c + 1) * MXU_COLS], preferred_element_type=F32)
        gate.append(jax.nn.sigmoid(logits))
    proj = jnp.dot(p.astype(BF16), wp_ref[...], preferred_element_type=F32)
    return x + proj * jnp.concatenate(gate, axis=1)


def _combine_ple_kernel(cur0_ref, cur1_ref, nxt0_ref, nxt1_ref, x_ref, gates_ref, p_ref, g_ref, wg_ref,
                        wp_ref, fg_ref, ys_hbm, o_ref, a_ref, b_ref, sem):
    i = pl.program_id(0)
    last = pl.num_programs(0) - 1
    tm = a_ref.shape[1] // SUBLANES
    groups = tm // DMA_UNROLL
    cur_ref, nxt_ref = (cur0_ref, cur1_ref), (nxt0_ref, nxt1_ref)

    def issue_group(idx_ref, offset, slot, base):
        idx = [[ref[0, 0, offset + base + q] for q in range(DMA_UNROLL)] for ref in idx_ref]
        for u in range(DMA_UNROLL):
            line = (base + u) * SUBLANES
            pltpu.make_async_copy(_tile_at(ys_hbm, idx[0][u]), _tile_at(a_ref.at[slot], line),
                                  sem.at[slot]).start(priority=0)
            pltpu.make_async_copy(_tile_at(ys_hbm, idx[1][u]), _tile_at(b_ref.at[slot], line),
                                  sem.at[slot]).start(priority=1)

    def wait(slot):
        whole = ys_hbm.at[pl.ds(0, tm * SUBLANES)]
        pltpu.make_async_copy(whole, a_ref.at[slot], sem.at[slot]).wait()
        pltpu.make_async_copy(whole, b_ref.at[slot], sem.at[slot]).wait()

    def compute(slot, idx_ref, offset):
        rows = slice(slot * tm, (slot + 1) * tm)
        g1, g2 = gates_ref[rows, :LANES], gates_ref[rows, LANES:]
        y = jnp.concatenate(
            [g1 * a_ref[slot, pl.ds(c, tm, stride=SUBLANES), :]
             + g2 * b_ref[slot, pl.ds(c, tm, stride=SUBLANES), :] for c in range(SUBLANES)], axis=1)
        chunks = wg_ref.shape[1] // MXU_COLS

        def between(c):
            for grp in range(c * groups // chunks, (c + 1) * groups // chunks):
                issue_group(idx_ref, offset, 1 - slot, grp * DMA_UNROLL)

        x = _ple_rows(x_ref[rows, :] + y, p_ref[rows, :], g_ref, wg_ref, wp_ref, between)
        o_ref[rows, :] = _rms(x, fg_ref[...])

    @pl.when(i == 0)
    def _():
        lax.fori_loop(0, groups, lambda r, c: (issue_group(cur_ref, 0, 0, r * DMA_UNROLL), c)[1], 0)

    wait(0)
    compute(0, cur_ref, tm)
    wait(1)
    compute(1, nxt_ref, 0)

    @pl.when(i == last)
    def _():
        wait(0)


def _combine_ple(x, gates, ys, dest_line, p, layer, g, wg, wp, fg):
    t, d = x.shape
    tm = TM_COMBINE
    steps = t // (2 * tm)
    row = lambda i: (i, 0)
    fixed = lambda i: (0, 0)
    idx = dest_line.reshape(2 * steps, 1, 2 * tm)
    idx_spec = lambda index: pl.BlockSpec((1, 1, 2 * tm), index, memory_space=pltpu.SMEM)
    nxt = lambda i: jnp.minimum(i + 1, steps - 1)
    return pl.pallas_call(
        _combine_ple_kernel,
        grid=(steps,),
        in_specs=[idx_spec(lambda i: (i, 0, 0)), idx_spec(lambda i: (steps + i, 0, 0)),
                  idx_spec(lambda i: (nxt(i), 0, 0)), idx_spec(lambda i: (steps + nxt(i), 0, 0)),
                  pl.BlockSpec((2 * tm, d), row),
                  pl.BlockSpec((2 * tm, 2 * LANES), row),
                  pl.BlockSpec((None, 2 * tm, p.shape[2]), lambda i: (layer, i, 0)),
                  pl.BlockSpec((1, d), fixed),
                  pl.BlockSpec(wg.shape, fixed),
                  pl.BlockSpec(wp.shape, fixed),
                  pl.BlockSpec((1, d), fixed),
                  pl.BlockSpec(memory_space=pl.ANY)],
        out_specs=pl.BlockSpec((2 * tm, d), row),
        out_shape=jax.ShapeDtypeStruct((t, d), F32),
        scratch_shapes=[pltpu.VMEM((2, tm * SUBLANES, LANES), F32),
                        pltpu.VMEM((2, tm * SUBLANES, LANES), F32),
                        pltpu.SemaphoreType.DMA((2,))],
        compiler_params=_params("arbitrary"),
        name="combine_ple_final",
    )(idx, idx, idx, idx, x, gates, p, g, wg, wp, fg, ys)


def _routing_tables(route, counts_f):
    t = route.shape[1]
    picked = route[0:2].astype(jnp.int32)
    rank = route[2:4].astype(jnp.int32)
    counts = counts_f[0, :N_EXPERTS].astype(jnp.int32)
    tiles_per = (counts + TM - 1) // TM
    tile_end = jnp.cumsum(tiles_per)
    tile_start = tile_end - tiles_per
    first_slot = jnp.zeros_like(picked)
    for e in range(N_EXPERTS):
        first_slot = jnp.where(picked == e, tile_start[e] * TM, first_slot)
    dest = first_slot + rank
    n_tiles = (2 * t) // TM + N_EXPERTS
    n_used = tile_end[-1]
    tile_ids = jnp.minimum(jnp.arange(n_tiles), n_used - 1)
    tile_expert = jnp.minimum(jnp.sum((tile_ids[:, None] >= tile_end[None, :]).astype(jnp.int32), axis=1),
                              N_EXPERTS - 1)
    pad_sizes = jnp.concatenate([tiles_per * TM - counts, ((n_tiles - n_used) * TM).reshape(1)])
    pad_end = jnp.cumsum(pad_sizes)
    seg_first = jnp.concatenate([tile_start * TM + counts, (n_used * TM).reshape(1)])
    k = jnp.arange(N_EXPERTS * TM)
    seg = jnp.sum((k[:, None] >= pad_end[None, :]).astype(jnp.int32), axis=1)
    pad = seg_first[seg] + k - (pad_end - pad_sizes)[seg]
    to_line = lambda v: (v * SUBLANES).astype(jnp.int32)
    return (to_line(dest), to_line(pad), tile_expert.astype(jnp.int32),
            n_used.reshape(1).astype(jnp.int32), n_tiles)


def kernel(x, p, attn_norm, ffn_norm, a_w_qkv, a_w_o, kv_norm, kv_w, b_w_q, b_sinks, b_w_o,
           dense_w1, dense_w3, dense_w2, moe_router, moe_w1, moe_w3, moe_w2,
           ple_norm, ple_w_gate, ple_w_proj, final_norm):
    batch, seq, d = x.shape
    t = batch * seq
    n_heads = d // HEAD_DIM
    n_groups = len(A_CONFIGS)
    assert d == SUBLANES * LANES and seq % TM == 0 and t % DISPATCH_ROWS == 0
    xr = x.reshape(t, d)
    pr = p.reshape(p.shape[0], t, p.shape[3])
    cos, sin = _rope_tables(np.arange(seq))
    row = lambda v: v.reshape(1, -1)

    w = a_w_qkv[0].astype(BF16).reshape(d, n_groups, 3, d)
    coef_a, coef_b = _rope_coeffs(cos, sin)
    cast_jobs = ((moe_w1[0], dense_w1[0], a_w_o[0], b_w_q[0]),
                 (moe_w3[0], dense_w3[0], ple_w_gate),
                 (moe_w2[0], b_w_o[0]))
    outs, stats, cast = [], [], []
    for g, (_, dil) in enumerate(A_CONFIGS):
        w_g = jnp.concatenate([_pair_columns(w[:, g, 0], True), _pair_columns(w[:, g, 1], True),
                               w[:, g, 2]], axis=1)
        qkv, done = _group_proj(xr, row(attn_norm[0]), w_g, coef_a, coef_b, batch, seq, dil, cast_jobs[g])
        cast.append(done)
        o_g, st_g = _dilated_group(qkv, batch, seq, g)
        outs.append(o_g)
        stats.append(st_g)
    (moe1, dense1, wo0, wq1), (moe3, dense3, ple_gate), (moe2, wo1) = cast
    head_of_col = jnp.arange(d) // HEAD_DIM
    expand = (jnp.arange(LANES)[:, None] == head_of_col[None, :]).astype(BF16)
    xr = _merge_ffn(xr, outs, stats, expand, wo0, row(ffn_norm[0]), dense1, dense3,
                    dense_w2[0].astype(BF16), batch, seq)

    half = n_heads // 2
    kv_cols = B_KV_HEADS * HEAD_DIM
    kv_bf = kv_w.astype(BF16)
    w_kv = jnp.concatenate([_pair_columns(kv_bf[:, :kv_cols], True), kv_bf[:, kv_cols:]], axis=1)
    xr, q1, kv1 = _ple_qkv1(xr, pr, 0, row(ple_norm[0]), ple_gate[0], ple_w_proj[0].astype(BF16),
                            row(attn_norm[1]), row(kv_norm), _pair_columns(wq1, False), w_kv, cos, sin)
    o1 = _swa_attention(q1, kv1, b_sinks[0], batch, seq)
    w_o1 = wo1.reshape(2, half, HEAD_DIM, d).transpose(1, 0, 2, 3).reshape(d, d)
    wr = jnp.pad(moe_router[0], ((0, 0), (0, LANES - N_EXPERTS))).astype(BF16)
    xr, x_tiles, route, gates, counts = _wo_router(xr, o1, w_o1, row(ffn_norm[1]), wr)

    dest_line, pad_line, tile_expert, n_used, n_tiles = _routing_tables(route, counts)
    xs = _dispatch(x_tiles, dest_line, pad_line, n_tiles * TM)
    ys = _moe_experts(xs, row(ffn_norm[1]), moe1, moe3, moe2, tile_expert, n_used, n_tiles)
    xr = _combine_ple(xr, gates, ys, dest_line, pr, 1, row(ple_norm[1]), ple_gate[1],
                      ple_w_proj[1].astype(BF16), row(final_norm))
    return xr.reshape(batch, seq, d)
```

```python
import functools

import numpy as np
import jax
import jax.numpy as jnp
from jax import lax
from jax.experimental import pallas as pl
from jax.experimental.pallas import tpu as pltpu

F32 = jnp.float32
BF16 = jnp.bfloat16

LANES = 128
SUBLANES = 8
MXU_COLS = 256
HEAD_DIM = 64
HALF_DIM = HEAD_DIM // 2
BLOCK = 128
Q_BLOCKS = 4
A_CONFIGS = ((128, 1), (512, 4), (2048, 16))
B_KV_HEADS = 2
B_WINDOW = 128
N_EXPERTS = 8
ROPE_THETA = 10000.0
EPS = 1e-6
NEG_INF = -1e30
VMEM_LIMIT = 56 * 1024 * 1024

TM = 512
TM_COMBINE = 256
DISPATCH_ROWS = 2048
DMA_UNROLL = 8
SWIGLU_SPLITS = 2

PLAIN, ROPE, ROPE_SCALED = 0, 1, 2
LOG2_E = 1.4426950408889634
LN_2 = 0.6931471805599453
Q_SCALE = HEAD_DIM ** -0.5 * LOG2_E


def _params(*sem):
    return pltpu.CompilerParams(dimension_semantics=sem, vmem_limit_bytes=VMEM_LIMIT)


def _rms(x, g):
    ms = jnp.mean(x * x, axis=-1, keepdims=True)
    return x * lax.rsqrt(ms + EPS) * g


def _pair_columns(w, adjacent):
    d, heads = w.shape[0], w.shape[1] // HEAD_DIM
    if adjacent:
        return w.reshape(d, heads // 2, 2, 2, HALF_DIM).transpose(0, 1, 3, 2, 4).reshape(w.shape)
    return w.reshape(d, 2, heads // 2, 2, HALF_DIM).transpose(0, 2, 3, 1, 4).reshape(w.shape)


def _rope_tables(positions):
    pos = jnp.asarray(positions, F32)
    inv = 1.0 / (ROPE_THETA ** (jnp.arange(HALF_DIM, dtype=F32) / HALF_DIM))
    ang = pos[:, None] * inv[None, :]
    cos, sin = jnp.cos(ang), jnp.sin(ang)
    return jnp.tile(cos, (1, 4)), jnp.concatenate([-sin, -sin, sin, sin], axis=1)


def _rope_coeffs(cos, sin):
    a = jnp.stack([cos * Q_SCALE, cos, jnp.ones_like(cos)])
    b = jnp.stack([sin * Q_SCALE, sin, jnp.zeros_like(sin)])
    return a, b


def _store_chunks(acc, cos, sin, o_ref, kinds):
    for c, kind in enumerate(kinds):
        seg = acc[:, c * LANES:(c + 1) * LANES]
        if kind != PLAIN:
            seg = seg * cos + pltpu.roll(seg, LANES // 2, 1) * sin
            if kind == ROPE_SCALED:
                seg = seg * Q_SCALE
        o_ref[:, c * LANES:(c + 1) * LANES] = seg.astype(o_ref.dtype)


def _to_token_tiles(val, ref):
    rows = val.shape[0]
    for c in range(val.shape[1] // LANES):
        ref[pl.ds(c, rows, stride=SUBLANES), :] = val[:, c * LANES:(c + 1) * LANES]


def _from_token_tiles(ref, rows):
    return jnp.concatenate([ref[pl.ds(c, rows, stride=SUBLANES), :] for c in range(SUBLANES)], axis=1)


def _tile_at(ref, line):
    return ref.at[pl.ds(pl.multiple_of(line, SUBLANES), SUBLANES)]


class _Part:
    def __init__(self, body, inputs, in_specs, out_specs, out_shapes):
        self.body, self.inputs, self.in_specs = body, list(inputs), list(in_specs)
        self.out_specs, self.out_shapes = list(out_specs), list(out_shapes)


def _run_parts(parts, steps, name):
    n_in = [len(p.in_specs) for p in parts]
    n_out = [len(p.out_specs) for p in parts]

    def kernel(*refs):
        ins, outs = refs[:sum(n_in)], refs[sum(n_in):]
        for k, part in enumerate(parts):
            part.body(ins[sum(n_in[:k]):sum(n_in[:k + 1])], outs[sum(n_out[:k]):sum(n_out[:k + 1])])

    res = pl.pallas_call(
        kernel,
        grid=(steps,),
        in_specs=[s for p in parts for s in p.in_specs],
        out_specs=[s for p in parts for s in p.out_specs],
        out_shape=[s for p in parts for s in p.out_shapes],
        compiler_params=_params("parallel"),
        name=name,
    )(*[a for p in parts for a in p.inputs])
    return [res[sum(n_out[:k]):sum(n_out[:k + 1])] for k in range(len(parts))]


def _group_proj_body(ins, outs, *, dil):
    x_ref, g_ref, w_ref, a_ref, b_ref, perm_ref, *cast_refs = ins
    o_ref, *cast_out_refs = outs
    for src_ref, dst_ref in zip(cast_refs, cast_out_refs):
        dst_ref[...] = src_ref[...].astype(dst_ref.dtype)
    h = _rms(x_ref[...], g_ref[...]).astype(BF16)
    if dil > 1:
        h = jnp.dot(perm_ref[...], h, preferred_element_type=F32).astype(BF16)
    sub = h.shape[0] // dil
    tn = w_ref.shape[1] // 3
    for kind in range(3):
        acc = jnp.dot(h, w_ref[:, kind * tn:(kind + 1) * tn], preferred_element_type=F32)
        a, b = a_ref[kind], b_ref[kind]
        for c in range(tn // LANES):
            seg = acc[:, c * LANES:(c + 1) * LANES]
            seg = (seg * a + pltpu.roll(seg, LANES // 2, 1) * b).astype(o_ref.dtype)
            cols = slice(kind * tn + c * LANES, kind * tn + (c + 1) * LANES)
            if dil == 1:
                o_ref[:, cols] = seg
            else:
                for r in range(dil):
                    o_ref[r, :, cols] = seg[r * sub:(r + 1) * sub]


def _residue_major_rows(dil):
    q = np.arange(TM)
    sub = TM // dil
    return (q % sub) * dil + q // sub


def _group_proj_part(x, g, w, coef_a, coef_b, batch, seq, dil, cast_srcs):
    t, d = x.shape
    n_out = w.shape[1]
    tiles_per_seq = seq // TM
    steps = t // TM
    cast2d = [c.reshape(-1, c.shape[-1]) for c in cast_srcs]
    for c in cast2d:
        assert c.shape[0] % (steps * 2 * SUBLANES) == 0, c.shape
    cast_specs = [pl.BlockSpec((c.shape[0] // steps, c.shape[1]), lambda i: (i, 0)) for c in cast2d]
    if dil == 1:
        out_spec = pl.BlockSpec((TM, n_out), lambda i: (i, 0))
        out_shape = jax.ShapeDtypeStruct((t, n_out), BF16)
    else:
        out_spec = pl.BlockSpec((None, dil, TM // dil, n_out),
                                lambda i: (i // tiles_per_seq, 0, i % tiles_per_seq, 0))
        out_shape = jax.ShapeDtypeStruct((batch, dil, seq // dil, n_out), BF16)
    nat = _residue_major_rows(dil)
    perm = jnp.asarray(nat[:, None] == np.arange(TM)[None, :], BF16)
    if dil > 1:
        tile_order = lambda c: c.reshape(3, seq // TM, TM // dil, dil, LANES).transpose(0, 1, 3, 2, 4) \
            .reshape(3, seq, LANES)
        coef_a, coef_b = tile_order(coef_a), tile_order(coef_b)
    coef_spec = pl.BlockSpec((3, TM, LANES), lambda i: (0, i % tiles_per_seq, 0))
    return _Part(
        functools.partial(_group_proj_body, dil=dil),
        [x, g, w, coef_a, coef_b, perm, *cast2d],
        [pl.BlockSpec((TM, d), lambda i: (i, 0)),
         pl.BlockSpec((1, d), lambda i: (0, 0)),
         pl.BlockSpec(w.shape, lambda i: (0, 0), pipeline_mode=pl.Buffered(1)),
         coef_spec, coef_spec,
         pl.BlockSpec((TM, TM), lambda i: (0, 0), pipeline_mode=pl.Buffered(1))] + cast_specs,
        [out_spec] + cast_specs,
        [out_shape] + [jax.ShapeDtypeStruct(c.shape, BF16) for c in cast2d])


def _ple_qkv1_kernel(x_ref, p_ref, g_ref, wg_ref, wp_ref, gq_ref, gkv_ref, wq_ref, wkv_ref, cos_ref, sin_ref,
                     x_out_ref, q_ref, kv_ref):
    x = _ple_rows(x_ref[...], p_ref[...], g_ref, wg_ref, wp_ref)
    x_out_ref[...] = x
    xn = x * lax.rsqrt(jnp.mean(x * x, axis=-1, keepdims=True) + EPS)
    hq = (xn * gq_ref[...]).astype(BF16)
    hkv = (xn * gkv_ref[...]).astype(BF16)
    cos, sin = cos_ref[...], sin_ref[...]
    accq = jnp.dot(hq, wq_ref[...], preferred_element_type=F32)
    _store_chunks(accq, cos, sin, q_ref, (ROPE_SCALED,) * (accq.shape[1] // LANES))
    acckv = jnp.dot(hkv, wkv_ref[...], preferred_element_type=F32)
    _store_chunks(acckv, cos, sin, kv_ref, (ROPE, PLAIN))


def _ple_qkv1(x, p, layer, g, wg, wp, gq, gkv, wq, wkv, cos, sin):
    t, d = x.shape
    s_tiles = cos.shape[0] // TM
    row = lambda i: (i, 0)
    fixed = lambda i: (0, 0)
    return pl.pallas_call(
        _ple_qkv1_kernel,
        grid=(t // TM,),
        in_specs=[
            pl.BlockSpec((TM, d), row),
            pl.BlockSpec((None, TM, p.shape[2]), lambda i: (layer, i, 0)),
            pl.BlockSpec((1, d), fixed),
            pl.BlockSpec(wg.shape, fixed),
            pl.BlockSpec(wp.shape, fixed),
            pl.BlockSpec((1, d), fixed),
            pl.BlockSpec((1, d), fixed),
            pl.BlockSpec(wq.shape, fixed),
            pl.BlockSpec(wkv.shape, fixed),
            pl.BlockSpec((TM, LANES), lambda i: (i % s_tiles, 0)),
            pl.BlockSpec((TM, LANES), lambda i: (i % s_tiles, 0)),
        ],
        out_specs=[pl.BlockSpec((TM, d), row), pl.BlockSpec((TM, wq.shape[1]), row),
                   pl.BlockSpec((TM, wkv.shape[1]), row)],
        out_shape=[jax.ShapeDtypeStruct((t, d), F32),
                   jax.ShapeDtypeStruct((t, wq.shape[1]), BF16),
                   jax.ShapeDtypeStruct((t, wkv.shape[1]), BF16)],
        compiler_params=_params("parallel"),
        name="ple_qkv_layer1",
    )(x, p, g, wg, wp, gq, gkv, wq, wkv, cos, sin)


def _band_mask(has_prev, max_dist):
    qi = lax.broadcasted_iota(jnp.int32, (BLOCK, 2 * BLOCK), 0) + BLOCK
    kj = lax.broadcasted_iota(jnp.int32, (BLOCK, 2 * BLOCK), 1)
    rel = qi - kj
    mask = (rel >= 0) & (rel <= max_dist)
    if has_prev is not True:
        mask = mask & ((kj >= BLOCK) | has_prev)
    return jnp.concatenate([mask, mask], axis=0)


def _band_blocks(qb, prev_ref, cur_ref, lanes):
    own = cur_ref[qb * BLOCK:(qb + 1) * BLOCK, lanes]
    before = prev_ref[:, lanes] if qb == 0 else cur_ref[(qb - 1) * BLOCK:qb * BLOCK, lanes]
    return jnp.concatenate([before, own], axis=0)


def _lane_masks():
    lane = lax.broadcasted_iota(jnp.int32, (BLOCK, LANES), 1)
    first_qk = (lane % HEAD_DIM) < HALF_DIM
    sel_a = jnp.where(first_qk, 1.0, 0.0).astype(BF16)
    sel_b = jnp.where(first_qk, 0.0, 1.0).astype(BF16)
    return lane, sel_a, sel_b, lane < HEAD_DIM


def _pair_scores(q2, kcat, sel_a, sel_b):
    qs = jnp.concatenate([q2 * sel_a, q2 * sel_b], axis=0)
    return lax.dot_general(qs, kcat, (((1,), (1,)), ((), ())), preferred_element_type=F32)


def _dil_attn_body(ins, outs, *, steps, steps_per_seq):
    q_ref, kp_ref, kc_ref, vp_ref, vc_ref = ins
    o_ref, st_ref = outs
    lane, sel_a, sel_b, first_v = _lane_masks()
    for qb in range(q_ref.shape[0] // BLOCK):
        rows = slice(qb * BLOCK, (qb + 1) * BLOCK)
        mask2 = _band_mask(True if qb else pl.program_id(0) % steps_per_seq > 0, steps)
        stats = jnp.zeros((BLOCK, LANES), F32)
        for p in range(q_ref.shape[1] // LANES):
            sl = slice(p * LANES, (p + 1) * LANES)
            kcat = _band_blocks(qb, kp_ref, kc_ref, sl)
            vcat = _band_blocks(qb, vp_ref, vc_ref, sl)
            s = jnp.where(mask2, _pair_scores(q_ref[rows, sl], kcat, sel_a, sel_b), NEG_INF)
            m = jnp.max(s, axis=-1, keepdims=True)
            pe = jnp.exp2(s - m)
            den = jnp.sum(pe, axis=-1, keepdims=True)
            pv = jnp.dot(pe.astype(BF16), vcat, preferred_element_type=F32)
            pv = pv * (1.0 / den)
            o_ref[rows, sl] = jnp.where(first_v, pv[:BLOCK], pv[BLOCK:]).astype(o_ref.dtype)
            lse = m * LN_2 + jnp.log(den)
            stats = jnp.where(lane == 2 * p, lse[:BLOCK], stats)
            stats = jnp.where(lane == 2 * p + 1, lse[BLOCK:], stats)
        st_ref[rows, :] = stats


def _dilated_attn_part(qkv, batch, seq, g):
    win, dil = A_CONFIGS[g]
    n = seq // dil
    d = qkv.shape[-1] // 3
    view = qkv.reshape(batch * dil, n, 3 * d)

    q_blocks = min(Q_BLOCKS, n // BLOCK)
    rows = q_blocks * BLOCK
    per_seq = n // rows

    def spec(kind, prev):
        if prev:
            return pl.BlockSpec((None, BLOCK, d), lambda i: (
                i // per_seq, jnp.maximum((i % per_seq) * q_blocks - 1, 0), kind))
        return pl.BlockSpec((None, rows, d), lambda i: (i // per_seq, i % per_seq, kind))

    out_index = lambda i: (i // per_seq, i % per_seq, 0)
    part = _Part(
        functools.partial(_dil_attn_body, steps=win // dil, steps_per_seq=per_seq),
        [view] * 5,
        [spec(0, False), spec(1, True), spec(1, False), spec(2, True), spec(2, False)],
        [pl.BlockSpec((None, rows, d), out_index), pl.BlockSpec((None, rows, LANES), out_index)],
        [jax.ShapeDtypeStruct((batch * dil, n, d), BF16),
         jax.ShapeDtypeStruct((batch * dil, n, LANES), F32)])
    return part, batch * dil * per_seq


def _swa_kernel(fill_ref, q_ref, kp_ref, kc_ref, vp_ref, vc_ref, o_ref):
    _, sel_a, sel_b, first_v = _lane_masks()
    all_lanes = slice(0, LANES)
    key_row = lax.broadcasted_iota(jnp.int32, (2 * BLOCK, LANES), 0)
    for qb in range(q_ref.shape[0] // BLOCK):
        rows = slice(qb * BLOCK, (qb + 1) * BLOCK)
        mask2 = _band_mask(True if qb else pl.program_id(1) > 0, B_WINDOW - 1)
        kcat = _band_blocks(qb, kp_ref, kc_ref, all_lanes)
        vcat = _band_blocks(qb, vp_ref, vc_ref, all_lanes)
        vcat = jnp.where(key_row == 0, jnp.zeros_like(vcat), vcat)
        for j in range(q_ref.shape[1] // LANES):
            sl = slice(j * LANES, (j + 1) * LANES)
            s = jnp.where(mask2, _pair_scores(q_ref[rows, sl], kcat, sel_a, sel_b), fill_ref[j])
            m = jnp.max(s, axis=-1, keepdims=True)
            pe = jnp.exp2(s - m)
            den = jnp.sum(pe, axis=-1, keepdims=True)
            pv = jnp.dot(pe.astype(BF16), vcat, preferred_element_type=F32)
            pv = pv * (1.0 / den)
            o_ref[rows, sl] = jnp.where(first_v, pv[:BLOCK], pv[BLOCK:]).astype(o_ref.dtype)


def _swa_attention(q, kv, sinks, batch, seq):
    d = q.shape[1]
    q_blocks = min(Q_BLOCKS, seq // BLOCK)
    rows = q_blocks * BLOCK
    qv = q.reshape(batch, seq, d)
    kvv = kv.reshape(batch, seq, 2 * LANES)

    def kv_spec(col, prev):
        if prev:
            return pl.BlockSpec((None, BLOCK, LANES),
                                lambda b, blk: (b, jnp.maximum(blk * q_blocks - 1, 0), col))
        return pl.BlockSpec((None, rows, LANES), lambda b, blk: (b, blk, col))

    half = sinks.shape[0] // 2
    sink_rows = jnp.repeat(jnp.stack([sinks[:half], sinks[half:]], axis=1), BLOCK, axis=1)
    first_key = jnp.arange(2 * BLOCK)[None, None, :] == 0
    fill = jnp.where(first_key, sink_rows[:, :, None].astype(F32) * LOG2_E, NEG_INF)

    o = pl.pallas_call(
        _swa_kernel,
        grid=(batch, seq // rows),
        in_specs=[pl.BlockSpec(fill.shape, lambda b, blk: (0, 0, 0)),
                  pl.BlockSpec((None, rows, d), lambda b, blk: (b, blk, 0)),
                  kv_spec(0, True), kv_spec(0, False), kv_spec(1, True), kv_spec(1, False)],
        out_specs=pl.BlockSpec((None, rows, d), lambda b, blk: (b, blk, 0)),
        out_shape=jax.ShapeDtypeStruct((batch, seq, d), BF16),
        compiler_params=_params("parallel", "arbitrary"),
        name="swa_sink_attn",
    )(fill, qv, kvv, kvv, kvv, kvv)
    return o.reshape(batch * seq, d)


def _interleave(src_ref, dst_ref):
    dil, sub = src_ref.shape[0], src_ref.shape[1]
    chunks = dst_ref.shape[0]
    for r in range(dil):
        for c in range(chunks):
            part = src_ref[r, :, c * LANES:(c + 1) * LANES].astype(F32)
            dst_ref[c, pl.ds(r, sub, stride=dil), :] = part
    return jnp.concatenate([dst_ref[c] for c in range(chunks)], axis=1)


def _merge_ffn_kernel(x_ref, o0_ref, o1_ref, o2_ref, s0_ref, s1_ref, s2_ref, e_ref, w_ref, g_ref, w1_ref,
                      w3_ref, w2_ref, out_ref, ob_ref, sb1_ref, sb2_ref):
    s0, s1, s2 = s0_ref[0], _interleave(s1_ref, sb1_ref), _interleave(s2_ref, sb2_ref)
    top = jnp.maximum(jnp.maximum(s0, s1), s2)
    e0, e1, e2 = jnp.exp(s0 - top), jnp.exp(s1 - top), jnp.exp(s2 - top)
    inv = 1.0 / (e0 + e1 + e2)
    expand = e_ref[...]

    def weight(e):
        return jnp.dot((e * inv).astype(BF16), expand, preferred_element_type=F32)

    merged = weight(e0) * o0_ref[0].astype(F32)
    merged = merged + weight(e1) * _interleave(o1_ref, ob_ref)
    merged = merged + weight(e2) * _interleave(o2_ref, ob_ref)
    x = x_ref[...] + jnp.dot(merged.astype(BF16), w_ref[...], preferred_element_type=F32)
    out_ref[...] = x + _swiglu(_rms(x, g_ref[...]).astype(BF16), w1_ref, w3_ref, w2_ref)


def _merge_ffn(x, outs, stats, expand, w, g, w1, w3, w2, batch, seq):
    t, d = x.shape
    tiles_per_seq = seq // TM
    row = lambda i: (i, 0)
    fixed = lambda i: (0, 0)
    resident = lambda a: pl.BlockSpec(a.shape, fixed, pipeline_mode=pl.Buffered(1))

    def sub(arr):
        dil = arr.shape[1]
        return pl.BlockSpec((None, dil, TM // dil, arr.shape[3]),
                            lambda i: (i // tiles_per_seq, 0, i % tiles_per_seq, 0))

    return pl.pallas_call(
        _merge_ffn_kernel,
        grid=(t // TM,),
        in_specs=[pl.BlockSpec((TM, d), row)]
        + [sub(o) for o in outs] + [sub(s) for s in stats]
        + [resident(expand), resident(w), pl.BlockSpec((1, d), fixed),
           resident(w1), resident(w3), resident(w2)],
        out_specs=pl.BlockSpec((TM, d), row),
        out_shape=jax.ShapeDtypeStruct((t, d), F32),
        scratch_shapes=[pltpu.VMEM((d // LANES, TM, LANES), F32), pltpu.VMEM((1, TM, LANES), F32),
                        pltpu.VMEM((1, TM, LANES), F32)],
        compiler_params=_params("parallel"),
        name="merge_wo_swiglu",
    )(x, *outs, *stats, expand, w, g, w1, w3, w2)


def _wo_router_kernel(x_ref, o_ref, w_ref, g_ref, wr_ref, tri_ref, x1_ref, x1t_ref, route_ref, gates_ref,
                      counts_ref, run_ref):
    @pl.when(pl.program_id(0) == 0)
    def _():
        run_ref[...] = jnp.zeros_like(run_ref)

    x1 = x_ref[...] + jnp.dot(o_ref[...], w_ref[...], preferred_element_type=F32)
    x1_ref[...] = x1
    _to_token_tiles(x1, x1t_ref)
    h = _rms(x1, g_ref[...]).astype(BF16)
    logits = jnp.dot(h, wr_ref[...], preferred_element_type=F32)
    lane = lax.broadcasted_iota(jnp.int32, logits.shape, 1)
    logits = jnp.where(lane < N_EXPERTS, logits, -jnp.inf)
    v1 = jnp.max(logits, axis=-1, keepdims=True)
    i1 = jnp.min(jnp.where(logits == v1, lane, LANES), axis=-1, keepdims=True)
    rest = jnp.where(lane == i1, -jnp.inf, logits)
    v2 = jnp.max(rest, axis=-1, keepdims=True)
    i2 = jnp.min(jnp.where(rest == v2, lane, LANES), axis=-1, keepdims=True)
    e2 = jnp.exp(v2 - v1)
    inv = 1.0 / (1.0 + e2)
    gates_ref[:, :LANES] = jnp.broadcast_to(inv, logits.shape)
    gates_ref[:, LANES:] = jnp.broadcast_to(e2 * inv, logits.shape)
    picked = jnp.where(lane == i1, 1.0, 0.0) + jnp.where(lane == i2, 1.0, 0.0)
    before = run_ref[0:1, :] + jnp.dot(tri_ref[...], picked.astype(BF16), preferred_element_type=F32)
    rank1 = jnp.sum(jnp.where(lane == i1, before, 0.0), axis=-1, keepdims=True)
    rank2 = jnp.sum(jnp.where(lane == i2, before, 0.0), axis=-1, keepdims=True)
    total = run_ref[0:1, :] + jnp.sum(picked, axis=0, keepdims=True)
    run_ref[0:1, :] = total
    counts_ref[...] = jnp.broadcast_to(total, counts_ref.shape)
    info = jnp.where(lane == 0, i1.astype(F32), 0.0)
    info = jnp.where(lane == 1, i2.astype(F32), info)
    info = jnp.where(lane == 2, rank1, info)
    info = jnp.where(lane == 3, rank2, info)
    route_ref[...] = info.T[:SUBLANES, :]


def _wo_router(x, o, w, g, wr):
    t, d = x.shape
    row = lambda i: (i, 0)
    fixed = lambda i: (0, 0)
    tri = jnp.asarray(np.tril(np.ones((TM, TM), np.float32), -1), BF16)
    return pl.pallas_call(
        _wo_router_kernel,
        grid=(t // TM,),
        in_specs=[pl.BlockSpec((TM, d), row), pl.BlockSpec((TM, o.shape[1]), row),
                  pl.BlockSpec(w.shape, fixed), pl.BlockSpec((1, d), fixed),
                  pl.BlockSpec(wr.shape, fixed), pl.BlockSpec(tri.shape, fixed)],
        out_specs=[pl.BlockSpec((TM, d), row), pl.BlockSpec((TM * SUBLANES, LANES), row),
                   pl.BlockSpec((SUBLANES, TM), lambda i: (0, i)), pl.BlockSpec((TM, 2 * LANES), row),
                   pl.BlockSpec((SUBLANES, LANES), fixed)],
        out_shape=[jax.ShapeDtypeStruct((t, d), F32),
                   jax.ShapeDtypeStruct((t * SUBLANES, LANES), F32),
                   jax.ShapeDtypeStruct((SUBLANES, t), F32),
                   jax.ShapeDtypeStruct((t, 2 * LANES), F32),
                   jax.ShapeDtypeStruct((SUBLANES, LANES), F32)],
        scratch_shapes=[pltpu.VMEM((SUBLANES, LANES), F32)],
        compiler_params=_params("arbitrary"),
        name="wo_router",
    )(x, o, w, g, wr, tri)


def _swiglu(h, w1_ref, w3_ref, w2_ref):
    width = w1_ref.shape[1] // SWIGLU_SPLITS
    y = None
    for c in range(SWIGLU_SPLITS):
        sl = slice(c * width, (c + 1) * width)
        a = jnp.dot(h, w1_ref[:, sl], preferred_element_type=F32)
        b = jnp.dot(h, w3_ref[:, sl], preferred_element_type=F32)
        act = (a * jax.nn.sigmoid(a) * b).astype(BF16)
        part = jnp.dot(act, w2_ref[sl, :], preferred_element_type=F32)
        y = part if y is None else y + part
    return y


def _moe_kernel(te_ref, nu_ref, xs_ref, g_ref, w1_ref, w3_ref, w2_ref, o_ref):
    used = pl.program_id(0) < nu_ref[0]

    @pl.when(used)
    def _():
        x = _from_token_tiles(xs_ref, xs_ref.shape[0] // SUBLANES)
        _to_token_tiles(_swiglu(_rms(x, g_ref[...]).astype(BF16), w1_ref, w3_ref, w2_ref), o_ref)

    @pl.when(jnp.logical_not(used))
    def _():
        o_ref[...] = jnp.zeros_like(o_ref)


def _moe_experts(xs, g, w1, w3, w2, tile_expert, n_used, n_tiles):
    d, f = w1.shape[1], w1.shape[2]
    lines = TM * SUBLANES
    resident = pl.Buffered(1)
    grid_spec = pltpu.PrefetchScalarGridSpec(
        num_scalar_prefetch=2,
        grid=(n_tiles,),
        in_specs=[pl.BlockSpec((lines, LANES), lambda i, te, nu: (i, 0)),
                  pl.BlockSpec((1, d), lambda i, te, nu: (0, 0)),
                  pl.BlockSpec((None, d, f), lambda i, te, nu: (te[i], 0, 0), pipeline_mode=resident),
                  pl.BlockSpec((None, d, f), lambda i, te, nu: (te[i], 0, 0), pipeline_mode=resident),
                  pl.BlockSpec((None, f, d), lambda i, te, nu: (te[i], 0, 0))],
        out_specs=pl.BlockSpec((lines, LANES), lambda i, te, nu: (i, 0)),
    )
    return pl.pallas_call(
        _moe_kernel,
        grid_spec=grid_spec,
        out_shape=jax.ShapeDtypeStruct((n_tiles * lines, LANES), F32),
        compiler_params=_params("arbitrary"),
        name="moe_experts",
    )(tile_expert, n_used, xs, g, w1, w3, w2)


def _dispatch_kernel(dst0_ref, dst1_ref, pad_ref, src_ref, zero_ref, out_hbm, sem):
    rows = dst0_ref.shape[2]
    n_pad = pad_ref.shape[0]

    def issue(r, carry):
        base = r * DMA_UNROLL
        idx = [[ref[0, 0, base + q] for q in range(DMA_UNROLL)] for ref in (dst0_ref, dst1_ref)]
        for u in range(DMA_UNROLL):
            src = _tile_at(src_ref, (base + u) * SUBLANES)
            for k in range(2):
                pltpu.make_async_copy(src, _tile_at(out_hbm, idx[k][u]), sem).start(priority=k)
        return carry

    lax.fori_loop(0, rows // DMA_UNROLL, issue, 0)
    lines = 2 * rows * SUBLANES
    pltpu.make_async_copy(out_hbm.at[pl.ds(0, lines)], out_hbm.at[pl.ds(0, lines)], sem).wait()

    @pl.when(pl.program_id(0) == 0)
    def _():
        def fill(r, carry):
            base = r * DMA_UNROLL
            idx = [pad_ref[base + q] for q in range(DMA_UNROLL)]
            for q in range(DMA_UNROLL):
                pltpu.make_async_copy(zero_ref, _tile_at(out_hbm, idx[q]), sem).start()
            return carry

        lax.fori_loop(0, n_pad // DMA_UNROLL, fill, 0)
        pad_lines = n_pad * SUBLANES
        pltpu.make_async_copy(out_hbm.at[pl.ds(0, pad_lines)], out_hbm.at[pl.ds(0, pad_lines)],
                              sem).wait()


def _dispatch(x_tiles, dest_line, pad_line, n_slots):
    t = x_tiles.shape[0] // SUBLANES
    steps = t // DISPATCH_ROWS
    zero = jnp.zeros((SUBLANES, LANES), x_tiles.dtype)
    idx = dest_line.reshape(2 * steps, 1, DISPATCH_ROWS)
    idx_spec = lambda pick: pl.BlockSpec((1, 1, DISPATCH_ROWS), lambda i: (pick * steps + i, 0, 0),
                                         memory_space=pltpu.SMEM)
    return pl.pallas_call(
        _dispatch_kernel,
        grid=(steps,),
        in_specs=[idx_spec(0), idx_spec(1),
                  pl.BlockSpec(memory_space=pltpu.SMEM),
                  pl.BlockSpec((DISPATCH_ROWS * SUBLANES, LANES), lambda i: (i, 0)),
                  pl.BlockSpec((SUBLANES, LANES), lambda i: (0, 0))],
        out_specs=pl.BlockSpec(memory_space=pl.ANY),
        out_shape=jax.ShapeDtypeStruct((n_slots * SUBLANES, LANES), x_tiles.dtype),
        scratch_shapes=[pltpu.SemaphoreType.DMA(())],
        compiler_params=_params("arbitrary"),
        name="moe_dispatch",
    )(idx, idx, pad_line, x_tiles, zero)


def _ple_rows(x, p, g_ref, wg_ref, wp_ref, between=None):
    h = _rms(x, g_ref[...]).astype(BF16)
    gate = []
    for c in range(wg_ref.shape[1] // MXU_COLS):
        if between is not None:
            between(c)
        logits = jnp.dot(h, wg_ref[:, c * MXU_COLS:(c + 1) * MXU_COLS], preferred_element_type=F32)
        gate.append(jax.nn.sigmoid(logits))
    proj = jnp.dot(p.astype(BF16), wp_ref[...], preferred_element_type=F32)
    return x + proj * jnp.concatenate(gate, axis=1)


def _combine_ple_kernel(cur0_ref, cur1_ref, nxt0_ref, nxt1_ref, x_ref, gates_ref, p_ref, g_ref, wg_ref,
                        wp_ref, fg_ref, ys_hbm, o_ref, a_ref, b_ref, sem):
    i = pl.program_id(0)
    last = pl.num_programs(0) - 1
    tm = a_ref.shape[1] // SUBLANES
    groups = tm // DMA_UNROLL
    cur_ref, nxt_ref = (cur0_ref, cur1_ref), (nxt0_ref, nxt1_ref)

    def issue_group(idx_ref, offset, slot, base):
        idx = [[ref[0, 0, offset + base + q] for q in range(DMA_UNROLL)] for ref in idx_ref]
        for u in range(DMA_UNROLL):
            line = (base + u) * SUBLANES
            pltpu.make_async_copy(_tile_at(ys_hbm, idx[0][u]), _tile_at(a_ref.at[slot], line),
                                  sem.at[slot]).start(priority=0)
            pltpu.make_async_copy(_tile_at(ys_hbm, idx[1][u]), _tile_at(b_ref.at[slot], line),
                                  sem.at[slot]).start(priority=1)

    def wait(slot):
        whole = ys_hbm.at[pl.ds(0, tm * SUBLANES)]
        pltpu.make_async_copy(whole, a_ref.at[slot], sem.at[slot]).wait()
        pltpu.make_async_copy(whole, b_ref.at[slot], sem.at[slot]).wait()

    def compute(slot, idx_ref, offset):
        rows = slice(slot * tm, (slot + 1) * tm)
        g1, g2 = gates_ref[rows, :LANES], gates_ref[rows, LANES:]
        y = jnp.concatenate(
            [g1 * a_ref[slot, pl.ds(c, tm, stride=SUBLANES), :]
             + g2 * b_ref[slot, pl.ds(c, tm, stride=SUBLANES), :] for c in range(SUBLANES)], axis=1)
        chunks = wg_ref.shape[1] // MXU_COLS

        def between(c):
            for grp in range(c * groups // chunks, (c + 1) * groups // chunks):
                issue_group(idx_ref, offset, 1 - slot, grp * DMA_UNROLL)

        x = _ple_rows(x_ref[rows, :] + y, p_ref[rows, :], g_ref, wg_ref, wp_ref, between)
        o_ref[rows, :] = _rms(x, fg_ref[...])

    @pl.when(i == 0)
    def _():
        lax.fori_loop(0, groups, lambda r, c: (issue_group(cur_ref, 0, 0, r * DMA_UNROLL), c)[1], 0)

    wait(0)
    compute(0, cur_ref, tm)
    wait(1)
    compute(1, nxt_ref, 0)

    @pl.when(i == last)
    def _():
        wait(0)


def _combine_ple(x, gates, ys, dest_line, p, layer, g, wg, wp, fg):
    t, d = x.shape
    tm = TM_COMBINE
    steps = t // (2 * tm)
    row = lambda i: (i, 0)
    fixed = lambda i: (0, 0)
    idx = dest_line.reshape(2 * steps, 1, 2 * tm)
    idx_spec = lambda index: pl.BlockSpec((1, 1, 2 * tm), index, memory_space=pltpu.SMEM)
    nxt = lambda i: jnp.minimum(i + 1, steps - 1)
    return pl.pallas_call(
        _combine_ple_kernel,
        grid=(steps,),
        in_specs=[idx_spec(lambda i: (i, 0, 0)), idx_spec(lambda i: (steps + i, 0, 0)),
                  idx_spec(lambda i: (nxt(i), 0, 0)), idx_spec(lambda i: (steps + nxt(i), 0, 0)),
                  pl.BlockSpec((2 * tm, d), row),
                  pl.BlockSpec((2 * tm, 2 * LANES), row),
                  pl.BlockSpec((None, 2 * tm, p.shape[2]), lambda i: (layer, i, 0)),
                  pl.BlockSpec((1, d), fixed),
                  pl.BlockSpec(wg.shape, fixed),
                  pl.BlockSpec(wp.shape, fixed),
                  pl.BlockSpec((1, d), fixed),
                  pl.BlockSpec(memory_space=pl.ANY)],
        out_specs=pl.BlockSpec((2 * tm, d), row),
        out_shape=jax.ShapeDtypeStruct((t, d), F32),
        scratch_shapes=[pltpu.VMEM((2, tm * SUBLANES, LANES), F32),
                        pltpu.VMEM((2, tm * SUBLANES, LANES), F32),
                        pltpu.SemaphoreType.DMA((2,))],
        compiler_params=_params("arbitrary"),
        name="combine_ple_final",
    )(idx, idx, idx, idx, x, gates, p, g, wg, wp, fg, ys)


def _routing_tables(route, counts_f):
    t = route.shape[1]
    picked = route[0:2].astype(jnp.int32)
    rank = route[2:4].astype(jnp.int32)
    counts = counts_f[0, :N_EXPERTS].astype(jnp.int32)
    tiles_per = (counts + TM - 1) // TM
    tile_end = jnp.cumsum(tiles_per)
    tile_start = tile_end - tiles_per
    first_slot = jnp.zeros_like(picked)
    for e in range(N_EXPERTS):
        first_slot = jnp.where(picked == e, tile_start[e] * TM, first_slot)
    dest = first_slot + rank
    n_tiles = (2 * t) // TM + N_EXPERTS
    n_used = tile_end[-1]
    tile_ids = jnp.minimum(jnp.arange(n_tiles), n_used - 1)
    tile_expert = jnp.minimum(jnp.sum((tile_ids[:, None] >= tile_end[None, :]).astype(jnp.int32), axis=1),
                              N_EXPERTS - 1)
    pad_sizes = jnp.concatenate([tiles_per * TM - counts, ((n_tiles - n_used) * TM).reshape(1)])
    pad_end = jnp.cumsum(pad_sizes)
    seg_first = jnp.concatenate([tile_start * TM + counts, (n_used * TM).reshape(1)])
    k = jnp.arange(N_EXPERTS * TM)
    seg = jnp.sum((k[:, None] >= pad_end[None, :]).astype(jnp.int32), axis=1)
    pad = seg_first[seg] + k - (pad_end - pad_sizes)[seg]
    to_line = lambda v: (v * SUBLANES).astype(jnp.int32)
    return (to_line(dest), to_line(pad), tile_expert.astype(jnp.int32),
            n_used.reshape(1).astype(jnp.int32), n_tiles)


def kernel(x, p, attn_norm, ffn_norm, a_w_qkv, a_w_o, kv_norm, kv_w, b_w_q, b_sinks, b_w_o,
           dense_w1, dense_w3, dense_w2, moe_router, moe_w1, moe_w3, moe_w2,
           ple_norm, ple_w_gate, ple_w_proj, final_norm):
    batch, seq, d = x.shape
    t = batch * seq
    n_heads = d // HEAD_DIM
    n_groups = len(A_CONFIGS)
    assert d == SUBLANES * LANES and seq % TM == 0 and t % DISPATCH_ROWS == 0
    xr = x.reshape(t, d)
    pr = p.reshape(p.shape[0], t, p.shape[3])
    cos, sin = _rope_tables(np.arange(seq))
    row = lambda v: v.reshape(1, -1)

    w = a_w_qkv[0].astype(BF16).reshape(d, n_groups, 3, d)
    coef_a, coef_b = _rope_coeffs(cos, sin)
    cast_jobs = ((moe_w1[0], dense_w1[0], a_w_o[0], b_w_q[0]),
                 (moe_w3[0], dense_w3[0], ple_w_gate),
                 (moe_w2[0], b_w_o[0]))
    def proj_part(g):
        w_g = jnp.concatenate([_pair_columns(w[:, g, 0], True), _pair_columns(w[:, g, 1], True),
                               w[:, g, 2]], axis=1)
        return _group_proj_part(xr, row(attn_norm[0]), w_g, coef_a, coef_b, batch, seq, A_CONFIGS[g][1],
                                cast_jobs[g])

    outs, stats, cast = [], [], []
    (qkv, *done), = _run_parts([proj_part(0)], t // TM, "group_proj_d1")
    cast.append(done)
    for g, (_, dil) in enumerate(A_CONFIGS):
        attn, steps = _dilated_attn_part(qkv, batch, seq, g)
        if g + 1 < n_groups:
            assert steps == t // TM
            (o_g, st_g), (qkv, *done) = _run_parts([attn, proj_part(g + 1)], steps, f"attn_g{g}_proj_g{g + 1}")
            cast.append(done)
        else:
            (o_g, st_g), = _run_parts([attn], steps, f"dilated_attn_g{g}")
        outs.append(o_g.reshape(batch, dil, seq // dil, d))
        stats.append(st_g.reshape(batch, dil, seq // dil, LANES))
    (moe1, dense1, wo0, wq1), (moe3, dense3, ple_gate), (moe2, wo1) = [
        [c.reshape(s.shape) for c, s in zip(done, srcs)] for done, srcs in zip(cast, cast_jobs)]
    head_of_col = jnp.arange(d) // HEAD_DIM
    expand = (jnp.arange(LANES)[:, None] == head_of_col[None, :]).astype(BF16)
    xr = _merge_ffn(xr, outs, stats, expand, wo0, row(ffn_norm[0]), dense1, dense3,
                    dense_w2[0].astype(BF16), batch, seq)

    half = n_heads // 2
    kv_cols = B_KV_HEADS * HEAD_DIM
    kv_bf = kv_w.astype(BF16)
    w_kv = jnp.concatenate([_pair_columns(kv_bf[:, :kv_cols], True), kv_bf[:, kv_cols:]], axis=1)
    xr, q1, kv1 = _ple_qkv1(xr, pr, 0, row(ple_norm[0]), ple_gate[0], ple_w_proj[0].astype(BF16),
                            row(attn_norm[1]), row(kv_norm), _pair_columns(wq1, False), w_kv, cos, sin)
    o1 = _swa_attention(q1, kv1, b_sinks[0], batch, seq)
    w_o1 = wo1.reshape(2, half, HEAD_DIM, d).transpose(1, 0, 2, 3).reshape(d, d)
    wr = jnp.pad(moe_router[0], ((0, 0), (0, LANES - N_EXPERTS))).astype(BF16)
    xr, x_tiles, route, gates, counts = _wo_router(xr, o1, w_o1, row(ffn_norm[1]), wr)

    dest_line, pad_line, tile_expert, n_used, n_tiles = _routing_tables(route, counts)
    xs = _dispatch(x_tiles, dest_line, pad_line, n_tiles * TM)
    ys = _moe_experts(xs, row(ffn_norm[1]), moe1, moe3, moe2, tile_expert, n_used, n_tiles)
    xr = _combine_ple(xr, gates, ys, dest_line, pr, 1, row(ple_norm[1]), ple_gate[1],
                      ple_w_proj[1].astype(BF16), row(final_norm))
    return xr.reshape(batch, seq, d)
```

```python
import functools

import numpy as np
import jax
import jax.numpy as jnp
from jax import lax
from jax.experimental import pallas as pl
from jax.experimental.pallas import tpu as pltpu

F32 = jnp.float32
BF16 = jnp.bfloat16

LANES = 128
SUBLANES = 8
MXU_COLS = 256
HEAD_DIM = 64
HALF_DIM = HEAD_DIM // 2
BLOCK = 128
Q_BLOCKS = 4
A_CONFIGS = ((128, 1), (512, 4), (2048, 16))
B_KV_HEADS = 2
B_WINDOW = 128
N_EXPERTS = 8
ROPE_THETA = 10000.0
EPS = 1e-6
NEG_INF = -1e30
VMEM_LIMIT = 56 * 1024 * 1024

TM = 512
TM_COMBINE = 256
DISPATCH_ROWS = 2048
DMA_UNROLL = 8
SWIGLU_SPLITS = 2

PLAIN, ROPE, ROPE_SCALED = 0, 1, 2
LOG2_E = 1.4426950408889634
LN_2 = 0.6931471805599453
Q_SCALE = HEAD_DIM ** -0.5 * LOG2_E


def _params(*sem):
    return pltpu.CompilerParams(dimension_semantics=sem, vmem_limit_bytes=VMEM_LIMIT)


def _rms(x, g):
    ms = jnp.mean(x * x, axis=-1, keepdims=True)
    return x * lax.rsqrt(ms + EPS) * g


def _pair_columns(w, adjacent):
    d, heads = w.shape[0], w.shape[1] // HEAD_DIM
    if adjacent:
        return w.reshape(d, heads // 2, 2, 2, HALF_DIM).transpose(0, 1, 3, 2, 4).reshape(w.shape)
    return w.reshape(d, 2, heads // 2, 2, HALF_DIM).transpose(0, 2, 3, 1, 4).reshape(w.shape)


def _rope_tables(positions):
    pos = jnp.asarray(positions, F32)
    inv = 1.0 / (ROPE_THETA ** (jnp.arange(HALF_DIM, dtype=F32) / HALF_DIM))
    ang = pos[:, None] * inv[None, :]
    cos, sin = jnp.cos(ang), jnp.sin(ang)
    return jnp.tile(cos, (1, 4)), jnp.concatenate([-sin, -sin, sin, sin], axis=1)


def _rope_coeffs(cos, sin):
    a = jnp.stack([cos * Q_SCALE, cos, jnp.ones_like(cos)])
    b = jnp.stack([sin * Q_SCALE, sin, jnp.zeros_like(sin)])
    return a, b


def _store_chunks(acc, cos, sin, o_ref, kinds):
    for c, kind in enumerate(kinds):
        seg = acc[:, c * LANES:(c + 1) * LANES]
        if kind != PLAIN:
            seg = seg * cos + pltpu.roll(seg, LANES // 2, 1) * sin
            if kind == ROPE_SCALED:
                seg = seg * Q_SCALE
        o_ref[:, c * LANES:(c + 1) * LANES] = seg.astype(o_ref.dtype)


def _to_token_tiles(val, ref):
    rows = val.shape[0]
    for c in range(val.shape[1] // LANES):
        ref[pl.ds(c, rows, stride=SUBLANES), :] = val[:, c * LANES:(c + 1) * LANES]


def _from_token_tiles(ref, rows):
    return jnp.concatenate([ref[pl.ds(c, rows, stride=SUBLANES), :] for c in range(SUBLANES)], axis=1)


def _tile_at(ref, line):
    return ref.at[pl.ds(pl.multiple_of(line, SUBLANES), SUBLANES)]


class _Part:
    def __init__(self, body, inputs, in_specs, out_specs, out_shapes):
        self.body, self.inputs, self.in_specs = body, list(inputs), list(in_specs)
        self.out_specs, self.out_shapes = list(out_specs), list(out_shapes)


def _run_parts(parts, steps, name):
    n_in = [len(p.in_specs) for p in parts]
    n_out = [len(p.out_specs) for p in parts]

    def kernel(*refs):
        ins, outs = refs[:sum(n_in)], refs[sum(n_in):]
        for k, part in enumerate(parts):
            part.body(ins[sum(n_in[:k]):sum(n_in[:k + 1])], outs[sum(n_out[:k]):sum(n_out[:k + 1])])

    res = pl.pallas_call(
        kernel,
        grid=(steps,),
        in_specs=[s for p in parts for s in p.in_specs],
        out_specs=[s for p in parts for s in p.out_specs],
        out_shape=[s for p in parts for s in p.out_shapes],
        compiler_params=_params("parallel"),
        name=name,
    )(*[a for p in parts for a in p.inputs])
    return [res[sum(n_out[:k]):sum(n_out[:k + 1])] for k in range(len(parts))]


def _group_proj_body(ins, outs, *, dil):
    x_ref, g_ref, w_ref, a_ref, b_ref, perm_ref, *cast_refs = ins
    o_ref, *cast_out_refs = outs
    for src_ref, dst_ref in zip(cast_refs, cast_out_refs):
        dst_ref[...] = src_ref[...].astype(dst_ref.dtype)
    h = _rms(x_ref[...], g_ref[...]).astype(BF16)
    if dil > 1:
        h = jnp.dot(perm_ref[...], h, preferred_element_type=F32).astype(BF16)
    sub = h.shape[0] // dil
    tn = w_ref.shape[1] // 3
    for kind in range(3):
        acc = jnp.dot(h, w_ref[:, kind * tn:(kind + 1) * tn], preferred_element_type=F32)
        a, b = a_ref[kind], b_ref[kind]
        for c in range(tn // LANES):
            seg = acc[:, c * LANES:(c + 1) * LANES]
            seg = (seg * a + pltpu.roll(seg, LANES // 2, 1) * b).astype(o_ref.dtype)
            cols = slice(kind * tn + c * LANES, kind * tn + (c + 1) * LANES)
            if dil == 1:
                o_ref[:, cols] = seg
            else:
                for r in range(dil):
                    o_ref[r, :, cols] = seg[r * sub:(r + 1) * sub]


def _residue_major_rows(dil):
    q = np.arange(TM)
    sub = TM // dil
    return (q % sub) * dil + q // sub


def _group_proj_part(x, g, w, coef_a, coef_b, batch, seq, dil, cast_srcs):
    t, d = x.shape
    n_out = w.shape[1]
    tiles_per_seq = seq // TM
    steps = t // TM
    cast2d = [c.reshape(-1, c.shape[-1]) for c in cast_srcs]
    for c in cast2d:
        assert c.shape[0] % (steps * 2 * SUBLANES) == 0, c.shape
    cast_specs = [pl.BlockSpec((c.shape[0] // steps, c.shape[1]), lambda i: (i, 0)) for c in cast2d]
    if dil == 1:
        out_spec = pl.BlockSpec((TM, n_out), lambda i: (i, 0))
        out_shape = jax.ShapeDtypeStruct((t, n_out), BF16)
    else:
        out_spec = pl.BlockSpec((None, dil, TM // dil, n_out),
                                lambda i: (i // tiles_per_seq, 0, i % tiles_per_seq, 0))
        out_shape = jax.ShapeDtypeStruct((batch, dil, seq // dil, n_out), BF16)
    nat = _residue_major_rows(dil)
    perm = jnp.asarray(nat[:, None] == np.arange(TM)[None, :], BF16)
    if dil > 1:
        tile_order = lambda c: c.reshape(3, seq // TM, TM // dil, dil, LANES).transpose(0, 1, 3, 2, 4) \
            .reshape(3, seq, LANES)
        coef_a, coef_b = tile_order(coef_a), tile_order(coef_b)
    coef_spec = pl.BlockSpec((3, TM, LANES), lambda i: (0, i % tiles_per_seq, 0))
    return _Part(
        functools.partial(_group_proj_body, dil=dil),
        [x, g, w, coef_a, coef_b, perm, *cast2d],
        [pl.BlockSpec((TM, d), lambda i: (i, 0)),
         pl.BlockSpec((1, d), lambda i: (0, 0)),
         pl.BlockSpec(w.shape, lambda i: (0, 0), pipeline_mode=pl.Buffered(1)),
         coef_spec, coef_spec,
         pl.BlockSpec((TM, TM), lambda i: (0, 0), pipeline_mode=pl.Buffered(1))] + cast_specs,
        [out_spec] + cast_specs,
        [out_shape] + [jax.ShapeDtypeStruct(c.shape, BF16) for c in cast2d])


def _ple_qkv1_kernel(x_ref, p_ref, g_ref, wg_ref, wp_ref, gq_ref, gkv_ref, wq_ref, wkv_ref, cos_ref, sin_ref,
                     x_out_ref, q_ref, kv_ref):
    x = _ple_rows(x_ref[...], p_ref[...], g_ref, wg_ref, wp_ref)
    x_out_ref[...] = x
    xn = x * lax.rsqrt(jnp.mean(x * x, axis=-1, keepdims=True) + EPS)
    hq = (xn * gq_ref[...]).astype(BF16)
    hkv = (xn * gkv_ref[...]).astype(BF16)
    cos, sin = cos_ref[...], sin_ref[...]
    accq = jnp.dot(hq, wq_ref[...], preferred_element_type=F32)
    _store_chunks(accq, cos, sin, q_ref, (ROPE_SCALED,) * (accq.shape[1] // LANES))
    acckv = jnp.dot(hkv, wkv_ref[...], preferred_element_type=F32)
    _store_chunks(acckv, cos, sin, kv_ref, (ROPE, PLAIN))


def _ple_qkv1(x, p, layer, g, wg, wp, gq, gkv, wq, wkv, cos, sin):
    t, d = x.shape
    s_tiles = cos.shape[0] // TM
    row = lambda i: (i, 0)
    fixed = lambda i: (0, 0)
    return pl.pallas_call(
        _ple_qkv1_kernel,
        grid=(t // TM,),
        in_specs=[
            pl.BlockSpec((TM, d), row),
            pl.BlockSpec((None, TM, p.shape[2]), lambda i: (layer, i, 0)),
            pl.BlockSpec((1, d), fixed),
            pl.BlockSpec(wg.shape, fixed),
            pl.BlockSpec(wp.shape, fixed),
            pl.BlockSpec((1, d), fixed),
            pl.BlockSpec((1, d), fixed),
            pl.BlockSpec(wq.shape, fixed),
            pl.BlockSpec(wkv.shape, fixed),
            pl.BlockSpec((TM, LANES), lambda i: (i % s_tiles, 0)),
            pl.BlockSpec((TM, LANES), lambda i: (i % s_tiles, 0)),
        ],
        out_specs=[pl.BlockSpec((TM, d), row), pl.BlockSpec((TM, wq.shape[1]), row),
                   pl.BlockSpec((TM, wkv.shape[1]), row)],
        out_shape=[jax.ShapeDtypeStruct((t, d), F32),
                   jax.ShapeDtypeStruct((t, wq.shape[1]), BF16),
                   jax.ShapeDtypeStruct((t, wkv.shape[1]), BF16)],
        compiler_params=_params("parallel"),
        name="ple_qkv_layer1",
    )(x, p, g, wg, wp, gq, gkv, wq, wkv, cos, sin)


def _band_mask(has_prev, max_dist):
    qi = lax.broadcasted_iota(jnp.int32, (BLOCK, 2 * BLOCK), 0) + BLOCK
    kj = lax.broadcasted_iota(jnp.int32, (BLOCK, 2 * BLOCK), 1)
    rel = qi - kj
    mask = (rel >= 0) & (rel <= max_dist)
    if has_prev is not True:
        mask = mask & ((kj >= BLOCK) | has_prev)
    return jnp.concatenate([mask, mask], axis=0)


def _band_blocks(qb, prev_ref, cur_ref, lanes):
    own = cur_ref[qb * BLOCK:(qb + 1) * BLOCK, lanes]
    before = prev_ref[:, lanes] if qb == 0 else cur_ref[(qb - 1) * BLOCK:qb * BLOCK, lanes]
    return jnp.concatenate([before, own], axis=0)


def _lane_masks():
    lane = lax.broadcasted_iota(jnp.int32, (BLOCK, LANES), 1)
    first_qk = (lane % HEAD_DIM) < HALF_DIM
    sel_a = jnp.where(first_qk, 1.0, 0.0).astype(BF16)
    sel_b = jnp.where(first_qk, 0.0, 1.0).astype(BF16)
    return lane, sel_a, sel_b, lane < HEAD_DIM


def _pair_scores(q2, kcat, sel_a, sel_b):
    qs = jnp.concatenate([q2 * sel_a, q2 * sel_b], axis=0)
    return lax.dot_general(qs, kcat, (((1,), (1,)), ((), ())), preferred_element_type=F32)


def _dil_attn_body(ins, outs, *, steps, steps_per_seq):
    q_ref, kp_ref, kc_ref, vp_ref, vc_ref = ins
    o_ref, st_ref = outs
    lane, sel_a, sel_b, first_v = _lane_masks()
    for qb in range(q_ref.shape[0] // BLOCK):
        rows = slice(qb * BLOCK, (qb + 1) * BLOCK)
        mask2 = _band_mask(True if qb else pl.program_id(0) % steps_per_seq > 0, steps)
        stats = jnp.zeros((BLOCK, LANES), F32)
        for p in range(q_ref.shape[1] // LANES):
            sl = slice(p * LANES, (p + 1) * LANES)
            kcat = _band_blocks(qb, kp_ref, kc_ref, sl)
            vcat = _band_blocks(qb, vp_ref, vc_ref, sl)
            s = jnp.where(mask2, _pair_scores(q_ref[rows, sl], kcat, sel_a, sel_b), NEG_INF)
            m = jnp.max(s, axis=-1, keepdims=True)
            pe = jnp.exp2(s - m)
            den = jnp.sum(pe, axis=-1, keepdims=True)
            pv = jnp.dot(pe.astype(BF16), vcat, preferred_element_type=F32)
            pv = pv * (1.0 / den)
            o_ref[rows, sl] = jnp.where(first_v, pv[:BLOCK], pv[BLOCK:]).astype(o_ref.dtype)
            lse = m * LN_2 + jnp.log(den)
            stats = jnp.where(lane == 2 * p, lse[:BLOCK], stats)
            stats = jnp.where(lane == 2 * p + 1, lse[BLOCK:], stats)
        st_ref[rows, :] = stats


def _dilated_attn_part(qkv, batch, seq, g):
    win, dil = A_CONFIGS[g]
    n = seq // dil
    d = qkv.shape[-1] // 3
    view = qkv.reshape(batch * dil, n, 3 * d)

    q_blocks = min(Q_BLOCKS, n // BLOCK)
    rows = q_blocks * BLOCK
    per_seq = n // rows

    def spec(kind, prev):
        if prev:
            return pl.BlockSpec((None, BLOCK, d), lambda i: (
                i // per_seq, jnp.maximum((i % per_seq) * q_blocks - 1, 0), kind))
        return pl.BlockSpec((None, rows, d), lambda i: (i // per_seq, i % per_seq, kind))

    out_index = lambda i: (i // per_seq, i % per_seq, 0)
    part = _Part(
        functools.partial(_dil_attn_body, steps=win // dil, steps_per_seq=per_seq),
        [view] * 5,
        [spec(0, False), spec(1, True), spec(1, False), spec(2, True), spec(2, False)],
        [pl.BlockSpec((None, rows, d), out_index), pl.BlockSpec((None, rows, LANES), out_index)],
        [jax.ShapeDtypeStruct((batch * dil, n, d), BF16),
         jax.ShapeDtypeStruct((batch * dil, n, LANES), F32)])
    return part, batch * dil * per_seq


def _swa_blocks(fill_ref, q_ref, kp_ref, kc_ref, vp_ref, vc_ref, o_ref, has_prev, after_block):
    _, sel_a, sel_b, first_v = _lane_masks()
    all_lanes = slice(0, LANES)
    key_row = lax.broadcasted_iota(jnp.int32, (2 * BLOCK, LANES), 0)
    for qb in range(q_ref.shape[0] // BLOCK):
        rows = slice(qb * BLOCK, (qb + 1) * BLOCK)
        mask2 = _band_mask(True if qb else has_prev, B_WINDOW - 1)
        kcat = _band_blocks(qb, kp_ref, kc_ref, all_lanes)
        vcat = _band_blocks(qb, vp_ref, vc_ref, all_lanes)
        vcat = jnp.where(key_row == 0, jnp.zeros_like(vcat), vcat)
        for j in range(q_ref.shape[1] // LANES):
            sl = slice(j * LANES, (j + 1) * LANES)
            s = jnp.where(mask2, _pair_scores(q_ref[rows, sl], kcat, sel_a, sel_b), fill_ref[j])
            m = jnp.max(s, axis=-1, keepdims=True)
            pe = jnp.exp2(s - m)
            den = jnp.sum(pe, axis=-1, keepdims=True)
            pv = jnp.dot(pe.astype(BF16), vcat, preferred_element_type=F32)
            pv = pv * (1.0 / den)
            o_ref[rows, sl] = jnp.where(first_v, pv[:BLOCK], pv[BLOCK:]).astype(o_ref.dtype)
        after_block(rows)


def _sink_fill(sinks):
    half = sinks.shape[0] // 2
    sink_rows = jnp.repeat(jnp.stack([sinks[:half], sinks[half:]], axis=1), BLOCK, axis=1)
    first_key = jnp.arange(2 * BLOCK)[None, None, :] == 0
    return jnp.where(first_key, sink_rows[:, :, None].astype(F32) * LOG2_E, NEG_INF)


def _interleave(src_ref, dst_ref):
    dil, sub = src_ref.shape[0], src_ref.shape[1]
    chunks = dst_ref.shape[0]
    for r in range(dil):
        for c in range(chunks):
            part = src_ref[r, :, c * LANES:(c + 1) * LANES].astype(F32)
            dst_ref[c, pl.ds(r, sub, stride=dil), :] = part
    return jnp.concatenate([dst_ref[c] for c in range(chunks)], axis=1)


def _merge_ffn_kernel(x_ref, o0_ref, o1_ref, o2_ref, s0_ref, s1_ref, s2_ref, e_ref, w_ref, g_ref, w1_ref,
                      w3_ref, w2_ref, out_ref, ob_ref, sb1_ref, sb2_ref):
    s0, s1, s2 = s0_ref[0], _interleave(s1_ref, sb1_ref), _interleave(s2_ref, sb2_ref)
    top = jnp.maximum(jnp.maximum(s0, s1), s2)
    e0, e1, e2 = jnp.exp(s0 - top), jnp.exp(s1 - top), jnp.exp(s2 - top)
    inv = 1.0 / (e0 + e1 + e2)
    expand = e_ref[...]

    def weight(e):
        return jnp.dot((e * inv).astype(BF16), expand, preferred_element_type=F32)

    merged = weight(e0) * o0_ref[0].astype(F32)
    merged = merged + weight(e1) * _interleave(o1_ref, ob_ref)
    merged = merged + weight(e2) * _interleave(o2_ref, ob_ref)
    x = x_ref[...] + jnp.dot(merged.astype(BF16), w_ref[...], preferred_element_type=F32)
    out_ref[...] = x + _swiglu(_rms(x, g_ref[...]).astype(BF16), w1_ref, w3_ref, w2_ref)


def _merge_ffn(x, outs, stats, expand, w, g, w1, w3, w2, batch, seq):
    t, d = x.shape
    tiles_per_seq = seq // TM
    row = lambda i: (i, 0)
    fixed = lambda i: (0, 0)
    resident = lambda a: pl.BlockSpec(a.shape, fixed, pipeline_mode=pl.Buffered(1))

    def sub(arr):
        dil = arr.shape[1]
        return pl.BlockSpec((None, dil, TM // dil, arr.shape[3]),
                            lambda i: (i // tiles_per_seq, 0, i % tiles_per_seq, 0))

    return pl.pallas_call(
        _merge_ffn_kernel,
        grid=(t // TM,),
        in_specs=[pl.BlockSpec((TM, d), row)]
        + [sub(o) for o in outs] + [sub(s) for s in stats]
        + [resident(expand), resident(w), pl.BlockSpec((1, d), fixed),
           resident(w1), resident(w3), resident(w2)],
        out_specs=pl.BlockSpec((TM, d), row),
        out_shape=jax.ShapeDtypeStruct((t, d), F32),
        scratch_shapes=[pltpu.VMEM((d // LANES, TM, LANES), F32), pltpu.VMEM((1, TM, LANES), F32),
                        pltpu.VMEM((1, TM, LANES), F32)],
        compiler_params=_params("parallel"),
        name="merge_wo_swiglu",
    )(x, *outs, *stats, expand, w, g, w1, w3, w2)


def _swa_wo_router_kernel(fill_ref, q_ref, kp_ref, kc_ref, vp_ref, vc_ref, x_ref, w_ref, g_ref, wr_ref,
                          tri_ref, x1_ref, x1t_ref, route_ref, gates_ref, counts_ref, o_ref, run_ref,
                          *, steps_per_seq):
    @pl.when(pl.program_id(0) == 0)
    def _():
        run_ref[...] = jnp.zeros_like(run_ref)

    def project(rows):
        if rows.stop % MXU_COLS == 0:
            rows = slice(rows.stop - MXU_COLS, rows.stop)
            x1_ref[rows, :] = x_ref[rows, :] + jnp.dot(o_ref[rows, :], w_ref[...],
                                                       preferred_element_type=F32)

    _swa_blocks(fill_ref, q_ref, kp_ref, kc_ref, vp_ref, vc_ref, o_ref,
                pl.program_id(0) % steps_per_seq > 0, project)
    x1 = x1_ref[...]
    _to_token_tiles(x1, x1t_ref)
    h = _rms(x1, g_ref[...]).astype(BF16)
    logits = jnp.dot(h, wr_ref[...], preferred_element_type=F32)
    lane = lax.broadcasted_iota(jnp.int32, logits.shape, 1)
    logits = jnp.where(lane < N_EXPERTS, logits, -jnp.inf)
    v1 = jnp.max(logits, axis=-1, keepdims=True)
    i1 = jnp.min(jnp.where(logits == v1, lane, LANES), axis=-1, keepdims=True)
    rest = jnp.where(lane == i1, -jnp.inf, logits)
    v2 = jnp.max(rest, axis=-1, keepdims=True)
    i2 = jnp.min(jnp.where(rest == v2, lane, LANES), axis=-1, keepdims=True)
    e2 = jnp.exp(v2 - v1)
    inv = 1.0 / (1.0 + e2)
    gates_ref[:, :LANES] = jnp.broadcast_to(inv, logits.shape)
    gates_ref[:, LANES:] = jnp.broadcast_to(e2 * inv, logits.shape)
    picked = jnp.where(lane == i1, 1.0, 0.0) + jnp.where(lane == i2, 1.0, 0.0)
    before = run_ref[0:1, :] + jnp.dot(tri_ref[...], picked.astype(BF16), preferred_element_type=F32)
    rank1 = jnp.sum(jnp.where(lane == i1, before, 0.0), axis=-1, keepdims=True)
    rank2 = jnp.sum(jnp.where(lane == i2, before, 0.0), axis=-1, keepdims=True)
    total = run_ref[0:1, :] + jnp.sum(picked, axis=0, keepdims=True)
    run_ref[0:1, :] = total
    counts_ref[...] = jnp.broadcast_to(total, counts_ref.shape)
    info = jnp.where(lane == 0, i1.astype(F32), 0.0)
    info = jnp.where(lane == 1, i2.astype(F32), info)
    info = jnp.where(lane == 2, rank1, info)
    info = jnp.where(lane == 3, rank2, info)
    route_ref[...] = info.T[:SUBLANES, :]


def _swa_wo_router(q, kv, sinks, x, w, g, wr, seq):
    t, d = x.shape
    assert TM % BLOCK == 0 and seq % TM == 0
    per_seq = seq // TM
    blocks = TM // BLOCK
    row = lambda i: (i, 0)
    fixed = lambda i: (0, 0)
    tri = jnp.asarray(np.tril(np.ones((TM, TM), np.float32), -1), BF16)
    fill = _sink_fill(sinks)

    def kv_spec(col, prev):
        if prev:
            return pl.BlockSpec((BLOCK, LANES), lambda i: (jnp.maximum(i * blocks - 1, 0), col))
        return pl.BlockSpec((TM, LANES), lambda i: (i, col))

    return pl.pallas_call(
        functools.partial(_swa_wo_router_kernel, steps_per_seq=per_seq),
        grid=(t // TM,),
        in_specs=[pl.BlockSpec(fill.shape, lambda i: (0, 0, 0)),
                  pl.BlockSpec((TM, d), row),
                  kv_spec(0, True), kv_spec(0, False), kv_spec(1, True), kv_spec(1, False),
                  pl.BlockSpec((TM, d), row),
                  pl.BlockSpec(w.shape, fixed), pl.BlockSpec((1, d), fixed),
                  pl.BlockSpec(wr.shape, fixed), pl.BlockSpec(tri.shape, fixed)],
        out_specs=[pl.BlockSpec((TM, d), row), pl.BlockSpec((TM * SUBLANES, LANES), row),
                   pl.BlockSpec((SUBLANES, TM), lambda i: (0, i)), pl.BlockSpec((TM, 2 * LANES), row),
                   pl.BlockSpec((SUBLANES, LANES), fixed)],
        out_shape=[jax.ShapeDtypeStruct((t, d), F32),
                   jax.ShapeDtypeStruct((t * SUBLANES, LANES), F32),
                   jax.ShapeDtypeStruct((SUBLANES, t), F32),
                   jax.ShapeDtypeStruct((t, 2 * LANES), F32),
                   jax.ShapeDtypeStruct((SUBLANES, LANES), F32)],
        scratch_shapes=[pltpu.VMEM((TM, d), BF16), pltpu.VMEM((SUBLANES, LANES), F32)],
        compiler_params=_params("arbitrary"),
        name="swa_wo_router",
    )(fill, q, kv, kv, kv, kv, x, w, g, wr, tri)


def _swiglu(h, w1_ref, w3_ref, w2_ref):
    width = w1_ref.shape[1] // SWIGLU_SPLITS
    y = None
    for c in range(SWIGLU_SPLITS):
        sl = slice(c * width, (c + 1) * width)
        a = jnp.dot(h, w1_ref[:, sl], preferred_element_type=F32)
        b = jnp.dot(h, w3_ref[:, sl], preferred_element_type=F32)
        act = (a * jax.nn.sigmoid(a) * b).astype(BF16)
        part = jnp.dot(act, w2_ref[sl, :], preferred_element_type=F32)
        y = part if y is None else y + part
    return y


def _moe_kernel(te_ref, nu_ref, xs_ref, g_ref, w1_ref, w3_ref, w2_ref, o_ref):
    used = pl.program_id(0) < nu_ref[0]

    @pl.when(used)
    def _():
        x = _from_token_tiles(xs_ref, xs_ref.shape[0] // SUBLANES)
        _to_token_tiles(_swiglu(_rms(x, g_ref[...]).astype(BF16), w1_ref, w3_ref, w2_ref), o_ref)

    @pl.when(jnp.logical_not(used))
    def _():
        o_ref[...] = jnp.zeros_like(o_ref)


def _moe_experts(xs, g, w1, w3, w2, tile_expert, n_used, n_tiles):
    d, f = w1.shape[1], w1.shape[2]
    lines = TM * SUBLANES
    resident = pl.Buffered(1)
    grid_spec = pltpu.PrefetchScalarGridSpec(
        num_scalar_prefetch=2,
        grid=(n_tiles,),
        in_specs=[pl.BlockSpec((lines, LANES), lambda i, te, nu: (i, 0)),
                  pl.BlockSpec((1, d), lambda i, te, nu: (0, 0)),
                  pl.BlockSpec((None, d, f), lambda i, te, nu: (te[i], 0, 0), pipeline_mode=resident),
                  pl.BlockSpec((None, d, f), lambda i, te, nu: (te[i], 0, 0), pipeline_mode=resident),
                  pl.BlockSpec((None, f, d), lambda i, te, nu: (te[i], 0, 0))],
        out_specs=pl.BlockSpec((lines, LANES), lambda i, te, nu: (i, 0)),
    )
    return pl.pallas_call(
        _moe_kernel,
        grid_spec=grid_spec,
        out_shape=jax.ShapeDtypeStruct((n_tiles * lines, LANES), F32),
        compiler_params=_params("arbitrary"),
        name="moe_experts",
    )(tile_expert, n_used, xs, g, w1, w3, w2)


def _dispatch_kernel(dst0_ref, dst1_ref, pad_ref, src_ref, zero_ref, out_hbm, sem):
    rows = dst0_ref.shape[2]
    n_pad = pad_ref.shape[0]

    def issue(r, carry):
        base = r * DMA_UNROLL
        idx = [[ref[0, 0, base + q] for q in range(DMA_UNROLL)] for ref in (dst0_ref, dst1_ref)]
        for u in range(DMA_UNROLL):
            src = _tile_at(src_ref, (base + u) * SUBLANES)
            for k in range(2):
                pltpu.make_async_copy(src, _tile_at(out_hbm, idx[k][u]), sem).start(priority=k)
        return carry

    lax.fori_loop(0, rows // DMA_UNROLL, issue, 0)
    lines = 2 * rows * SUBLANES
    pltpu.make_async_copy(out_hbm.at[pl.ds(0, lines)], out_hbm.at[pl.ds(0, lines)], sem).wait()

    @pl.when(pl.program_id(0) == 0)
    def _():
        def fill(r, carry):
            base = r * DMA_UNROLL
            idx = [pad_ref[base + q] for q in range(DMA_UNROLL)]
            for q in range(DMA_UNROLL):
                pltpu.make_async_copy(zero_ref, _tile_at(out_hbm, idx[q]), sem).start()
            return carry

        lax.fori_loop(0, n_pad // DMA_UNROLL, fill, 0)
        pad_lines = n_pad * SUBLANES
        pltpu.make_async_copy(out_hbm.at[pl.ds(0, pad_lines)], out_hbm.at[pl.ds(0, pad_lines)],
                              sem).wait()


def _dispatch(x_tiles, dest_line, pad_line, n_slots):
    t = x_tiles.shape[0] // SUBLANES
    steps = t // DISPATCH_ROWS
    zero = jnp.zeros((SUBLANES, LANES), x_tiles.dtype)
    idx = dest_line.reshape(2 * steps, 1, DISPATCH_ROWS)
    idx_spec = lambda pick: pl.BlockSpec((1, 1, DISPATCH_ROWS), lambda i: (pick * steps + i, 0, 0),
                                         memory_space=pltpu.SMEM)
    return pl.pallas_call(
        _dispatch_kernel,
        grid=(steps,),
        in_specs=[idx_spec(0), idx_spec(1),
                  pl.BlockSpec(memory_space=pltpu.SMEM),
                  pl.BlockSpec((DISPATCH_ROWS * SUBLANES, LANES), lambda i: (i, 0)),
                  pl.BlockSpec((SUBLANES, LANES), lambda i: (0, 0))],
        out_specs=pl.BlockSpec(memory_space=pl.ANY),
        out_shape=jax.ShapeDtypeStruct((n_slots * SUBLANES, LANES), x_tiles.dtype),
        scratch_shapes=[pltpu.SemaphoreType.DMA(())],
        compiler_params=_params("arbitrary"),
        name="moe_dispatch",
    )(idx, idx, pad_line, x_tiles, zero)


def _ple_rows(x, p, g_ref, wg_ref, wp_ref, between=None):
    h = _rms(x, g_ref[...]).astype(BF16)
    gate = []
    for c in range(wg_ref.shape[1] // MXU_COLS):
        if between is not None:
            between(c)
        logits = jnp.dot(h, wg_ref[:, c * MXU_COLS:(c + 1) * MXU_COLS], preferred_element_type=F32)
        gate.append(jax.nn.sigmoid(logits))
    proj = jnp.dot(p.astype(BF16), wp_ref[...], preferred_element_type=F32)
    return x + proj * jnp.concatenate(gate, axis=1)


def _combine_ple_kernel(cur0_ref, cur1_ref, nxt0_ref, nxt1_ref, x_ref, gates_ref, p_ref, g_ref, wg_ref,
                        wp_ref, fg_ref, ys_hbm, o_ref, a_ref, b_ref, sem):
    i = pl.program_id(0)
    last = pl.num_programs(0) - 1
    tm = a_ref.shape[1] // SUBLANES
    groups = tm // DMA_UNROLL
    cur_ref, nxt_ref = (cur0_ref, cur1_ref), (nxt0_ref, nxt1_ref)

    def issue_group(idx_ref, offset, slot, base):
        idx = [[ref[0, 0, offset + base + q] for q in range(DMA_UNROLL)] for ref in idx_ref]
        for u in range(DMA_UNROLL):
            line = (base + u) * SUBLANES
            pltpu.make_async_copy(_tile_at(ys_hbm, idx[0][u]), _tile_at(a_ref.at[slot], line),
                                  sem.at[slot]).start(priority=0)
            pltpu.make_async_copy(_tile_at(ys_hbm, idx[1][u]), _tile_at(b_ref.at[slot], line),
                                  sem.at[slot]).start(priority=1)

    def wait(slot):
        whole = ys_hbm.at[pl.ds(0, tm * SUBLANES)]
        pltpu.make_async_copy(whole, a_ref.at[slot], sem.at[slot]).wait()
        pltpu.make_async_copy(whole, b_ref.at[slot], sem.at[slot]).wait()

    def compute(slot, idx_ref, offset):
        rows = slice(slot * tm, (slot + 1) * tm)
        g1, g2 = gates_ref[rows, :LANES], gates_ref[rows, LANES:]
        y = jnp.concatenate(
            [g1 * a_ref[slot, pl.ds(c, tm, stride=SUBLANES), :]
             + g2 * b_ref[slot, pl.ds(c, tm, stride=SUBLANES), :] for c in range(SUBLANES)], axis=1)
        chunks = wg_ref.shape[1] // MXU_COLS

        def between(c):
            for grp in range(c * groups // chunks, (c + 1) * groups // chunks):
                issue_group(idx_ref, offset, 1 - slot, grp * DMA_UNROLL)

        x = _ple_rows(x_ref[rows, :] + y, p_ref[rows, :], g_ref, wg_ref, wp_ref, between)
        o_ref[rows, :] = _rms(x, fg_ref[...])

    @pl.when(i == 0)
    def _():
        lax.fori_loop(0, groups, lambda r, c: (issue_group(cur_ref, 0, 0, r * DMA_UNROLL), c)[1], 0)

    wait(0)
    compute(0, cur_ref, tm)
    wait(1)
    compute(1, nxt_ref, 0)

    @pl.when(i == last)
    def _():
        wait(0)


def _combine_ple(x, gates, ys, dest_line, p, layer, g, wg, wp, fg):
    t, d = x.shape
    tm = TM_COMBINE
    steps = t // (2 * tm)
    row = lambda i: (i, 0)
    fixed = lambda i: (0, 0)
    idx = dest_line.reshape(2 * steps, 1, 2 * tm)
    idx_spec = lambda index: pl.BlockSpec((1, 1, 2 * tm), index, memory_space=pltpu.SMEM)
    nxt = lambda i: jnp.minimum(i + 1, steps - 1)
    return pl.pallas_call(
        _combine_ple_kernel,
        grid=(steps,),
        in_specs=[idx_spec(lambda i: (i, 0, 0)), idx_spec(lambda i: (steps + i, 0, 0)),
                  idx_spec(lambda i: (nxt(i), 0, 0)), idx_spec(lambda i: (steps + nxt(i), 0, 0)),
                  pl.BlockSpec((2 * tm, d), row),
                  pl.BlockSpec((2 * tm, 2 * LANES), row),
                  pl.BlockSpec((None, 2 * tm, p.shape[2]), lambda i: (layer, i, 0)),
                  pl.BlockSpec((1, d), fixed),
                  pl.BlockSpec(wg.shape, fixed),
                  pl.BlockSpec(wp.shape, fixed),
                  pl.BlockSpec((1, d), fixed),
                  pl.BlockSpec(memory_space=pl.ANY)],
        out_specs=pl.BlockSpec((2 * tm, d), row),
        out_shape=jax.ShapeDtypeStruct((t, d), F32),
        scratch_shapes=[pltpu.VMEM((2, tm * SUBLANES, LANES), F32),
                        pltpu.VMEM((2, tm * SUBLANES, LANES), F32),
                        pltpu.SemaphoreType.DMA((2,))],
        compiler_params=_params("arbitrary"),
        name="combine_ple_final",
    )(idx, idx, idx, idx, x, gates, p, g, wg, wp, fg, ys)


def _routing_tables(route, counts_f):
    t = route.shape[1]
    picked = route[0:2].astype(jnp.int32)
    rank = route[2:4].astype(jnp.int32)
    counts = counts_f[0, :N_EXPERTS].astype(jnp.int32)
    tiles_per = (counts + TM - 1) // TM
    tile_end = jnp.cumsum(tiles_per)
    tile_start = tile_end - tiles_per
    first_slot = jnp.zeros_like(picked)
    for e in range(N_EXPERTS):
        first_slot = jnp.where(picked == e, tile_start[e] * TM, first_slot)
    dest = first_slot + rank
    n_tiles = (2 * t) // TM + N_EXPERTS
    n_used = tile_end[-1]
    tile_ids = jnp.minimum(jnp.arange(n_tiles), n_used - 1)
    tile_expert = jnp.minimum(jnp.sum((tile_ids[:, None] >= tile_end[None, :]).astype(jnp.int32), axis=1),
                              N_EXPERTS - 1)
    pad_sizes = jnp.concatenate([tiles_per * TM - counts, ((n_tiles - n_used) * TM).reshape(1)])
    pad_end = jnp.cumsum(pad_sizes)
    seg_first = jnp.concatenate([tile_start * TM + counts, (n_used * TM).reshape(1)])
    k = jnp.arange(N_EXPERTS * TM)
    seg = jnp.sum((k[:, None] >= pad_end[None, :]).astype(jnp.int32), axis=1)
    pad = seg_first[seg] + k - (pad_end - pad_sizes)[seg]
    to_line = lambda v: (v * SUBLANES).astype(jnp.int32)
    return (to_line(dest), to_line(pad), tile_expert.astype(jnp.int32),
            n_used.reshape(1).astype(jnp.int32), n_tiles)


def kernel(x, p, attn_norm, ffn_norm, a_w_qkv, a_w_o, kv_norm, kv_w, b_w_q, b_sinks, b_w_o,
           dense_w1, dense_w3, dense_w2, moe_router, moe_w1, moe_w3, moe_w2,
           ple_norm, ple_w_gate, ple_w_proj, final_norm):
    batch, seq, d = x.shape
    t = batch * seq
    n_heads = d // HEAD_DIM
    n_groups = len(A_CONFIGS)
    assert d == SUBLANES * LANES and seq % TM == 0 and t % DISPATCH_ROWS == 0
    xr = x.reshape(t, d)
    pr = p.reshape(p.shape[0], t, p.shape[3])
    cos, sin = _rope_tables(np.arange(seq))
    row = lambda v: v.reshape(1, -1)

    w = a_w_qkv[0].astype(BF16).reshape(d, n_groups, 3, d)
    coef_a, coef_b = _rope_coeffs(cos, sin)
    cast_jobs = ((moe_w1[0], dense_w1[0], a_w_o[0], b_w_q[0]),
                 (moe_w3[0], dense_w3[0], ple_w_gate),
                 (moe_w2[0], b_w_o[0]))
    def proj_part(g):
        w_g = jnp.concatenate([_pair_columns(w[:, g, 0], True), _pair_columns(w[:, g, 1], True),
                               w[:, g, 2]], axis=1)
        return _group_proj_part(xr, row(attn_norm[0]), w_g, coef_a, coef_b, batch, seq, A_CONFIGS[g][1],
                                cast_jobs[g])

    outs, stats, cast = [], [], []
    (qkv, *done), = _run_parts([proj_part(0)], t // TM, "group_proj_d1")
    cast.append(done)
    for g, (_, dil) in enumerate(A_CONFIGS):
        attn, steps = _dilated_attn_part(qkv, batch, seq, g)
        if g + 1 < n_groups:
            assert steps == t // TM
            (o_g, st_g), (qkv, *done) = _run_parts([attn, proj_part(g + 1)], steps, f"attn_g{g}_proj_g{g + 1}")
            cast.append(done)
        else:
            (o_g, st_g), = _run_parts([attn], steps, f"dilated_attn_g{g}")
        outs.append(o_g.reshape(batch, dil, seq // dil, d))
        stats.append(st_g.reshape(batch, dil, seq // dil, LANES))
    (moe1, dense1, wo0, wq1), (moe3, dense3, ple_gate), (moe2, wo1) = [
        [c.reshape(s.shape) for c, s in zip(done, srcs)] for done, srcs in zip(cast, cast_jobs)]
    head_of_col = jnp.arange(d) // HEAD_DIM
    expand = (jnp.arange(LANES)[:, None] == head_of_col[None, :]).astype(BF16)
    xr = _merge_ffn(xr, outs, stats, expand, wo0, row(ffn_norm[0]), dense1, dense3,
                    dense_w2[0].astype(BF16), batch, seq)

    half = n_heads // 2
    kv_cols = B_KV_HEADS * HEAD_DIM
    kv_bf = kv_w.astype(BF16)
    w_kv = jnp.concatenate([_pair_columns(kv_bf[:, :kv_cols], True), kv_bf[:, kv_cols:]], axis=1)
    xr, q1, kv1 = _ple_qkv1(xr, pr, 0, row(ple_norm[0]), ple_gate[0], ple_w_proj[0].astype(BF16),
                            row(attn_norm[1]), row(kv_norm), _pair_columns(wq1, False), w_kv, cos, sin)
    w_o1 = wo1.reshape(2, half, HEAD_DIM, d).transpose(1, 0, 2, 3).reshape(d, d)
    wr = jnp.pad(moe_router[0], ((0, 0), (0, LANES - N_EXPERTS))).astype(BF16)
    xr, x_tiles, route, gates, counts = _swa_wo_router(q1, kv1, b_sinks[0], xr, w_o1, row(ffn_norm[1]),
                                                       wr, seq)

    dest_line, pad_line, tile_expert, n_used, n_tiles = _routing_tables(route, counts)
    xs = _dispatch(x_tiles, dest_line, pad_line, n_tiles * TM)
    ys = _moe_experts(xs, row(ffn_norm[1]), moe1, moe3, moe2, tile_expert, n_used, n_tiles)
    xr = _combine_ple(xr, gates, ys, dest_line, pr, 1, row(ple_norm[1]), ple_gate[1],
                      ple_w_proj[1].astype(BF16), row(final_norm))
    return xr.reshape(batch, seq, d)
```

```python
import functools

import numpy as np
import jax
import jax.numpy as jnp
from jax import lax
from jax.experimental import pallas as pl
from jax.experimental.pallas import tpu as pltpu

F32 = jnp.float32
BF16 = jnp.bfloat16

LANES = 128
SUBLANES = 8
MXU_COLS = 256
HEAD_DIM = 64
HALF_DIM = HEAD_DIM // 2
BLOCK = 128
Q_BLOCKS = 4
A_CONFIGS = ((128, 1), (512, 4), (2048, 16))
B_KV_HEADS = 2
B_WINDOW = 128
N_EXPERTS = 8
ROPE_THETA = 10000.0
EPS = 1e-6
NEG_INF = -1e30
VMEM_LIMIT = 56 * 1024 * 1024

TM = 512
TM_COMBINE = 512
DISPATCH_ROWS = 2048
DMA_UNROLL = 8
SWIGLU_SPLITS = 2

PLAIN, ROPE, ROPE_SCALED = 0, 1, 2
LOG2_E = 1.4426950408889634
LN_2 = 0.6931471805599453
Q_SCALE = HEAD_DIM ** -0.5 * LOG2_E


def _params(*sem):
    return pltpu.CompilerParams(dimension_semantics=sem, vmem_limit_bytes=VMEM_LIMIT)


def _rms(x, g):
    ms = jnp.mean(x * x, axis=-1, keepdims=True)
    return x * lax.rsqrt(ms + EPS) * g


def _pair_columns(w, adjacent):
    d, heads = w.shape[0], w.shape[1] // HEAD_DIM
    if adjacent:
        return w.reshape(d, heads // 2, 2, 2, HALF_DIM).transpose(0, 1, 3, 2, 4).reshape(w.shape)
    return w.reshape(d, 2, heads // 2, 2, HALF_DIM).transpose(0, 2, 3, 1, 4).reshape(w.shape)


def _rope_tables(positions):
    pos = jnp.asarray(positions, F32)
    inv = 1.0 / (ROPE_THETA ** (jnp.arange(HALF_DIM, dtype=F32) / HALF_DIM))
    ang = pos[:, None] * inv[None, :]
    cos, sin = jnp.cos(ang), jnp.sin(ang)
    return jnp.tile(cos, (1, 4)), jnp.concatenate([-sin, -sin, sin, sin], axis=1)


def _rope_coeffs(cos, sin):
    a = jnp.stack([cos * Q_SCALE, cos, jnp.ones_like(cos)])
    b = jnp.stack([sin * Q_SCALE, sin, jnp.zeros_like(sin)])
    return a, b


def _store_chunks(acc, cos, sin, o_ref, kinds):
    for c, kind in enumerate(kinds):
        seg = acc[:, c * LANES:(c + 1) * LANES]
        if kind != PLAIN:
            seg = seg * cos + pltpu.roll(seg, LANES // 2, 1) * sin
            if kind == ROPE_SCALED:
                seg = seg * Q_SCALE
        o_ref[:, c * LANES:(c + 1) * LANES] = seg.astype(o_ref.dtype)


def _to_token_tiles(val, ref):
    rows = val.shape[0]
    for c in range(val.shape[1] // LANES):
        ref[pl.ds(c, rows, stride=SUBLANES), :] = val[:, c * LANES:(c + 1) * LANES]


def _from_token_tiles(ref, rows):
    return jnp.concatenate([ref[pl.ds(c, rows, stride=SUBLANES), :] for c in range(SUBLANES)], axis=1)


def _tile_at(ref, line):
    return ref.at[pl.ds(pl.multiple_of(line, SUBLANES), SUBLANES)]


class _Part:
    def __init__(self, body, inputs, in_specs, out_specs, out_shapes):
        self.body, self.inputs, self.in_specs = body, list(inputs), list(in_specs)
        self.out_specs, self.out_shapes = list(out_specs), list(out_shapes)


def _run_parts(parts, steps, name):
    n_in = [len(p.in_specs) for p in parts]
    n_out = [len(p.out_specs) for p in parts]

    def kernel(*refs):
        ins, outs = refs[:sum(n_in)], refs[sum(n_in):]
        for k, part in enumerate(parts):
            part.body(ins[sum(n_in[:k]):sum(n_in[:k + 1])], outs[sum(n_out[:k]):sum(n_out[:k + 1])])

    res = pl.pallas_call(
        kernel,
        grid=(steps,),
        in_specs=[s for p in parts for s in p.in_specs],
        out_specs=[s for p in parts for s in p.out_specs],
        out_shape=[s for p in parts for s in p.out_shapes],
        compiler_params=_params("parallel"),
        name=name,
    )(*[a for p in parts for a in p.inputs])
    return [res[sum(n_out[:k]):sum(n_out[:k + 1])] for k in range(len(parts))]


def _group_proj_body(ins, outs, *, dil):
    x_ref, g_ref, w_ref, a_ref, b_ref, perm_ref, *cast_refs = ins
    o_ref, *cast_out_refs = outs
    for src_ref, dst_ref in zip(cast_refs, cast_out_refs):
        dst_ref[...] = src_ref[...].astype(dst_ref.dtype)
    h = _rms(x_ref[...], g_ref[...]).astype(BF16)
    if dil > 1:
        h = jnp.dot(perm_ref[...], h, preferred_element_type=F32).astype(BF16)
    sub = h.shape[0] // dil
    tn = w_ref.shape[1] // 3
    for kind in range(3):
        acc = jnp.dot(h, w_ref[:, kind * tn:(kind + 1) * tn], preferred_element_type=F32)
        a, b = a_ref[kind], b_ref[kind]
        for c in range(tn // LANES):
            seg = acc[:, c * LANES:(c + 1) * LANES]
            seg = (seg * a + pltpu.roll(seg, LANES // 2, 1) * b).astype(o_ref.dtype)
            cols = slice(kind * tn + c * LANES, kind * tn + (c + 1) * LANES)
            if dil == 1:
                o_ref[:, cols] = seg
            else:
                for r in range(dil):
                    o_ref[r, :, cols] = seg[r * sub:(r + 1) * sub]


def _residue_major_rows(dil):
    q = np.arange(TM)
    sub = TM // dil
    return (q % sub) * dil + q // sub


def _group_proj_part(x, g, w, coef_a, coef_b, batch, seq, dil, cast_srcs):
    t, d = x.shape
    n_out = w.shape[1]
    tiles_per_seq = seq // TM
    steps = t // TM
    cast2d = [c.reshape(-1, c.shape[-1]) for c in cast_srcs]
    for c in cast2d:
        assert c.shape[0] % (steps * 2 * SUBLANES) == 0, c.shape
    cast_specs = [pl.BlockSpec((c.shape[0] // steps, c.shape[1]), lambda i: (i, 0)) for c in cast2d]
    if dil == 1:
        out_spec = pl.BlockSpec((TM, n_out), lambda i: (i, 0))
        out_shape = jax.ShapeDtypeStruct((t, n_out), BF16)
    else:
        out_spec = pl.BlockSpec((None, dil, TM // dil, n_out),
                                lambda i: (i // tiles_per_seq, 0, i % tiles_per_seq, 0))
        out_shape = jax.ShapeDtypeStruct((batch, dil, seq // dil, n_out), BF16)
    nat = _residue_major_rows(dil)
    perm = jnp.asarray(nat[:, None] == np.arange(TM)[None, :], BF16)
    if dil > 1:
        tile_order = lambda c: c.reshape(3, seq // TM, TM // dil, dil, LANES).transpose(0, 1, 3, 2, 4) \
            .reshape(3, seq, LANES)
        coef_a, coef_b = tile_order(coef_a), tile_order(coef_b)
    coef_spec = pl.BlockSpec((3, TM, LANES), lambda i: (0, i % tiles_per_seq, 0))
    return _Part(
        functools.partial(_group_proj_body, dil=dil),
        [x, g, w, coef_a, coef_b, perm, *cast2d],
        [pl.BlockSpec((TM, d), lambda i: (i, 0)),
         pl.BlockSpec((1, d), lambda i: (0, 0)),
         pl.BlockSpec(w.shape, lambda i: (0, 0), pipeline_mode=pl.Buffered(1)),
         coef_spec, coef_spec,
         pl.BlockSpec((TM, TM), lambda i: (0, 0), pipeline_mode=pl.Buffered(1))] + cast_specs,
        [out_spec] + cast_specs,
        [out_shape] + [jax.ShapeDtypeStruct(c.shape, BF16) for c in cast2d])


def _ple_qkv1_kernel(x_ref, p_ref, g_ref, wg_ref, wp_ref, gq_ref, gkv_ref, wq_ref, wkv_ref, cos_ref, sin_ref,
                     x_out_ref, q_ref, kv_ref):
    x = _ple_rows(x_ref[...], p_ref[...], g_ref, wg_ref, wp_ref)
    x_out_ref[...] = x
    xn = x * lax.rsqrt(jnp.mean(x * x, axis=-1, keepdims=True) + EPS)
    hq = (xn * gq_ref[...]).astype(BF16)
    hkv = (xn * gkv_ref[...]).astype(BF16)
    cos, sin = cos_ref[...], sin_ref[...]
    accq = jnp.dot(hq, wq_ref[...], preferred_element_type=F32)
    _store_chunks(accq, cos, sin, q_ref, (ROPE_SCALED,) * (accq.shape[1] // LANES))
    acckv = jnp.dot(hkv, wkv_ref[...], preferred_element_type=F32)
    _store_chunks(acckv, cos, sin, kv_ref, (ROPE, PLAIN))


def _ple_qkv1(x, p, layer, g, wg, wp, gq, gkv, wq, wkv, cos, sin):
    t, d = x.shape
    s_tiles = cos.shape[0] // TM
    row = lambda i: (i, 0)
    fixed = lambda i: (0, 0)
    return pl.pallas_call(
        _ple_qkv1_kernel,
        grid=(t // TM,),
        in_specs=[
            pl.BlockSpec((TM, d), row),
            pl.BlockSpec((None, TM, p.shape[2]), lambda i: (layer, i, 0)),
            pl.BlockSpec((1, d), fixed),
            pl.BlockSpec(wg.shape, fixed),
            pl.BlockSpec(wp.shape, fixed),
            pl.BlockSpec((1, d), fixed),
            pl.BlockSpec((1, d), fixed),
            pl.BlockSpec(wq.shape, fixed),
            pl.BlockSpec(wkv.shape, fixed),
            pl.BlockSpec((TM, LANES), lambda i: (i % s_tiles, 0)),
            pl.BlockSpec((TM, LANES), lambda i: (i % s_tiles, 0)),
        ],
        out_specs=[pl.BlockSpec((TM, d), row), pl.BlockSpec((TM, wq.shape[1]), row),
                   pl.BlockSpec((TM, wkv.shape[1]), row)],
        out_shape=[jax.ShapeDtypeStruct((t, d), F32),
                   jax.ShapeDtypeStruct((t, wq.shape[1]), BF16),
                   jax.ShapeDtypeStruct((t, wkv.shape[1]), BF16)],
        compiler_params=_params("parallel"),
        name="ple_qkv_layer1",
    )(x, p, g, wg, wp, gq, gkv, wq, wkv, cos, sin)


def _band_mask(has_prev, max_dist):
    qi = lax.broadcasted_iota(jnp.int32, (BLOCK, 2 * BLOCK), 0) + BLOCK
    kj = lax.broadcasted_iota(jnp.int32, (BLOCK, 2 * BLOCK), 1)
    rel = qi - kj
    mask = (rel >= 0) & (rel <= max_dist)
    if has_prev is not True:
        mask = mask & ((kj >= BLOCK) | has_prev)
    return jnp.concatenate([mask, mask], axis=0)


def _band_blocks(qb, prev_ref, cur_ref, lanes):
    own = cur_ref[qb * BLOCK:(qb + 1) * BLOCK, lanes]
    before = prev_ref[:, lanes] if qb == 0 else cur_ref[(qb - 1) * BLOCK:qb * BLOCK, lanes]
    return jnp.concatenate([before, own], axis=0)


def _lane_masks():
    lane = lax.broadcasted_iota(jnp.int32, (BLOCK, LANES), 1)
    first_qk = (lane % HEAD_DIM) < HALF_DIM
    sel_a = jnp.where(first_qk, 1.0, 0.0).astype(BF16)
    sel_b = jnp.where(first_qk, 0.0, 1.0).astype(BF16)
    return lane, sel_a, sel_b, lane < HEAD_DIM


def _pair_scores(q2, kcat, sel_a, sel_b):
    qs = jnp.concatenate([q2 * sel_a, q2 * sel_b], axis=0)
    return lax.dot_general(qs, kcat, (((1,), (1,)), ((), ())), preferred_element_type=F32)


def _dil_attn_body(ins, outs, *, steps, steps_per_seq):
    q_ref, kp_ref, kc_ref, vp_ref, vc_ref = ins
    o_ref, st_ref = outs
    lane, sel_a, sel_b, first_v = _lane_masks()
    for qb in range(q_ref.shape[0] // BLOCK):
        rows = slice(qb * BLOCK, (qb + 1) * BLOCK)
        mask2 = _band_mask(True if qb else pl.program_id(0) % steps_per_seq > 0, steps)
        stats = jnp.zeros((BLOCK, LANES), F32)
        for p in range(q_ref.shape[1] // LANES):
            sl = slice(p * LANES, (p + 1) * LANES)
            kcat = _band_blocks(qb, kp_ref, kc_ref, sl)
            vcat = _band_blocks(qb, vp_ref, vc_ref, sl)
            s = jnp.where(mask2, _pair_scores(q_ref[rows, sl], kcat, sel_a, sel_b), NEG_INF)
            m = jnp.max(s, axis=-1, keepdims=True)
            pe = jnp.exp2(s - m)
            den = jnp.sum(pe, axis=-1, keepdims=True)
            pv = jnp.dot(pe.astype(BF16), vcat, preferred_element_type=F32)
            pv = pv * (1.0 / den)
            o_ref[rows, sl] = jnp.where(first_v, pv[:BLOCK], pv[BLOCK:]).astype(o_ref.dtype)
            lse = m * LN_2 + jnp.log(den)
            stats = jnp.where(lane == 2 * p, lse[:BLOCK], stats)
            stats = jnp.where(lane == 2 * p + 1, lse[BLOCK:], stats)
        st_ref[rows, :] = stats


def _dilated_attn_part(qkv, batch, seq, g):
    win, dil = A_CONFIGS[g]
    n = seq // dil
    d = qkv.shape[-1] // 3
    view = qkv.reshape(batch * dil, n, 3 * d)

    q_blocks = min(Q_BLOCKS, n // BLOCK)
    rows = q_blocks * BLOCK
    per_seq = n // rows

    def spec(kind, prev):
        if prev:
            return pl.BlockSpec((None, BLOCK, d), lambda i: (
                i // per_seq, jnp.maximum((i % per_seq) * q_blocks - 1, 0), kind))
        return pl.BlockSpec((None, rows, d), lambda i: (i // per_seq, i % per_seq, kind))

    out_index = lambda i: (i // per_seq, i % per_seq, 0)
    part = _Part(
        functools.partial(_dil_attn_body, steps=win // dil, steps_per_seq=per_seq),
        [view] * 5,
        [spec(0, False), spec(1, True), spec(1, False), spec(2, True), spec(2, False)],
        [pl.BlockSpec((None, rows, d), out_index), pl.BlockSpec((None, rows, LANES), out_index)],
        [jax.ShapeDtypeStruct((batch * dil, n, d), BF16),
         jax.ShapeDtypeStruct((batch * dil, n, LANES), F32)])
    return part, batch * dil * per_seq


def _swa_blocks(fill_ref, q_ref, kp_ref, kc_ref, vp_ref, vc_ref, o_ref, has_prev, after_block):
    _, sel_a, sel_b, first_v = _lane_masks()
    all_lanes = slice(0, LANES)
    key_row = lax.broadcasted_iota(jnp.int32, (2 * BLOCK, LANES), 0)
    for qb in range(q_ref.shape[0] // BLOCK):
        rows = slice(qb * BLOCK, (qb + 1) * BLOCK)
        mask2 = _band_mask(True if qb else has_prev, B_WINDOW - 1)
        kcat = _band_blocks(qb, kp_ref, kc_ref, all_lanes)
        vcat = _band_blocks(qb, vp_ref, vc_ref, all_lanes)
        vcat = jnp.where(key_row == 0, jnp.zeros_like(vcat), vcat)
        for j in range(q_ref.shape[1] // LANES):
            sl = slice(j * LANES, (j + 1) * LANES)
            s = jnp.where(mask2, _pair_scores(q_ref[rows, sl], kcat, sel_a, sel_b), fill_ref[j])
            m = jnp.max(s, axis=-1, keepdims=True)
            pe = jnp.exp2(s - m)
            den = jnp.sum(pe, axis=-1, keepdims=True)
            pv = jnp.dot(pe.astype(BF16), vcat, preferred_element_type=F32)
            pv = pv * (1.0 / den)
            o_ref[rows, sl] = jnp.where(first_v, pv[:BLOCK], pv[BLOCK:]).astype(o_ref.dtype)
        after_block(rows)


def _sink_fill(sinks):
    half = sinks.shape[0] // 2
    sink_rows = jnp.repeat(jnp.stack([sinks[:half], sinks[half:]], axis=1), BLOCK, axis=1)
    first_key = jnp.arange(2 * BLOCK)[None, None, :] == 0
    return jnp.where(first_key, sink_rows[:, :, None].astype(F32) * LOG2_E, NEG_INF)


def _interleave(src_ref, dst_ref):
    dil, sub = src_ref.shape[0], src_ref.shape[1]
    chunks = dst_ref.shape[0]
    for r in range(dil):
        for c in range(chunks):
            part = src_ref[r, :, c * LANES:(c + 1) * LANES].astype(F32)
            dst_ref[c, pl.ds(r, sub, stride=dil), :] = part
    return jnp.concatenate([dst_ref[c] for c in range(chunks)], axis=1)


def _merge_ffn_kernel(x_ref, o0_ref, o1_ref, o2_ref, s0_ref, s1_ref, s2_ref, e_ref, w_ref, g_ref, w1_ref,
                      w3_ref, w2_ref, out_ref, ob_ref, sb1_ref, sb2_ref):
    s0, s1, s2 = s0_ref[0], _interleave(s1_ref, sb1_ref), _interleave(s2_ref, sb2_ref)
    top = jnp.maximum(jnp.maximum(s0, s1), s2)
    e0, e1, e2 = jnp.exp(s0 - top), jnp.exp(s1 - top), jnp.exp(s2 - top)
    inv = 1.0 / (e0 + e1 + e2)
    expand = e_ref[...]

    def weight(e):
        return jnp.dot((e * inv).astype(BF16), expand, preferred_element_type=F32)

    merged = weight(e0) * o0_ref[0].astype(F32)
    merged = merged + weight(e1) * _interleave(o1_ref, ob_ref)
    merged = merged + weight(e2) * _interleave(o2_ref, ob_ref)
    x = x_ref[...] + jnp.dot(merged.astype(BF16), w_ref[...], preferred_element_type=F32)
    out_ref[...] = x + _swiglu(_rms(x, g_ref[...]).astype(BF16), w1_ref, w3_ref, w2_ref)


def _merge_ffn(x, outs, stats, expand, w, g, w1, w3, w2, batch, seq):
    t, d = x.shape
    tiles_per_seq = seq // TM
    row = lambda i: (i, 0)
    fixed = lambda i: (0, 0)
    resident = lambda a: pl.BlockSpec(a.shape, fixed, pipeline_mode=pl.Buffered(1))

    def sub(arr):
        dil = arr.shape[1]
        return pl.BlockSpec((None, dil, TM // dil, arr.shape[3]),
                            lambda i: (i // tiles_per_seq, 0, i % tiles_per_seq, 0))

    return pl.pallas_call(
        _merge_ffn_kernel,
        grid=(t // TM,),
        in_specs=[pl.BlockSpec((TM, d), row)]
        + [sub(o) for o in outs] + [sub(s) for s in stats]
        + [resident(expand), resident(w), pl.BlockSpec((1, d), fixed),
           resident(w1), resident(w3), resident(w2)],
        out_specs=pl.BlockSpec((TM, d), row),
        out_shape=jax.ShapeDtypeStruct((t, d), F32),
        scratch_shapes=[pltpu.VMEM((d // LANES, TM, LANES), F32), pltpu.VMEM((1, TM, LANES), F32),
                        pltpu.VMEM((1, TM, LANES), F32)],
        compiler_params=_params("parallel"),
        name="merge_wo_swiglu",
    )(x, *outs, *stats, expand, w, g, w1, w3, w2)


def _swa_wo_router_kernel(fill_ref, q_ref, kp_ref, kc_ref, vp_ref, vc_ref, x_ref, w_ref, g_ref, wr_ref,
                          tri_ref, x1_ref, x1t_ref, route_ref, gates_ref, counts_ref, o_ref, run_ref,
                          *, steps_per_seq):
    @pl.when(pl.program_id(0) == 0)
    def _():
        run_ref[...] = jnp.zeros_like(run_ref)

    def project(rows):
        if rows.stop % MXU_COLS == 0:
            rows = slice(rows.stop - MXU_COLS, rows.stop)
            x1_ref[rows, :] = x_ref[rows, :] + jnp.dot(o_ref[rows, :], w_ref[...],
                                                       preferred_element_type=F32)

    _swa_blocks(fill_ref, q_ref, kp_ref, kc_ref, vp_ref, vc_ref, o_ref,
                pl.program_id(0) % steps_per_seq > 0, project)
    x1 = x1_ref[...]
    _to_token_tiles(x1, x1t_ref)
    h = _rms(x1, g_ref[...]).astype(BF16)
    logits = jnp.dot(h, wr_ref[...], preferred_element_type=F32)
    lane = lax.broadcasted_iota(jnp.int32, logits.shape, 1)
    logits = jnp.where(lane < N_EXPERTS, logits, -jnp.inf)
    v1 = jnp.max(logits, axis=-1, keepdims=True)
    i1 = jnp.min(jnp.where(logits == v1, lane, LANES), axis=-1, keepdims=True)
    rest = jnp.where(lane == i1, -jnp.inf, logits)
    v2 = jnp.max(rest, axis=-1, keepdims=True)
    i2 = jnp.min(jnp.where(rest == v2, lane, LANES), axis=-1, keepdims=True)
    e2 = jnp.exp(v2 - v1)
    inv = 1.0 / (1.0 + e2)
    gates_ref[:, :LANES] = jnp.broadcast_to(inv, logits.shape)
    gates_ref[:, LANES:] = jnp.broadcast_to(e2 * inv, logits.shape)
    picked = jnp.where(lane == i1, 1.0, 0.0) + jnp.where(lane == i2, 1.0, 0.0)
    before = run_ref[0:1, :] + jnp.dot(tri_ref[...], picked.astype(BF16), preferred_element_type=F32)
    rank1 = jnp.sum(jnp.where(lane == i1, before, 0.0), axis=-1, keepdims=True)
    rank2 = jnp.sum(jnp.where(lane == i2, before, 0.0), axis=-1, keepdims=True)
    total = run_ref[0:1, :] + jnp.sum(picked, axis=0, keepdims=True)
    run_ref[0:1, :] = total
    counts_ref[...] = jnp.broadcast_to(total, counts_ref.shape)
    info = jnp.where(lane == 0, i1.astype(F32), 0.0)
    info = jnp.where(lane == 1, i2.astype(F32), info)
    info = jnp.where(lane == 2, rank1, info)
    info = jnp.where(lane == 3, rank2, info)
    route_ref[...] = info.T[:SUBLANES, :]


def _swa_wo_router(q, kv, sinks, x, w, g, wr, seq):
    t, d = x.shape
    assert TM % BLOCK == 0 and seq % TM == 0
    per_seq = seq // TM
    blocks = TM // BLOCK
    row = lambda i: (i, 0)
    fixed = lambda i: (0, 0)
    tri = jnp.asarray(np.tril(np.ones((TM, TM), np.float32), -1), BF16)
    fill = _sink_fill(sinks)

    def kv_spec(col, prev):
        if prev:
            return pl.BlockSpec((BLOCK, LANES), lambda i: (jnp.maximum(i * blocks - 1, 0), col))
        return pl.BlockSpec((TM, LANES), lambda i: (i, col))

    return pl.pallas_call(
        functools.partial(_swa_wo_router_kernel, steps_per_seq=per_seq),
        grid=(t // TM,),
        in_specs=[pl.BlockSpec(fill.shape, lambda i: (0, 0, 0)),
                  pl.BlockSpec((TM, d), row),
                  kv_spec(0, True), kv_spec(0, False), kv_spec(1, True), kv_spec(1, False),
                  pl.BlockSpec((TM, d), row),
                  pl.BlockSpec(w.shape, fixed), pl.BlockSpec((1, d), fixed),
                  pl.BlockSpec(wr.shape, fixed), pl.BlockSpec(tri.shape, fixed)],
        out_specs=[pl.BlockSpec((TM, d), row), pl.BlockSpec((TM * SUBLANES, LANES), row),
                   pl.BlockSpec((SUBLANES, TM), lambda i: (0, i)), pl.BlockSpec((TM, 2 * LANES), row),
                   pl.BlockSpec((SUBLANES, LANES), fixed)],
        out_shape=[jax.ShapeDtypeStruct((t, d), F32),
                   jax.ShapeDtypeStruct((t * SUBLANES, LANES), F32),
                   jax.ShapeDtypeStruct((SUBLANES, t), F32),
                   jax.ShapeDtypeStruct((t, 2 * LANES), F32),
                   jax.ShapeDtypeStruct((SUBLANES, LANES), F32)],
        scratch_shapes=[pltpu.VMEM((TM, d), BF16), pltpu.VMEM((SUBLANES, LANES), F32)],
        compiler_params=_params("arbitrary"),
        name="swa_wo_router",
    )(fill, q, kv, kv, kv, kv, x, w, g, wr, tri)


def _swiglu(h, w1_ref, w3_ref, w2_ref):
    width = w1_ref.shape[1] // SWIGLU_SPLITS
    y = None
    for c in range(SWIGLU_SPLITS):
        sl = slice(c * width, (c + 1) * width)
        a = jnp.dot(h, w1_ref[:, sl], preferred_element_type=F32)
        b = jnp.dot(h, w3_ref[:, sl], preferred_element_type=F32)
        act = (a * jax.nn.sigmoid(a) * b).astype(BF16)
        part = jnp.dot(act, w2_ref[sl, :], preferred_element_type=F32)
        y = part if y is None else y + part
    return y


def _moe_kernel(te_ref, nu_ref, xs_ref, g_ref, w1_ref, w3_ref, w2_ref, o_ref):
    used = pl.program_id(0) < nu_ref[0]

    @pl.when(used)
    def _():
        x = _from_token_tiles(xs_ref, xs_ref.shape[0] // SUBLANES)
        _to_token_tiles(_swiglu(_rms(x, g_ref[...]).astype(BF16), w1_ref, w3_ref, w2_ref), o_ref)

    @pl.when(jnp.logical_not(used))
    def _():
        o_ref[...] = jnp.zeros_like(o_ref)


def _moe_experts(xs, g, w1, w3, w2, tile_expert, n_used, n_tiles):
    d, f = w1.shape[1], w1.shape[2]
    lines = TM * SUBLANES
    resident = pl.Buffered(1)
    grid_spec = pltpu.PrefetchScalarGridSpec(
        num_scalar_prefetch=2,
        grid=(n_tiles,),
        in_specs=[pl.BlockSpec((lines, LANES), lambda i, te, nu: (i, 0)),
                  pl.BlockSpec((1, d), lambda i, te, nu: (0, 0)),
                  pl.BlockSpec((None, d, f), lambda i, te, nu: (te[i], 0, 0), pipeline_mode=resident),
                  pl.BlockSpec((None, d, f), lambda i, te, nu: (te[i], 0, 0), pipeline_mode=resident),
                  pl.BlockSpec((None, f, d), lambda i, te, nu: (te[i], 0, 0))],
        out_specs=pl.BlockSpec((lines, LANES), lambda i, te, nu: (i, 0)),
    )
    return pl.pallas_call(
        _moe_kernel,
        grid_spec=grid_spec,
        out_shape=jax.ShapeDtypeStruct((n_tiles * lines, LANES), F32),
        compiler_params=_params("arbitrary"),
        name="moe_experts",
    )(tile_expert, n_used, xs, g, w1, w3, w2)


def _dispatch_kernel(dst0_ref, dst1_ref, pad_ref, src_ref, zero_ref, out_hbm, sem):
    rows = dst0_ref.shape[2]
    n_pad = pad_ref.shape[0]

    def issue(r, carry):
        base = r * DMA_UNROLL
        idx = [[ref[0, 0, base + q] for q in range(DMA_UNROLL)] for ref in (dst0_ref, dst1_ref)]
        for u in range(DMA_UNROLL):
            src = _tile_at(src_ref, (base + u) * SUBLANES)
            for k in range(2):
                pltpu.make_async_copy(src, _tile_at(out_hbm, idx[k][u]), sem).start(priority=k)
        return carry

    lax.fori_loop(0, rows // DMA_UNROLL, issue, 0)
    lines = 2 * rows * SUBLANES
    pltpu.make_async_copy(out_hbm.at[pl.ds(0, lines)], out_hbm.at[pl.ds(0, lines)], sem).wait()

    @pl.when(pl.program_id(0) == 0)
    def _():
        def fill(r, carry):
            base = r * DMA_UNROLL
            idx = [pad_ref[base + q] for q in range(DMA_UNROLL)]
            for q in range(DMA_UNROLL):
                pltpu.make_async_copy(zero_ref, _tile_at(out_hbm, idx[q]), sem).start()
            return carry

        lax.fori_loop(0, n_pad // DMA_UNROLL, fill, 0)
        pad_lines = n_pad * SUBLANES
        pltpu.make_async_copy(out_hbm.at[pl.ds(0, pad_lines)], out_hbm.at[pl.ds(0, pad_lines)],
                              sem).wait()


def _dispatch(x_tiles, dest_line, pad_line, n_slots):
    t = x_tiles.shape[0] // SUBLANES
    steps = t // DISPATCH_ROWS
    zero = jnp.zeros((SUBLANES, LANES), x_tiles.dtype)
    idx = dest_line.reshape(2 * steps, 1, DISPATCH_ROWS)
    idx_spec = lambda pick: pl.BlockSpec((1, 1, DISPATCH_ROWS), lambda i: (pick * steps + i, 0, 0),
                                         memory_space=pltpu.SMEM)
    return pl.pallas_call(
        _dispatch_kernel,
        grid=(steps,),
        in_specs=[idx_spec(0), idx_spec(1),
                  pl.BlockSpec(memory_space=pltpu.SMEM),
                  pl.BlockSpec((DISPATCH_ROWS * SUBLANES, LANES), lambda i: (i, 0)),
                  pl.BlockSpec((SUBLANES, LANES), lambda i: (0, 0))],
        out_specs=pl.BlockSpec(memory_space=pl.ANY),
        out_shape=jax.ShapeDtypeStruct((n_slots * SUBLANES, LANES), x_tiles.dtype),
        scratch_shapes=[pltpu.SemaphoreType.DMA(())],
        compiler_params=_params("arbitrary"),
        name="moe_dispatch",
    )(idx, idx, pad_line, x_tiles, zero)


def _ple_rows(x, p, g_ref, wg_ref, wp_ref, between=None):
    h = _rms(x, g_ref[...]).astype(BF16)
    proj = jnp.dot(p.astype(BF16), wp_ref[...], preferred_element_type=F32)
    if between is None:
        return x + proj * jax.nn.sigmoid(jnp.dot(h, wg_ref[...], preferred_element_type=F32))
    gate = []
    for c in range(wg_ref.shape[1] // MXU_COLS):
        between(c)
        logits = jnp.dot(h, wg_ref[:, c * MXU_COLS:(c + 1) * MXU_COLS], preferred_element_type=F32)
        gate.append(jax.nn.sigmoid(logits))
    return x + proj * jnp.concatenate(gate, axis=1)


def _combine_ple_kernel(cur0_ref, cur1_ref, nxt0_ref, nxt1_ref, x_ref, gates_ref, p_ref, g_ref, wg_ref,
                        wp_ref, fg_ref, ys_hbm, o_ref, a_ref, b_ref, sem):
    i = pl.program_id(0)
    last = pl.num_programs(0) - 1
    tm = a_ref.shape[1] // SUBLANES
    groups = tm // DMA_UNROLL
    cur_ref, nxt_ref = (cur0_ref, cur1_ref), (nxt0_ref, nxt1_ref)

    def issue_group(idx_ref, offset, slot, base):
        idx = [[ref[0, 0, offset + base + q] for q in range(DMA_UNROLL)] for ref in idx_ref]
        for u in range(DMA_UNROLL):
            line = (base + u) * SUBLANES
            pltpu.make_async_copy(_tile_at(ys_hbm, idx[0][u]), _tile_at(a_ref.at[slot], line),
                                  sem.at[slot]).start(priority=0)
            pltpu.make_async_copy(_tile_at(ys_hbm, idx[1][u]), _tile_at(b_ref.at[slot], line),
                                  sem.at[slot]).start(priority=1)

    def wait(slot):
        whole = ys_hbm.at[pl.ds(0, tm * SUBLANES)]
        pltpu.make_async_copy(whole, a_ref.at[slot], sem.at[slot]).wait()
        pltpu.make_async_copy(whole, b_ref.at[slot], sem.at[slot]).wait()

    def compute(slot, idx_ref, offset):
        rows = slice(slot * tm, (slot + 1) * tm)
        g1, g2 = gates_ref[rows, :LANES], gates_ref[rows, LANES:]
        y = jnp.concatenate(
            [g1 * a_ref[slot, pl.ds(c, tm, stride=SUBLANES), :]
             + g2 * b_ref[slot, pl.ds(c, tm, stride=SUBLANES), :] for c in range(SUBLANES)], axis=1)
        chunks = wg_ref.shape[1] // MXU_COLS

        def between(c):
            for grp in range(c * groups // chunks, (c + 1) * groups // chunks):
                issue_group(idx_ref, offset, 1 - slot, grp * DMA_UNROLL)

        x = _ple_rows(x_ref[rows, :] + y, p_ref[rows, :], g_ref, wg_ref, wp_ref, between)
        o_ref[rows, :] = _rms(x, fg_ref[...])

    @pl.when(i == 0)
    def _():
        lax.fori_loop(0, groups, lambda r, c: (issue_group(cur_ref, 0, 0, r * DMA_UNROLL), c)[1], 0)

    wait(0)
    compute(0, cur_ref, tm)
    wait(1)
    compute(1, nxt_ref, 0)

    @pl.when(i == last)
    def _():
        wait(0)


def _combine_ple(x, gates, ys, dest_line, p, layer, g, wg, wp, fg):
    t, d = x.shape
    tm = TM_COMBINE
    steps = t // (2 * tm)
    row = lambda i: (i, 0)
    fixed = lambda i: (0, 0)
    idx = dest_line.reshape(2 * steps, 1, 2 * tm)
    idx_spec = lambda index: pl.BlockSpec((1, 1, 2 * tm), index, memory_space=pltpu.SMEM)
    nxt = lambda i: jnp.minimum(i + 1, steps - 1)
    return pl.pallas_call(
        _combine_ple_kernel,
        grid=(steps,),
        in_specs=[idx_spec(lambda i: (i, 0, 0)), idx_spec(lambda i: (steps + i, 0, 0)),
                  idx_spec(lambda i: (nxt(i), 0, 0)), idx_spec(lambda i: (steps + nxt(i), 0, 0)),
                  pl.BlockSpec((2 * tm, d), row),
                  pl.BlockSpec((2 * tm, 2 * LANES), row),
                  pl.BlockSpec((None, 2 * tm, p.shape[2]), lambda i: (layer, i, 0)),
                  pl.BlockSpec((1, d), fixed),
                  pl.BlockSpec(wg.shape, fixed),
                  pl.BlockSpec(wp.shape, fixed),
                  pl.BlockSpec((1, d), fixed),
                  pl.BlockSpec(memory_space=pl.ANY)],
        out_specs=pl.BlockSpec((2 * tm, d), row),
        out_shape=jax.ShapeDtypeStruct((t, d), F32),
        scratch_shapes=[pltpu.VMEM((2, tm * SUBLANES, LANES), F32),
                        pltpu.VMEM((2, tm * SUBLANES, LANES), F32),
                        pltpu.SemaphoreType.DMA((2,))],
        compiler_params=_params("arbitrary"),
        name="combine_ple_final",
    )(idx, idx, idx, idx, x, gates, p, g, wg, wp, fg, ys)


def _routing_tables(route, counts_f):
    t = route.shape[1]
    picked = route[0:2].astype(jnp.int32)
    rank = route[2:4].astype(jnp.int32)
    counts = counts_f[0, :N_EXPERTS].astype(jnp.int32)
    tiles_per = (counts + TM - 1) // TM
    tile_end = jnp.cumsum(tiles_per)
    tile_start = tile_end - tiles_per
    first_slot = jnp.zeros_like(picked)
    for e in range(N_EXPERTS):
        first_slot = jnp.where(picked == e, tile_start[e] * TM, first_slot)
    dest = first_slot + rank
    n_tiles = (2 * t) // TM + N_EXPERTS
    n_used = tile_end[-1]
    tile_ids = jnp.minimum(jnp.arange(n_tiles), n_used - 1)
    tile_expert = jnp.minimum(jnp.sum((tile_ids[:, None] >= tile_end[None, :]).astype(jnp.int32), axis=1),
                              N_EXPERTS - 1)
    pad_sizes = jnp.concatenate([tiles_per * TM - counts, ((n_tiles - n_used) * TM).reshape(1)])
    pad_end = jnp.cumsum(pad_sizes)
    seg_first = jnp.concatenate([tile_start * TM + counts, (n_used * TM).reshape(1)])
    k = jnp.arange(N_EXPERTS * TM)
    seg = jnp.sum((k[:, None] >= pad_end[None, :]).astype(jnp.int32), axis=1)
    pad = seg_first[seg] + k - (pad_end - pad_sizes)[seg]
    to_line = lambda v: (v * SUBLANES).astype(jnp.int32)
    return (to_line(dest), to_line(pad), tile_expert.astype(jnp.int32),
            n_used.reshape(1).astype(jnp.int32), n_tiles)


def kernel(x, p, attn_norm, ffn_norm, a_w_qkv, a_w_o, kv_norm, kv_w, b_w_q, b_sinks, b_w_o,
           dense_w1, dense_w3, dense_w2, moe_router, moe_w1, moe_w3, moe_w2,
           ple_norm, ple_w_gate, ple_w_proj, final_norm):
    batch, seq, d = x.shape
    t = batch * seq
    n_heads = d // HEAD_DIM
    n_groups = len(A_CONFIGS)
    assert d == SUBLANES * LANES and seq % TM == 0 and t % DISPATCH_ROWS == 0
    xr = x.reshape(t, d)
    pr = p.reshape(p.shape[0], t, p.shape[3])
    cos, sin = _rope_tables(np.arange(seq))
    row = lambda v: v.reshape(1, -1)

    w = a_w_qkv[0].astype(BF16).reshape(d, n_groups, 3, d)
    coef_a, coef_b = _rope_coeffs(cos, sin)
    cast_jobs = ((moe_w1[0], dense_w1[0], a_w_o[0], b_w_q[0]),
                 (moe_w3[0], dense_w3[0], ple_w_gate),
                 (moe_w2[0], b_w_o[0]))
    def proj_part(g):
        w_g = jnp.concatenate([_pair_columns(w[:, g, 0], True), _pair_columns(w[:, g, 1], True),
                               w[:, g, 2]], axis=1)
        return _group_proj_part(xr, row(attn_norm[0]), w_g, coef_a, coef_b, batch, seq, A_CONFIGS[g][1],
                                cast_jobs[g])

    outs, stats, cast = [], [], []
    (qkv, *done), = _run_parts([proj_part(0)], t // TM, "group_proj_d1")
    cast.append(done)
    for g, (_, dil) in enumerate(A_CONFIGS):
        attn, steps = _dilated_attn_part(qkv, batch, seq, g)
        if g + 1 < n_groups:
            assert steps == t // TM
            (o_g, st_g), (qkv, *done) = _run_parts([attn, proj_part(g + 1)], steps, f"attn_g{g}_proj_g{g + 1}")
            cast.append(done)
        else:
            (o_g, st_g), = _run_parts([attn], steps, f"dilated_attn_g{g}")
        outs.append(o_g.reshape(batch, dil, seq // dil, d))
        stats.append(st_g.reshape(batch, dil, seq // dil, LANES))
    (moe1, dense1, wo0, wq1), (moe3, dense3, ple_gate), (moe2, wo1) = [
        [c.reshape(s.shape) for c, s in zip(done, srcs)] for done, srcs in zip(cast, cast_jobs)]
    head_of_col = jnp.arange(d) // HEAD_DIM
    expand = (jnp.arange(LANES)[:, None] == head_of_col[None, :]).astype(BF16)
    xr = _merge_ffn(xr, outs, stats, expand, wo0, row(ffn_norm[0]), dense1, dense3,
                    dense_w2[0].astype(BF16), batch, seq)

    half = n_heads // 2
    kv_cols = B_KV_HEADS * HEAD_DIM
    kv_bf = kv_w.astype(BF16)
    w_kv = jnp.concatenate([_pair_columns(kv_bf[:, :kv_cols], True), kv_bf[:, kv_cols:]], axis=1)
    xr, q1, kv1 = _ple_qkv1(xr, pr, 0, row(ple_norm[0]), ple_gate[0], ple_w_proj[0].astype(BF16),
                            row(attn_norm[1]), row(kv_norm), _pair_columns(wq1, False), w_kv, cos, sin)
    w_o1 = wo1.reshape(2, half, HEAD_DIM, d).transpose(1, 0, 2, 3).reshape(d, d)
    wr = jnp.pad(moe_router[0], ((0, 0), (0, LANES - N_EXPERTS))).astype(BF16)
    xr, x_tiles, route, gates, counts = _swa_wo_router(q1, kv1, b_sinks[0], xr, w_o1, row(ffn_norm[1]),
                                                       wr, seq)

    dest_line, pad_line, tile_expert, n_used, n_tiles = _routing_tables(route, counts)
    xs = _dispatch(x_tiles, dest_line, pad_line, n_tiles * TM)
    ys = _moe_experts(xs, row(ffn_norm[1]), moe1, moe3, moe2, tile_expert, n_used, n_tiles)
    xr = _combine_ple(xr, gates, ys, dest_line, pr, 1, row(ple_norm[1]), ple_gate[1],
                      ple_w_proj[1].astype(BF16), row(final_norm))
    return xr.reshape(batch, seq, d)
```
